```python
import jax, jax.numpy as jnp
from jax import lax
import numpy as np

D_MODEL = 1024
BATCH = 4
SEQ = 8192
DEPTH = 1
DEC_BATCH = 32
DEC_SEQ = 32
PAST_LEN = 2048

CHUNK = 64
N_HEADS_A = 16
N_KV_A = 2
HEAD_DIM_A = 64
GROUP_A = N_HEADS_A // N_KV_A
WINDOW = 128
WINDOW_CHUNKS = WINDOW // CHUNK
KV_WIN = min(WINDOW, PAST_LEN)
N_HEADS_B = 4
DQK_B = 128
DV_B = 256
D_FF = 2816
CONV_W = 3
LN_EPS = 1e-5
ALPHA = (2 * DEPTH) ** 0.25
BETA = (8 * DEPTH) ** -0.25
Q_A = N_HEADS_A * HEAD_DIM_A
KV_A_W = N_KV_A * HEAD_DIM_A
QK_B = N_HEADS_B * DQK_B
V_B = N_HEADS_B * DV_B
Z_PARTS = (Q_A, KV_A_W, KV_A_W, QK_B, QK_B, V_B, N_HEADS_B, N_HEADS_B, V_B, D_MODEL, D_MODEL)
Z_WIDTH = int(sum(Z_PARTS))
Z_OFFSETS = tuple(int(v) for v in np.cumsum(Z_PARTS)[:-1])

kernel_name = 'streaming_swa_mlstm_convffn_hybrid_step'


def layer_norm(x, w, b):
    xf = x.astype(jnp.float32)
    mu = jnp.mean(xf, -1, keepdims=True)
    var = jnp.mean(jnp.square(xf - mu), -1, keepdims=True)
    y = (xf - mu) * lax.rsqrt(var + LN_EPS)
    return (y * w.astype(jnp.float32) + b.astype(jnp.float32)).astype(x.dtype)


def head_norm(h, w):
    N, T = h.shape[:2]
    hf = h.astype(jnp.float32)
    mu = jnp.mean(hf, -1, keepdims=True)
    var = jnp.mean(jnp.square(hf - mu), -1, keepdims=True)
    y = ((hf - mu) * lax.rsqrt(var + LN_EPS)).reshape(N, T, V_B)
    return (y * w.astype(jnp.float32)).astype(h.dtype)


def alibi_slopes():
    return 2.0 ** (-8.0 * jnp.arange(1, N_HEADS_A + 1, dtype=jnp.float32) / N_HEADS_A)


def banded_attention(qb, kb, vb, qpos, kpos, sinks):
    s = jnp.einsum('nbqhgd,nbshd->nbhgqs', qb.astype(jnp.float32), kb.astype(jnp.float32)) * (HEAD_DIM_A ** -0.5)
    qc = qpos // CHUNK
    kc = kpos // CHUNK
    visible = ((kpos[:, None, :] >= 0) & (kc[:, None, :] <= qc[:, :, None])
               & (kc[:, None, :] >= qc[:, :, None] - WINDOW_CHUNKS))
    dist = jnp.abs(qpos[:, :, None] - kpos[:, None, :]).astype(jnp.float32)
    slopes = alibi_slopes().reshape(N_KV_A, GROUP_A)
    s = s - slopes[None, None, :, :, None, None] * dist[None, :, None, None]
    s = jnp.where(visible[None, :, None, None], s, -jnp.inf)
    sink = sinks.astype(jnp.float32).reshape(1, 1, N_KV_A, GROUP_A, 1, 1)
    mx = jnp.maximum(jnp.max(s, -1, keepdims=True), sink)
    p = jnp.exp(s - mx)
    p = p / (jnp.sum(p, -1, keepdims=True) + jnp.exp(sink - mx))
    o = jnp.einsum('nbhgqs,nbshd->nbqhgd', p, vb.astype(jnp.float32))
    return o.astype(qb.dtype)


def local_attention(q, k, v, k_cache, v_cache, sinks):
    N, T = q.shape[:2]
    if k_cache is None:
        nb = T // CHUNK
        pad = WINDOW_CHUNKS * CHUNK
        span = (WINDOW_CHUNKS + 1) * CHUNK
        qb = q.reshape(N, nb, CHUNK, N_KV_A, GROUP_A, HEAD_DIM_A)

        def bands(a):
            ap = jnp.pad(a, ((0, 0), (pad, 0), (0, 0), (0, 0)))
            ap = ap.reshape(N, nb + WINDOW_CHUNKS, CHUNK, N_KV_A, HEAD_DIM_A)
            return jnp.concatenate([ap[:, j:j + nb] for j in range(WINDOW_CHUNKS + 1)], axis=2)

        kb, vb = bands(k), bands(v)
        qpos = jnp.arange(T).reshape(nb, CHUNK)
        kpos = jnp.arange(nb)[:, None] * CHUNK - pad + jnp.arange(span)[None, :]
        k_win, v_win = k[:, T - KV_WIN:], v[:, T - KV_WIN:]
    else:
        kk = jnp.concatenate([k_cache.astype(k.dtype), k], axis=1)
        vv = jnp.concatenate([v_cache.astype(v.dtype), v], axis=1)
        qb = q.reshape(N, 1, T, N_KV_A, GROUP_A, HEAD_DIM_A)
        kb, vb = kk[:, None], vv[:, None]
        qpos = (PAST_LEN + jnp.arange(T))[None]
        kpos = (PAST_LEN - KV_WIN + jnp.arange(KV_WIN + T))[None]
        k_win, v_win = kk[:, T:], vv[:, T:]
    o = banded_attention(qb, kb, vb, qpos, kpos, sinks)
    return o.reshape(N, T, Q_A), k_win, v_win


def mlstm_chunkwise(q, k, v, i_pre, logf, C0, n0, m0, chunk):
    N, T = q.shape[:2]
    nc = T // chunk

    def to_chunks(a):
        a = a.astype(jnp.float32).reshape((N, nc, chunk) + a.shape[2:])
        return jnp.swapaxes(jnp.moveaxis(a, 1, 0), 2, 3)

    xs = (to_chunks(q), to_chunks(k), to_chunks(v), to_chunks(i_pre), to_chunks(logf))
    causal = jnp.tril(jnp.ones((chunk, chunk), dtype=bool))

    def step(carry, inp):
        C, n, m = carry
        qc, kc, vc, ic, fc = inp
        b = jnp.cumsum(fc, axis=-1)
        dmat = jnp.where(causal, b[..., :, None] - b[..., None, :] + ic[..., None, :], -jnp.inf)
        inter = b + m[..., None]
        m_t = jnp.maximum(jnp.max(dmat, -1), inter)
        w = jnp.exp(dmat - m_t[..., None])
        g = jnp.exp(inter - m_t)
        s = jnp.einsum('nhld,nhsd->nhls', qc, kc) * w
        num = jnp.einsum('nhls,nhsv->nhlv', s, vc) + g[..., None] * jnp.einsum('nhvd,nhld->nhlv', C, qc)
        den = jnp.sum(s, -1) + g * jnp.einsum('nhd,nhld->nhl', n, qc)
        h = num / jnp.maximum(jnp.abs(den), jnp.exp(-m_t))[..., None]
        m_new = m_t[..., -1]
        wk = jnp.exp(b[..., -1:] - b + ic - m_new[..., None])
        decay = jnp.exp(b[..., -1] + m - m_new)
        C_new = decay[..., None, None] * C + jnp.einsum('nhl,nhlv,nhld->nhvd', wk, vc, kc)
        n_new = decay[..., None] * n + jnp.einsum('nhl,nhld->nhd', wk, kc)
        return (C_new, n_new, m_new), h

    init = (C0.astype(jnp.float32), n0.astype(jnp.float32), m0.astype(jnp.float32))
    (C, n, m), h = lax.scan(step, init, xs)
    h = jnp.moveaxis(jnp.swapaxes(h, 2, 3), 0, 1).reshape(N, T, N_HEADS_B, DV_B)
    return h.astype(q.dtype), C, n, m


def conv_ffn(h, conv_buf, w_up, b_up, conv_w, conv_b, w_down):
    T = h.shape[1]
    u = h @ w_up + b_up
    ctx = jnp.concatenate([conv_buf.astype(u.dtype), u], axis=1)
    uc = conv_b + sum(conv_w[j] * ctx[:, j:j + T] for j in range(CONV_W))
    a, g = jnp.split(uc, 2, axis=-1)
    return (a * jax.nn.gelu(g)) @ w_down, ctx[:, T:]


def trunk_layer(x, c, k_cache, v_cache, C0, n0, m0, conv_buf, mlstm_chunk, prm):
    N, T, _ = x.shape
    mod = jax.nn.silu(c) @ prm['w_ada'] + prm['b_ada']
    sh1, sc1, g1, sh2, sc2, g2 = jnp.split(mod[:, None, :], 6, axis=-1)
    h = x * (1 + sc1) + sh1
    z = h @ prm['w_in']
    qa, ka, va, qm, km, vm, ig, fg, og, ga, gb = jnp.split(z, Z_OFFSETS, axis=-1)
    ya, k_win, v_win = local_attention(qa.reshape(N, T, N_HEADS_A, HEAD_DIM_A),
                                       ka.reshape(N, T, N_KV_A, HEAD_DIM_A),
                                       va.reshape(N, T, N_KV_A, HEAD_DIM_A),
                                       k_cache, v_cache, prm['attn_sinks'])
    i_pre = (ig + prm['b_igate']).astype(jnp.float32)
    logf = jax.nn.log_sigmoid((fg + prm['b_fgate']).astype(jnp.float32))
    hm, C, n, m = mlstm_chunkwise(qm.reshape(N, T, N_HEADS_B, DQK_B),
                                  km.reshape(N, T, N_HEADS_B, DQK_B) * (DQK_B ** -0.5),
                                  vm.reshape(N, T, N_HEADS_B, DV_B),
                                  i_pre, logf, C0, n0, m0, mlstm_chunk)
    yb = jax.nn.sigmoid(og) * head_norm(hm, prm['mlstm_norm_w'])
    merged = jax.nn.sigmoid(ga) * (ya @ prm['w_proj_a']) + jax.nn.sigmoid(gb) * (yb @ prm['w_proj_b'])
    x = layer_norm(ALPHA * x + g1 * (merged @ prm['w_out']), prm['ln1_w'], prm['ln1_b'])
    h = x * (1 + sc2) + sh2
    f, conv_new = conv_ffn(h, conv_buf, prm['w_up'], prm['b_up'], prm['conv_w'], prm['conv_b'], prm['w_down'])
    x = layer_norm(ALPHA * x + g2 * f, prm['ln2_w'], prm['ln2_b'])
    dt = x.dtype
    return x, k_win, v_win, C.astype(dt), n.astype(dt), m.astype(dt), conv_new


def setup_inputs(seed: int = 0) -> dict:
    key = jax.random.key(seed)
    ks = jax.random.split(key, 32)
    L, D = DEPTH, D_MODEL

    def nrm(k, shape, s):
        return jax.random.normal(k, shape, jnp.float32) * s

    return {
        'x_prompt': nrm(ks[0], (BATCH, SEQ, D), 1.0),
        'x_sample': nrm(ks[1], (DEC_BATCH, DEC_SEQ, D), 1.0),
        'cache_attn_k': nrm(ks[2], (L, DEC_BATCH, KV_WIN, N_KV_A, HEAD_DIM_A), 1.0),
        'cache_attn_v': nrm(ks[3], (L, DEC_BATCH, KV_WIN, N_KV_A, HEAD_DIM_A), 1.0),
        'state_mlstm_C': nrm(ks[4], (L, DEC_BATCH, N_HEADS_B, DV_B, DQK_B), 0.1),
        'state_mlstm_n': nrm(ks[5], (L, DEC_BATCH, N_HEADS_B, DQK_B), 0.1),
        'state_mlstm_m': nrm(ks[6], (L, DEC_BATCH, N_HEADS_B), 0.5),
        'state_ffn_conv': nrm(ks[7], (L, DEC_BATCH, CONV_W - 1, 2 * D_FF), 1.0),
        'c_prompt': nrm(ks[8], (BATCH, D), 1.0),
        'c_sample': nrm(ks[9], (DEC_BATCH, D), 1.0),
        'w_ada': nrm(ks[10], (L, D, 6 * D), 0.5 * D ** -0.5),
        'b_ada': nrm(ks[11], (L, 6 * D), 0.01),
        'w_in': nrm(ks[12], (L, D, Z_WIDTH), D ** -0.5),
        'b_igate': nrm(ks[13], (L, N_HEADS_B), 0.1),
        'b_fgate': jnp.linspace(3.0, 6.0, N_HEADS_B, dtype=jnp.float32)[None] + nrm(ks[14], (L, N_HEADS_B), 0.1),
        'attn_sinks': nrm(ks[15], (L, N_HEADS_A), 0.5),
        'mlstm_norm_w': 1.0 + nrm(ks[16], (L, V_B), 0.02),
        'w_proj_a': nrm(ks[17], (L, Q_A, D), Q_A ** -0.5),
        'w_proj_b': nrm(ks[18], (L, V_B, D), V_B ** -0.5),
        'w_out': nrm(ks[19], (L, D, D), BETA * D ** -0.5),
        'ln1_w': 1.0 + nrm(ks[20], (L, D), 0.02),
        'ln1_b': nrm(ks[21], (L, D), 0.02),
        'w_up': nrm(ks[22], (L, D, 2 * D_FF), D ** -0.5),
        'b_up': nrm(ks[23], (L, 2 * D_FF), 0.01),
        'conv_w': nrm(ks[24], (L, CONV_W, 2 * D_FF), CONV_W ** -0.5),
        'conv_b': nrm(ks[25], (L, 2 * D_FF), 0.01),
        'w_down': nrm(ks[26], (L, D_FF, D), BETA * D_FF ** -0.5),
        'ln2_w': 1.0 + nrm(ks[27], (L, D), 0.02),
        'ln2_b': nrm(ks[28], (L, D), 0.02),
    }


def reference(x_prompt, x_sample, cache_attn_k, cache_attn_v, state_mlstm_C, state_mlstm_n, state_mlstm_m,
              state_ffn_conv, c_prompt, c_sample, w_ada, b_ada, w_in, b_igate, b_fgate, attn_sinks, mlstm_norm_w,
              w_proj_a, w_proj_b, w_out, ln1_w, ln1_b, w_up, b_up, conv_w, conv_b, w_down, ln2_w, ln2_b):
    Bp = x_prompt.shape[0]
    dec_t = x_sample.shape[1]
    y_prompt, y_sample = x_prompt, x_sample
    p_out, s_out = [], []
    for l in range(DEPTH):
        prm = dict(w_ada=w_ada[l], b_ada=b_ada[l], w_in=w_in[l], b_igate=b_igate[l], b_fgate=b_fgate[l],
                   attn_sinks=attn_sinks[l], mlstm_norm_w=mlstm_norm_w[l], w_proj_a=w_proj_a[l],
                   w_proj_b=w_proj_b[l], w_out=w_out[l], ln1_w=ln1_w[l], ln1_b=ln1_b[l], w_up=w_up[l],
                   b_up=b_up[l], conv_w=conv_w[l], conv_b=conv_b[l], w_down=w_down[l], ln2_w=ln2_w[l],
                   ln2_b=ln2_b[l])
        C0 = jnp.zeros((Bp, N_HEADS_B, DV_B, DQK_B), jnp.float32)
        n0 = jnp.zeros((Bp, N_HEADS_B, DQK_B), jnp.float32)
        m0 = jnp.zeros((Bp, N_HEADS_B), jnp.float32)
        conv0 = jnp.zeros((Bp, CONV_W - 1, 2 * D_FF), x_prompt.dtype)
        res_p = trunk_layer(y_prompt, c_prompt, None, None, C0, n0, m0, conv0, CHUNK, prm)
        y_prompt = res_p[0]
        p_out.append(res_p[1:])
        res_s = trunk_layer(y_sample, c_sample, cache_attn_k[l], cache_attn_v[l], state_mlstm_C[l],
                            state_mlstm_n[l], state_mlstm_m[l], state_ffn_conv[l], dec_t, prm)
        y_sample = res_s[0]
        s_out.append(res_s[1:])
    p_k, p_v, p_C, p_n, p_m, p_conv = [jnp.stack([o[i] for o in p_out]) for i in range(6)]
    s_k, s_v, s_C, s_n, s_m, s_conv = [jnp.stack([o[i] for o in s_out]) for i in range(6)]
    return (y_prompt, y_sample, p_k, p_v, p_C, p_n, p_m, p_conv, s_k, s_v, s_C, s_n, s_m, s_conv)
```

```python
import functools

import numpy as np
import jax
import jax.numpy as jnp
from jax import lax
from jax.experimental import pallas as pl
from jax.experimental.pallas import tpu as pltpu

F32 = jnp.float32
BF16 = jnp.bfloat16

D_MODEL = 1024
DEPTH = 1
PAST_LEN = 2048
CHUNK = 64
N_HEADS_A = 16
N_KV_A = 2
HEAD_DIM_A = 64
GROUP_A = N_HEADS_A // N_KV_A
WINDOW = 128
WINDOW_CHUNKS = WINDOW // CHUNK
KV_WIN = min(WINDOW, PAST_LEN)
N_HEADS_B = 4
DQK_B = 128
DV_B = 256
D_FF = 2816
CONV_W = 3
LN_EPS = 1e-5
ALPHA = (2 * DEPTH) ** 0.25
Q_A = N_HEADS_A * HEAD_DIM_A
KV_A_W = N_KV_A * HEAD_DIM_A
QK_B = N_HEADS_B * DQK_B
V_B = N_HEADS_B * DV_B
Z_PARTS = (Q_A, KV_A_W, KV_A_W, QK_B, QK_B, V_B, N_HEADS_B, N_HEADS_B, V_B, D_MODEL, D_MODEL)
Z_OFF = tuple(int(v) for v in np.cumsum((0,) + Z_PARTS))

M_QA = 0
M_KA = M_QA + Q_A
M_VA = M_KA + KV_A_W
M_QM = M_VA + KV_A_W
M_KM = M_QM + QK_B
M_VM = M_KM + QK_B
M_OG = M_VM + V_B
M_GA = M_OG + V_B
M_GB = M_GA + D_MODEL
M_END = M_GB + D_MODEL

LANES = 128
ATT_QB = 128
ATT_W = 256
FFN_CHUNKS = 2


def _bdot(a, b):
    return jnp.dot(a.astype(BF16), b.astype(BF16), preferred_element_type=F32)


def _bdot_nt(a, b):
    return lax.dot_general(a.astype(BF16), b.astype(BF16), (((1,), (1,)), ((), ())),
                           preferred_element_type=F32)


def _bdot_tn(a, b):
    return lax.dot_general(a.astype(BF16), b.astype(BF16), (((0,), (0,)), ((), ())),
                           preferred_element_type=F32)


def _const_spec(shape):
    nd = len(shape)
    return pl.BlockSpec(shape, lambda *_: (0,) * nd, pipeline_mode=pl.Buffered(1))


def _layer_norm(x, w, b):
    mu = jnp.mean(x, -1, keepdims=True)
    xc = x - mu
    var = jnp.mean(jnp.square(xc), -1, keepdims=True)
    return xc * lax.rsqrt(var + LN_EPS) * w + b


def _mod_kernel(c_ref, w_ref, b_ref, o_ref):
    c = c_ref[...]
    s = c * jax.nn.sigmoid(c)
    o_ref[...] = _bdot(s, w_ref[...]) + b_ref[...]


def _mod_call(c_all, w_ada, b_ada):
    n = c_all.shape[0]
    tn = 1536
    return pl.pallas_call(
        _mod_kernel,
        grid=(6 * D_MODEL // tn,),
        in_specs=[pl.BlockSpec((n, D_MODEL), lambda j: (0, 0)),
                  pl.BlockSpec((D_MODEL, tn), lambda j: (0, j)),
                  pl.BlockSpec((1, tn), lambda j: (0, j))],
        out_specs=pl.BlockSpec((n, tn), lambda j: (0, j)),
        out_shape=jax.ShapeDtypeStruct((n, 6 * D_MODEL), F32),
        name="mod",
    )(c_all, w_ada, b_ada)


def _log_sigmoid(x):
    return jnp.minimum(x, 0.0) - jnp.log1p(jnp.exp(-jnp.abs(x)))


def _inproj_kernel(x_ref, mod_ref, w_ref, wi_ref, wf_ref, bi_ref, bf_ref,
                   qa_ref, ka_ref, va_ref, qm_ref, km_ref, vm_ref, og_ref, ga_ref, gb_ref,
                   a_ref, b_ref, *, S, R, L):
    SR = S * R
    sh = mod_ref[:, 0:1, :]
    sc = mod_ref[:, 1:2, :]
    h = (x_ref[...] * (1.0 + sc) + sh).reshape(SR, D_MODEL).astype(BF16)

    def proj(lo, hi):
        return jnp.dot(h, w_ref[:, lo:hi], preferred_element_type=F32)

    qa_ref[...] = (proj(M_QA, M_KA) * (HEAD_DIM_A ** -0.5)).astype(BF16).reshape(S, R, Q_A)
    ka_ref[...] = proj(M_KA, M_VA).reshape(S, R, KV_A_W)
    va_ref[...] = proj(M_VA, M_QM).reshape(S, R, KV_A_W)
    qm_ref[...] = proj(M_QM, M_KM).astype(BF16).reshape(S, R, QK_B)
    km_ref[...] = (proj(M_KM, M_VM) * (DQK_B ** -0.5)).astype(BF16).reshape(S, R, QK_B)
    vm_ref[...] = proj(M_VM, M_OG).astype(BF16).reshape(S, R, V_B)
    og_ref[...] = proj(M_OG, M_GA).reshape(S, R, V_B)
    ga_ref[...] = proj(M_GA, M_GB).reshape(S, R, D_MODEL)
    gb_ref[...] = proj(M_GB, M_END).reshape(S, R, D_MODEL)

    i_pre = _bdot_nt(wi_ref[...], h) + bi_ref[...]
    logf = _log_sigmoid(_bdot_nt(wf_ref[...], h) + bf_ref[...])
    pos = lax.broadcasted_iota(jnp.int32, (8, SR), 1) % L
    b = logf
    step = 1
    while step < L:
        b = b + jnp.where(pos >= step, pltpu.roll(b, step, axis=1), 0.0)
        step *= 2
    a_ref[...] = i_pre - b
    b_ref[...] = b


def _inproj_call(x, mod, w_main, w_i, w_f, b_i, b_f, *, S, R, L):
    N, T, _ = x.shape
    nsb, nrb = N // S, T // R
    SR = S * R

    def tok(c):
        return pl.BlockSpec((S, R, c), lambda i, j: (i, j, 0))

    def tok_shape(c, dt):
        return jax.ShapeDtypeStruct((N, T, c), dt)

    gate_spec = pl.BlockSpec((8, SR), lambda i, j: (0, i * nrb + j))
    gate_shape = jax.ShapeDtypeStruct((8, N * T), F32)
    return pl.pallas_call(
        functools.partial(_inproj_kernel, S=S, R=R, L=L),
        grid=(nsb, nrb),
        in_specs=[tok(D_MODEL),
                  pl.BlockSpec((S, 6, D_MODEL), lambda i, j: (i, 0, 0)),
                  _const_spec((D_MODEL, M_END)),
                  _const_spec((8, D_MODEL)), _const_spec((8, D_MODEL)),
                  _const_spec((8, 1)), _const_spec((8, 1))],
        out_specs=[tok(Q_A), tok(KV_A_W), tok(KV_A_W), tok(QK_B), tok(QK_B), tok(V_B),
                   tok(V_B), tok(D_MODEL), tok(D_MODEL), gate_spec, gate_spec],
        out_shape=[tok_shape(Q_A, BF16), tok_shape(KV_A_W, F32), tok_shape(KV_A_W, F32),
                   tok_shape(QK_B, BF16), tok_shape(QK_B, BF16), tok_shape(V_B, BF16),
                   tok_shape(V_B, F32), tok_shape(D_MODEL, F32), tok_shape(D_MODEL, F32),
                   gate_shape, gate_shape],
        compiler_params=pltpu.CompilerParams(dimension_semantics=("parallel", "parallel")),
        name="inproj",
    )(x, mod, w_main, w_i, w_f, b_i, b_f)


def _alibi_slopes():
    return 2.0 ** (-8.0 * np.arange(1, N_HEADS_A + 1, dtype=np.float64) / N_HEADS_A)


def _attn_bias(qpos, kpos, kvalid):
    qpos = np.asarray(qpos)[:, None]
    kpos = np.asarray(kpos)[None, :]
    qc, kc = qpos // CHUNK, kpos // CHUNK
    visible = (kpos >= 0) & (kc <= qc) & (kc >= qc - WINDOW_CHUNKS) & np.asarray(kvalid)[None, :]
    dist = np.abs(qpos - kpos).astype(np.float64)
    slopes = _alibi_slopes()
    out = np.zeros((N_KV_A, GROUP_A // 2, qpos.shape[0], 2, kpos.shape[1]), np.float32)
    for kv in range(N_KV_A):
        for p in range(GROUP_A // 2):
            for half in range(2):
                hd = kv * GROUP_A + 2 * p + half
                out[kv, p, :, half, :] = np.where(visible, -slopes[hd] * dist, -np.inf)
    return out.reshape(N_KV_A, (GROUP_A // 2) * qpos.shape[0], 2 * kpos.shape[1])


def _attn_core(q, kwin, vwin, bias_of, sinks_ref, store):
    Mq = q.shape[0]
    W = kwin.shape[0]
    npair = GROUP_A // 2
    lane = lax.broadcasted_iota(jnp.int32, (npair * Mq, LANES), 1)
    for kv in range(N_KV_A):
        kk = kwin[:, kv * HEAD_DIM_A:(kv + 1) * HEAD_DIM_A].astype(BF16)
        vv = vwin[:, kv * HEAD_DIM_A:(kv + 1) * HEAD_DIM_A].astype(BF16)
        zero = jnp.zeros_like(kk)
        one = jnp.ones_like(vv)
        k2 = jnp.concatenate([jnp.concatenate([kk, zero], axis=1),
                              jnp.concatenate([zero, kk], axis=1)], axis=0)
        v_aug = jnp.concatenate([vv, one, one, vv], axis=1)
        qs = jnp.concatenate([q[:, (kv * npair + p) * LANES:(kv * npair + p + 1) * LANES]
                              for p in range(npair)], axis=0)
        s = _bdot_nt(qs, k2) + bias_of(kv)
        probs, sink_terms = [], []
        for half in range(2):
            sh = s[:, half * W:(half + 1) * W]
            sink = jnp.concatenate(
                [jnp.full((Mq, 1), sinks_ref[kv * GROUP_A + 2 * p + half], F32) for p in range(npair)], axis=0)
            mx = jnp.maximum(jnp.max(sh, axis=1, keepdims=True), sink)
            probs.append(jnp.exp(sh - mx).astype(BF16))
            sink_terms.append(jnp.exp(sink - mx))
        o = jnp.dot(jnp.concatenate(probs, axis=0), v_aug, preferred_element_type=F32)
        oe, oo = o[:npair * Mq], o[npair * Mq:]
        ye = oe[:, :LANES] / (oe[:, LANES:] + sink_terms[0])
        yo = oo[:, LANES:] / (oo[:, :LANES] + sink_terms[1])
        y = jnp.where(lane < HEAD_DIM_A, ye, yo).astype(BF16)
        for p in range(npair):
            store((kv * npair + p) * LANES, y[p * Mq:(p + 1) * Mq])


def _attn_prompt_kernel(sinks_ref, q_ref, kp_ref, kc_ref, vp_ref, vc_ref, bias_ref, o_ref):
    kwin = jnp.concatenate([kp_ref[0], kc_ref[0]], axis=0)
    vwin = jnp.concatenate([vp_ref[0], vc_ref[0]], axis=0)

    def store(col, val):
        o_ref[0, :, col:col + LANES] = val

    _attn_core(q_ref[0], kwin, vwin, lambda kv: bias_ref[0, kv], sinks_ref, store)


def _attn_prompt_call(qa, ka, va, sinks):
    N, T, _ = qa.shape
    nq = T // ATT_QB
    rel_q = ATT_QB + np.arange(ATT_QB)
    rel_k = np.arange(ATT_W)
    bias = np.stack([_attn_bias(rel_q, rel_k, rel_k >= ATT_QB),
                     _attn_bias(rel_q, rel_k, rel_k >= 0)])
    kv_prev = pl.BlockSpec((1, ATT_QB, KV_A_W), lambda n, j: (n, jnp.maximum(j - 1, 0), 0))
    kv_cur = pl.BlockSpec((1, ATT_QB, KV_A_W), lambda n, j: (n, j, 0))
    return pl.pallas_call(
        _attn_prompt_kernel,
        grid=(N, nq),
        in_specs=[pl.BlockSpec(memory_space=pltpu.SMEM),
                  pl.BlockSpec((1, ATT_QB, Q_A), lambda n, j: (n, j, 0)),
                  kv_prev, kv_cur, kv_prev, kv_cur,
                  pl.BlockSpec((1,) + bias.shape[1:], lambda n, j: (jnp.minimum(j, 1), 0, 0, 0))],
        out_specs=pl.BlockSpec((1, ATT_QB, Q_A), lambda n, j: (n, j, 0)),
        out_shape=jax.ShapeDtypeStruct((N, T, Q_A), BF16),
        compiler_params=pltpu.CompilerParams(dimension_semantics=("parallel", "parallel")),
        name="attn_prompt",
    )(sinks, qa, ka, ka, va, va, jnp.asarray(bias))


def _attn_sample_kernel(sinks_ref, q_ref, kn_ref, vn_ref, kc_ref, vc_ref, bias_ref,
                        o_ref, ko_ref, vo_ref, *, T):
    pad = jnp.zeros((ATT_W - KV_WIN - T, KV_A_W), F32)
    kwin = jnp.concatenate([kc_ref[0], kn_ref[0], pad], axis=0)
    vwin = jnp.concatenate([vc_ref[0], vn_ref[0], pad], axis=0)
    ko_ref[0] = kwin[T:T + KV_WIN]
    vo_ref[0] = vwin[T:T + KV_WIN]

    def store(col, val):
        o_ref[0, :, col:col + LANES] = val

    _attn_core(q_ref[0], kwin, vwin, lambda kv: bias_ref[kv], sinks_ref, store)


def _attn_sample_call(qa, ka, va, cache_k, cache_v, sinks):
    N, T, _ = qa.shape
    qpos = PAST_LEN + np.arange(T)
    kpos = PAST_LEN - KV_WIN + np.arange(ATT_W)
    bias = _attn_bias(qpos, kpos, np.arange(ATT_W) < KV_WIN + T)
    new = pl.BlockSpec((1, T, KV_A_W), lambda n: (n, 0, 0))
    cache = pl.BlockSpec((1, KV_WIN, KV_A_W), lambda n: (n, 0, 0))
    cache_shape = jax.ShapeDtypeStruct((N, KV_WIN, KV_A_W), F32)
    return pl.pallas_call(
        functools.partial(_attn_sample_kernel, T=T),
        grid=(N,),
        in_specs=[pl.BlockSpec(memory_space=pltpu.SMEM),
                  pl.BlockSpec((1, T, Q_A), lambda n: (n, 0, 0)),
                  new, new, cache, cache,
                  _const_spec(bias.shape)],
        out_specs=[pl.BlockSpec((1, T, Q_A), lambda n: (n, 0, 0)), cache, cache],
        out_shape=[jax.ShapeDtypeStruct((N, T, Q_A), BF16), cache_shape, cache_shape],
        compiler_params=pltpu.CompilerParams(dimension_semantics=("parallel",)),
        name="attn_sample",
    )(sinks, qa, ka, va, cache_k, cache_v, jnp.asarray(bias))


def _mlstm_kernel(q_ref, k_ref, v_ref, og_ref, a_ref, b_ref, nw_ref, c0_ref, n0_ref, m0_ref,
                  y_ref, c_ref, n_ref, m_ref, *, SB, CPS, L):
    @pl.when(pl.program_id(1) == 0)
    def _():
        c_ref[...] = c0_ref[...]
        n_ref[...] = n0_ref[...]
        m_ref[...] = m0_ref[...]

    row = lax.broadcasted_iota(jnp.int32, (L, L), 0)
    col = lax.broadcasted_iota(jnp.int32, (L, L), 1)
    causal = row >= col
    diag = row == col
    for sb in range(SB):
        for c in range(CPS):
            r0 = c * L
            g0 = (sb * CPS + c) * L
            for hd in range(N_HEADS_B):
                q = q_ref[sb, r0:r0 + L, hd * DQK_B:(hd + 1) * DQK_B]
                k = k_ref[sb, r0:r0 + L, hd * DQK_B:(hd + 1) * DQK_B]
                v = v_ref[sb, r0:r0 + L, hd * DV_B:(hd + 1) * DV_B]
                a_row = a_ref[hd:hd + 1, g0:g0 + L]
                b_row = b_ref[hd:hd + 1, g0:g0 + L]
                m_prev = m_ref[sb, hd]
                C = c_ref[sb, hd]
                n = n_ref[sb, hd]

                dm = jnp.where(causal, a_row, -jnp.inf)
                m_col = jnp.maximum(jnp.max(dm, axis=1, keepdims=True), m_prev)
                w = jnp.exp(dm - m_col)
                g = jnp.exp(m_prev - m_col)
                a_col = jnp.sum(jnp.where(diag, a_row, 0.0), axis=1, keepdims=True)
                b_col = jnp.sum(jnp.where(diag, b_row, 0.0), axis=1, keepdims=True)

                s = _bdot_nt(q, k) * w
                num = _bdot(s, v) + g * _bdot_nt(q, C)
                qn = jnp.sum(q.astype(F32) * n, axis=1, keepdims=True)
                den = jnp.sum(s, axis=1, keepdims=True) + g * qn
                hh = num / jnp.maximum(jnp.abs(den), jnp.exp(-(b_col + m_col)))

                mu = jnp.mean(hh, axis=1, keepdims=True)
                hc = hh - mu
                var = jnp.mean(jnp.square(hc), axis=1, keepdims=True)
                yn = hc * lax.rsqrt(var + LN_EPS) * nw_ref[:, hd * DV_B:(hd + 1) * DV_B]
                og = og_ref[sb, r0:r0 + L, hd * DV_B:(hd + 1) * DV_B]
                y_ref[sb, r0:r0 + L, hd * DV_B:(hd + 1) * DV_B] = (jax.nn.sigmoid(og) * yn).astype(BF16)

                b_last = b_row[:, L - 1:L]
                m_new = b_last + jnp.maximum(jnp.max(a_row, axis=1, keepdims=True), m_prev)
                wk = jnp.exp(a_col + b_last - m_new)
                decay = jnp.exp(b_last + m_prev - m_new)
                kw = k.astype(F32) * wk
                c_ref[sb, hd] = decay * C + _bdot_tn(v, kw)
                n_ref[sb, hd] = decay * n + jnp.sum(kw, axis=0, keepdims=True)
                m_ref[sb, hd] = m_new


def _mlstm_call(qm, km, vm, og, ga, gb, norm_w, c0, n0, m0, *, SB, CPS, L):
    N, T, _ = qm.shape
    rows = CPS * L
    nsteps = T // rows
    nsb = N // SB

    def tok(c):
        return pl.BlockSpec((SB, rows, c), lambda i, j: (i, j, 0))

    gate = pl.BlockSpec((8, SB * rows), lambda i, j: (0, i * nsteps + j))
    st_c = pl.BlockSpec((SB, N_HEADS_B, DV_B, DQK_B), lambda i, j: (i, 0, 0, 0))
    st_n = pl.BlockSpec((SB, N_HEADS_B, 1, DQK_B), lambda i, j: (i, 0, 0, 0))
    st_m = pl.BlockSpec((SB, N_HEADS_B, 1, 1), lambda i, j: (i, 0, 0, 0))
    return pl.pallas_call(
        functools.partial(_mlstm_kernel, SB=SB, CPS=CPS, L=L),
        grid=(nsb, nsteps),
        in_specs=[tok(QK_B), tok(QK_B), tok(V_B), tok(V_B), gate, gate, _const_spec((1, V_B)),
                  st_c, st_n, st_m],
        out_specs=[tok(V_B), st_c, st_n, st_m],
        out_shape=[jax.ShapeDtypeStruct((N, T, V_B), BF16),
                   jax.ShapeDtypeStruct((N, N_HEADS_B, DV_B, DQK_B), F32),
                   jax.ShapeDtypeStruct((N, N_HEADS_B, 1, DQK_B), F32),
                   jax.ShapeDtypeStruct((N, N_HEADS_B, 1, 1), F32)],
        compiler_params=pltpu.CompilerParams(dimension_semantics=("parallel", "arbitrary")),
        name="mlstm",
    )(qm, km, vm, og, ga, gb, norm_w, c0, n0, m0)


def _post_kernel(x_ref, mod_ref, ya_ref, yb_ref, ga_ref, gb_ref, cin_ref,
                 wpa_ref, wpb_ref, wo_ref, l1w_ref, l1b_ref, wup_ref, bup_ref, cw_ref, cb_ref,
                 wdn_ref, l2w_ref, l2b_ref,
                 y_ref, cout_ref, ctx_ref, act_ref, *, S, R):
    SR = S * R
    FC = D_FF // FFN_CHUNKS

    @pl.when(pl.program_id(1) == 0)
    def _():
        cout_ref[...] = cin_ref[...]

    g1 = mod_ref[:, 2:3, :]
    sh2 = mod_ref[:, 3:4, :]
    sc2 = mod_ref[:, 4:5, :]
    g2 = mod_ref[:, 5:6, :]
    x = x_ref[...]

    pa = jnp.dot(ya_ref[...].reshape(SR, Q_A), wpa_ref[...], preferred_element_type=F32)
    pb = jnp.dot(yb_ref[...].reshape(SR, V_B), wpb_ref[...], preferred_element_type=F32)
    merged = (jax.nn.sigmoid(ga_ref[...].reshape(SR, D_MODEL)) * pa
              + jax.nn.sigmoid(gb_ref[...].reshape(SR, D_MODEL)) * pb)
    mo = _bdot(merged, wo_ref[...]).reshape(S, R, D_MODEL)
    x1 = _layer_norm(ALPHA * x + g1 * mo, l1w_ref[...], l1b_ref[...])
    h2 = (x1 * (1.0 + sc2) + sh2).reshape(SR, D_MODEL).astype(BF16)

    def conv(lo):
        u = (jnp.dot(h2, wup_ref[:, lo:lo + FC], preferred_element_type=F32)
             + bup_ref[:, lo:lo + FC]).reshape(S, R, FC)
        ctx_ref[:, 6:8, :] = cout_ref[:, :, lo:lo + FC]
        ctx_ref[:, 8:, :] = u
        cout_ref[:, :, lo:lo + FC] = ctx_ref[:, R + 6:R + 8, :]
        return (cb_ref[:, lo:lo + FC]
                + cw_ref[0:1, lo:lo + FC] * ctx_ref[:, 6:R + 6, :]
                + cw_ref[1:2, lo:lo + FC] * ctx_ref[:, 7:R + 7, :]
                + cw_ref[2:3, lo:lo + FC] * u)

    for ci in range(FFN_CHUNKS):
        lo = ci * FC
        a = conv(lo)
        g = conv(D_FF + lo)
        act_ref[:, lo:lo + FC] = (a * jax.nn.gelu(g, approximate=True)).reshape(SR, FC).astype(BF16)

    f = jnp.dot(act_ref[...], wdn_ref[...], preferred_element_type=F32).reshape(S, R, D_MODEL)
    y_ref[...] = _layer_norm(ALPHA * x1 + g2 * f, l2w_ref[...], l2b_ref[...])


def _post_call(x, mod, ya, yb, ga, gb, conv_in, wpa, wpb, wo, l1w, l1b, wup, bup, cw, cb, wdn, l2w, l2b,
               *, S, R):
    N, T, _ = x.shape
    nsb, nrb = N // S, T // R
    FC = D_FF // FFN_CHUNKS

    def tok(c):
        return pl.BlockSpec((S, R, c), lambda i, j: (i, j, 0))

    conv_spec = pl.BlockSpec((S, CONV_W - 1, 2 * D_FF), lambda i, j: (i, 0, 0))
    return pl.pallas_call(
        functools.partial(_post_kernel, S=S, R=R),
        grid=(nsb, nrb),
        in_specs=[tok(D_MODEL), pl.BlockSpec((S, 6, D_MODEL), lambda i, j: (i, 0, 0)),
                  tok(Q_A), tok(V_B), tok(D_MODEL), tok(D_MODEL), conv_spec,
                  _const_spec((Q_A, D_MODEL)), _const_spec((V_B, D_MODEL)), _const_spec((D_MODEL, D_MODEL)),
                  _const_spec((1, D_MODEL)), _const_spec((1, D_MODEL)),
                  _const_spec((D_MODEL, 2 * D_FF)), _const_spec((1, 2 * D_FF)),
                  _const_spec((CONV_W, 2 * D_FF)), _const_spec((1, 2 * D_FF)),
                  _const_spec((D_FF, D_MODEL)), _const_spec((1, D_MODEL)), _const_spec((1, D_MODEL))],
        out_specs=[tok(D_MODEL), conv_spec],
        out_shape=[jax.ShapeDtypeStruct((N, T, D_MODEL), F32),
                   jax.ShapeDtypeStruct((N, CONV_W - 1, 2 * D_FF), F32)],
        scratch_shapes=[pltpu.VMEM((S, R + 8, FC), F32), pltpu.VMEM((S * R, D_FF), BF16)],
        compiler_params=pltpu.CompilerParams(dimension_semantics=("parallel", "arbitrary")),
        name="post",
    )(x, mod, ya, yb, ga, gb, conv_in, wpa, wpb, wo, l1w, l1b, wup, bup, cw, cb, wdn, l2w, l2b)


def _pad_rows(a, rows):
    return jnp.pad(a, ((0, rows - a.shape[0]),) + ((0, 0),) * (a.ndim - 1))


def kernel(x_prompt, x_sample, cache_attn_k, cache_attn_v, state_mlstm_C, state_mlstm_n, state_mlstm_m, state_ffn_conv, c_prompt, c_sample, w_ada, b_ada, w_in, b_igate, b_fgate, attn_sinks, mlstm_norm_w, w_proj_a, w_proj_b, w_out, ln1_w, ln1_b, w_up, b_up, conv_w, conv_b, w_down, ln2_w, ln2_b):
    Bp, Tp, D = x_prompt.shape
    Bs, Ts, _ = x_sample.shape
    l = 0

    mod = _mod_call(jnp.concatenate([c_prompt, c_sample], axis=0), w_ada[l], b_ada[l][None])
    mod_p = mod[:Bp].reshape(Bp, 6, D)
    mod_s = mod[Bp:].reshape(Bs, 6, D)

    w = w_in[l]
    o_i, o_f, o_o = Z_OFF[6], Z_OFF[7], Z_OFF[8]
    w_main = jnp.concatenate([w[:, :o_i], w[:, o_o:]], axis=1).astype(BF16)
    w_i = _pad_rows(w[:, o_i:o_f].T, 8).astype(BF16)
    w_f = _pad_rows(w[:, o_f:o_o].T, 8).astype(BF16)
    b_i = _pad_rows(b_igate[l][:, None], 8)
    b_f = _pad_rows(b_fgate[l][:, None], 8)
    sinks = attn_sinks[l]
    norm_w = mlstm_norm_w[l][None]
    post_w = (w_proj_a[l].astype(BF16), w_proj_b[l].astype(BF16), w_out[l].astype(BF16),
              ln1_w[l][None], ln1_b[l][None], w_up[l].astype(BF16), b_up[l][None], conv_w[l], conv_b[l][None],
              w_down[l].astype(BF16), ln2_w[l][None], ln2_b[l][None])

    qa, ka, va, qm, km, vm, og, ga, gb, ga_row, gb_row = _inproj_call(
        x_prompt, mod_p, w_main, w_i, w_f, b_i, b_f, S=1, R=256, L=CHUNK)
    ya = _attn_prompt_call(qa, ka, va, sinks)
    yb, p_c, p_n, p_m = _mlstm_call(
        qm, km, vm, og, ga_row, gb_row, norm_w,
        jnp.zeros((Bp, N_HEADS_B, DV_B, DQK_B), F32), jnp.zeros((Bp, N_HEADS_B, 1, DQK_B), F32),
        jnp.zeros((Bp, N_HEADS_B, 1, 1), F32), SB=1, CPS=2, L=CHUNK)
    y_prompt, p_conv = _post_call(x_prompt, mod_p, ya, yb, ga, gb,
                                  jnp.zeros((Bp, CONV_W - 1, 2 * D_FF), F32), *post_w, S=1, R=256)
    p_k = ka[:, Tp - KV_WIN:].reshape(Bp, KV_WIN, N_KV_A, HEAD_DIM_A)
    p_v = va[:, Tp - KV_WIN:].reshape(Bp, KV_WIN, N_KV_A, HEAD_DIM_A)

    qa, ka, va, qm, km, vm, og, ga, gb, ga_row, gb_row = _inproj_call(
        x_sample, mod_s, w_main, w_i, w_f, b_i, b_f, S=8, R=Ts, L=Ts)
    ya, s_k, s_v = _attn_sample_call(qa, ka, va, cache_attn_k[l].reshape(Bs, KV_WIN, KV_A_W),
                                     cache_attn_v[l].reshape(Bs, KV_WIN, KV_A_W), sinks)
    yb, s_c, s_n, s_m = _mlstm_call(
        qm, km, vm, og, ga_row, gb_row, norm_w,
        state_mlstm_C[l], state_mlstm_n[l][:, :, None, :], state_mlstm_m[l][:, :, None, None],
        SB=4, CPS=1, L=Ts)
    y_sample, s_conv = _post_call(x_sample, mod_s, ya, yb, ga, gb, state_ffn_conv[l], *post_w, S=8, R=Ts)

    def st(a, shape):
        return a.reshape((1,) + shape)

    return (y_prompt, y_sample,
            st(p_k, (Bp, KV_WIN, N_KV_A, HEAD_DIM_A)), st(p_v, (Bp, KV_WIN, N_KV_A, HEAD_DIM_A)),
            st(p_c, (Bp, N_HEADS_B, DV_B, DQK_B)), st(p_n, (Bp, N_HEADS_B, DQK_B)), st(p_m, (Bp, N_HEADS_B)),
            st(p_conv, (Bp, CONV_W - 1, 2 * D_FF)),
            st(s_k, (Bs, KV_WIN, N_KV_A, HEAD_DIM_A)), st(s_v, (Bs, KV_WIN, N_KV_A, HEAD_DIM_A)),
            st(s_c, (Bs, N_HEADS_B, DV_B, DQK_B)), st(s_n, (Bs, N_HEADS_B, DQK_B)), st(s_m, (Bs, N_HEADS_B)),
            st(s_conv, (Bs, CONV_W - 1, 2 * D_FF)))
```

```python
import functools

import numpy as np
import jax
import jax.numpy as jnp
from jax import lax
from jax.experimental import pallas as pl
from jax.experimental.pallas import tpu as pltpu

F32 = jnp.float32
BF16 = jnp.bfloat16

D_MODEL = 1024
DEPTH = 1
PAST_LEN = 2048
CHUNK = 64
N_HEADS_A = 16
N_KV_A = 2
HEAD_DIM_A = 64
GROUP_A = N_HEADS_A // N_KV_A
WINDOW = 128
WINDOW_CHUNKS = WINDOW // CHUNK
KV_WIN = min(WINDOW, PAST_LEN)
N_HEADS_B = 4
DQK_B = 128
DV_B = 256
D_FF = 2816
CONV_W = 3
LN_EPS = 1e-5
ALPHA = (2 * DEPTH) ** 0.25
Q_A = N_HEADS_A * HEAD_DIM_A
KV_A_W = N_KV_A * HEAD_DIM_A
QK_B = N_HEADS_B * DQK_B
V_B = N_HEADS_B * DV_B
Z_PARTS = (Q_A, KV_A_W, KV_A_W, QK_B, QK_B, V_B, N_HEADS_B, N_HEADS_B, V_B, D_MODEL, D_MODEL)
Z_OFF = tuple(int(v) for v in np.cumsum((0,) + Z_PARTS))

LANES = 128

M_QA = 0
M_KA = M_QA + Q_A
M_VA = M_KA + KV_A_W
M_QM = M_VA + KV_A_W
M_VM = M_QM + QK_B
M_OG = M_VM + V_B
M_GA = M_OG + V_B
M_GB = M_GA + D_MODEL
M_GI = M_GB + D_MODEL
M_GF = M_GI + LANES
M_END = M_GF + LANES
ATT_QB = 128
ATT_W = 256
FFN_CHUNKS = 2
MLSTM_CHUNK = 256


def _bdot(a, b):
    return jnp.dot(a.astype(BF16), b.astype(BF16), preferred_element_type=F32)


def _bdot_nt(a, b):
    return lax.dot_general(a.astype(BF16), b.astype(BF16), (((1,), (1,)), ((), ())),
                           preferred_element_type=F32)


def _bdot_tn(a, b):
    return lax.dot_general(a.astype(BF16), b.astype(BF16), (((0,), (0,)), ((), ())),
                           preferred_element_type=F32)


def _const_spec(shape):
    nd = len(shape)
    return pl.BlockSpec(shape, lambda *_: (0,) * nd, pipeline_mode=pl.Buffered(1))


def _layer_norm(x, w, b):
    mu = jnp.mean(x, -1, keepdims=True)
    xc = x - mu
    var = jnp.mean(jnp.square(xc), -1, keepdims=True)
    return xc * lax.rsqrt(var + LN_EPS) * w + b


def _mod_kernel(c_ref, w_ref, b_ref, o_ref):
    c = c_ref[...]
    s = c * jax.nn.sigmoid(c)
    o_ref[...] = _bdot(s, w_ref[...]) + b_ref[...]


def _mod_call(c_all, w_ada, b_ada):
    n = c_all.shape[0]
    tn = 1536
    return pl.pallas_call(
        _mod_kernel,
        grid=(6 * D_MODEL // tn,),
        in_specs=[pl.BlockSpec((n, D_MODEL), lambda j: (0, 0)),
                  pl.BlockSpec((D_MODEL, tn), lambda j: (0, j)),
                  pl.BlockSpec((1, tn), lambda j: (0, j))],
        out_specs=pl.BlockSpec((n, tn), lambda j: (0, j)),
        out_shape=jax.ShapeDtypeStruct((n, 6 * D_MODEL), F32),
        name="mod",
    )(c_all, w_ada, b_ada)


def _log_sigmoid(x):
    return jnp.minimum(x, 0.0) - jnp.log1p(jnp.exp(-jnp.abs(x)))


def _seg_scan(x, L, axis, op, ident):
    pos = lax.broadcasted_iota(jnp.int32, x.shape, axis) % L
    step = 1
    while step < L:
        x = op(x, jnp.where(pos >= step, pltpu.roll(x, step, axis=axis), ident))
        step *= 2
    return x


def _inproj_kernel(x_ref, mod_ref, w_ref, wk_ref, wi_ref, wf_ref, bi_ref, bf_ref, bic_ref, bfc_ref,
                   qa_ref, ka_ref, va_ref, qm_ref, kt_ref, vm_ref, og_ref, ga_ref, gb_ref,
                   ar_ref, bc_ref, ac_ref, *, S, R, L):
    SR = S * R
    sh = mod_ref[:, 0:1, :]
    sc = mod_ref[:, 1:2, :]
    h = (x_ref[...] * (1.0 + sc) + sh).reshape(SR, D_MODEL).astype(BF16)

    def proj(lo, hi):
        return jnp.dot(h, w_ref[:, lo:hi], preferred_element_type=F32)

    qa_ref[...] = (proj(M_QA, M_KA) * (HEAD_DIM_A ** -0.5)).astype(BF16).reshape(S, R, Q_A)
    ka_ref[...] = proj(M_KA, M_VA).reshape(S, R, KV_A_W)
    va_ref[...] = proj(M_VA, M_QM).reshape(S, R, KV_A_W)
    qm_ref[...] = proj(M_QM, M_VM).astype(BF16).reshape(S, R, QK_B)
    vm_ref[...] = proj(M_VM, M_OG).astype(BF16).reshape(S, R, V_B)
    og_ref[...] = proj(M_OG, M_GA).reshape(S, R, V_B)
    ga_ref[...] = proj(M_GA, M_GB).reshape(S, R, D_MODEL)
    gb_ref[...] = proj(M_GB, M_GI).reshape(S, R, D_MODEL)

    kt = (_bdot_nt(wk_ref[...], h) * (DQK_B ** -0.5)).astype(BF16)
    i_row = _bdot_nt(wi_ref[...], h) + bi_ref[...]
    logf_row = _log_sigmoid(_bdot_nt(wf_ref[...], h) + bf_ref[...])
    a_row = i_row - _seg_scan(logf_row, L, 1, jnp.add, 0.0)
    for s in range(S):
        kt_ref[s] = kt[:, s * R:(s + 1) * R]
        ar_ref[s] = a_row[:, s * R:(s + 1) * R]
    i_col = proj(M_GI, M_GF) + bic_ref[...]
    logf_col = _log_sigmoid(proj(M_GF, M_END) + bfc_ref[...])
    b_col = _seg_scan(logf_col, L, 0, jnp.add, 0.0)
    a_col = _seg_scan(i_col - b_col, L, 0, jnp.maximum, -jnp.inf)
    bc_ref[...] = b_col.reshape(S, R, LANES)
    ac_ref[...] = a_col.reshape(S, R, LANES)


def _inproj_call(x, mod, w_main, w_kt, w_i, w_f, b_i, b_f, b_ic, b_fc, *, S, R, L):
    N, T, _ = x.shape
    nsb, nrb = N // S, T // R

    def tok(c):
        return pl.BlockSpec((S, R, c), lambda i, j: (i, j, 0))

    def tok_shape(c, dt):
        return jax.ShapeDtypeStruct((N, T, c), dt)

    def tr(c):
        return pl.BlockSpec((S, c, R), lambda i, j: (i, 0, j))

    return pl.pallas_call(
        functools.partial(_inproj_kernel, S=S, R=R, L=L),
        grid=(nsb, nrb),
        in_specs=[tok(D_MODEL),
                  pl.BlockSpec((S, 6, D_MODEL), lambda i, j: (i, 0, 0)),
                  _const_spec((D_MODEL, M_END)), _const_spec((QK_B, D_MODEL)),
                  _const_spec((8, D_MODEL)), _const_spec((8, D_MODEL)),
                  _const_spec((8, 1)), _const_spec((8, 1)),
                  _const_spec((1, LANES)), _const_spec((1, LANES))],
        out_specs=[tok(Q_A), tok(KV_A_W), tok(KV_A_W), tok(QK_B), tr(QK_B), tok(V_B),
                   tok(V_B), tok(D_MODEL), tok(D_MODEL), tr(8), tok(LANES), tok(LANES)],
        out_shape=[tok_shape(Q_A, BF16), tok_shape(KV_A_W, F32), tok_shape(KV_A_W, F32),
                   tok_shape(QK_B, BF16), jax.ShapeDtypeStruct((N, QK_B, T), BF16), tok_shape(V_B, BF16),
                   tok_shape(V_B, F32), tok_shape(D_MODEL, F32), tok_shape(D_MODEL, F32),
                   jax.ShapeDtypeStruct((N, 8, T), F32), tok_shape(LANES, F32), tok_shape(LANES, F32)],
        compiler_params=pltpu.CompilerParams(dimension_semantics=("parallel", "parallel")),
        name="inproj",
    )(x, mod, w_main, w_kt, w_i, w_f, b_i, b_f, b_ic, b_fc)


def _alibi_slopes():
    return 2.0 ** (-8.0 * np.arange(1, N_HEADS_A + 1, dtype=np.float64) / N_HEADS_A)


def _attn_bias(qpos, kpos, kvalid):
    qpos = np.asarray(qpos)[:, None]
    kpos = np.asarray(kpos)[None, :]
    qc, kc = qpos // CHUNK, kpos // CHUNK
    visible = (kpos >= 0) & (kc <= qc) & (kc >= qc - WINDOW_CHUNKS) & np.asarray(kvalid)[None, :]
    dist = np.abs(qpos - kpos).astype(np.float64)
    slopes = _alibi_slopes()
    out = np.zeros((N_KV_A, GROUP_A // 2, qpos.shape[0], 2, kpos.shape[1]), np.float32)
    for kv in range(N_KV_A):
        for p in range(GROUP_A // 2):
            for half in range(2):
                hd = kv * GROUP_A + 2 * p + half
                out[kv, p, :, half, :] = np.where(visible, -slopes[hd] * dist, -np.inf)
    return out.reshape(N_KV_A, (GROUP_A // 2) * qpos.shape[0], 2 * kpos.shape[1])


def _attn_core(q, kwin, vwin, bias_of, sinks_ref, store):
    Mq = q.shape[0]
    W = kwin.shape[0]
    npair = GROUP_A // 2
    lane = lax.broadcasted_iota(jnp.int32, (npair * Mq, LANES), 1)
    for kv in range(N_KV_A):
        kk = kwin[:, kv * HEAD_DIM_A:(kv + 1) * HEAD_DIM_A].astype(BF16)
        vv = vwin[:, kv * HEAD_DIM_A:(kv + 1) * HEAD_DIM_A].astype(BF16)
        zero = jnp.zeros_like(kk)
        one = jnp.ones_like(vv)
        k2 = jnp.concatenate([jnp.concatenate([kk, zero], axis=1),
                              jnp.concatenate([zero, kk], axis=1)], axis=0)
        v_aug = jnp.concatenate([vv, one, one, vv], axis=1)
        qs = jnp.concatenate([q[:, (kv * npair + p) * LANES:(kv * npair + p + 1) * LANES]
                              for p in range(npair)], axis=0)
        s = _bdot_nt(qs, k2) + bias_of(kv)
        probs, sink_terms = [], []
        for half in range(2):
            sh = s[:, half * W:(half + 1) * W]
            sink = jnp.concatenate(
                [jnp.full((Mq, 1), sinks_ref[kv * GROUP_A + 2 * p + half], F32) for p in range(npair)], axis=0)
            mx = jnp.maximum(jnp.max(sh, axis=1, keepdims=True), sink)
            probs.append(jnp.exp(sh - mx).astype(BF16))
            sink_terms.append(jnp.exp(sink - mx))
        o = jnp.dot(jnp.concatenate(probs, axis=0), v_aug, preferred_element_type=F32)
        oe, oo = o[:npair * Mq], o[npair * Mq:]
        ye = oe[:, :LANES] / (oe[:, LANES:] + sink_terms[0])
        yo = oo[:, LANES:] / (oo[:, :LANES] + sink_terms[1])
        y = jnp.where(lane < HEAD_DIM_A, ye, yo).astype(BF16)
        for p in range(npair):
            store((kv * npair + p) * LANES, y[p * Mq:(p + 1) * Mq])


def _attn_prompt_kernel(sinks_ref, q_ref, kp_ref, kc_ref, vp_ref, vc_ref, bias_ref, o_ref):
    kwin = jnp.concatenate([kp_ref[0], kc_ref[0]], axis=0)
    vwin = jnp.concatenate([vp_ref[0], vc_ref[0]], axis=0)

    def store(col, val):
        o_ref[0, :, col:col + LANES] = val

    _attn_core(q_ref[0], kwin, vwin, lambda kv: bias_ref[0, kv], sinks_ref, store)


def _attn_prompt_call(qa, ka, va, sinks):
    N, T, _ = qa.shape
    nq = T // ATT_QB
    rel_q = ATT_QB + np.arange(ATT_QB)
    rel_k = np.arange(ATT_W)
    bias = np.stack([_attn_bias(rel_q, rel_k, rel_k >= ATT_QB),
                     _attn_bias(rel_q, rel_k, rel_k >= 0)])
    kv_prev = pl.BlockSpec((1, ATT_QB, KV_A_W), lambda n, j: (n, jnp.maximum(j - 1, 0), 0))
    kv_cur = pl.BlockSpec((1, ATT_QB, KV_A_W), lambda n, j: (n, j, 0))
    return pl.pallas_call(
        _attn_prompt_kernel,
        grid=(N, nq),
        in_specs=[pl.BlockSpec(memory_space=pltpu.SMEM),
                  pl.BlockSpec((1, ATT_QB, Q_A), lambda n, j: (n, j, 0)),
                  kv_prev, kv_cur, kv_prev, kv_cur,
                  pl.BlockSpec((1,) + bias.shape[1:], lambda n, j: (jnp.minimum(j, 1), 0, 0, 0))],
        out_specs=pl.BlockSpec((1, ATT_QB, Q_A), lambda n, j: (n, j, 0)),
        out_shape=jax.ShapeDtypeStruct((N, T, Q_A), BF16),
        compiler_params=pltpu.CompilerParams(dimension_semantics=("parallel", "parallel")),
        name="attn_prompt",
    )(sinks, qa, ka, ka, va, va, jnp.asarray(bias))


def _attn_sample_kernel(sinks_ref, q_ref, kn_ref, vn_ref, kc_ref, vc_ref, bias_ref,
                        o_ref, ko_ref, vo_ref, *, T):
    pad = jnp.zeros((ATT_W - KV_WIN - T, KV_A_W), F32)
    kwin = jnp.concatenate([kc_ref[0], kn_ref[0], pad], axis=0)
    vwin = jnp.concatenate([vc_ref[0], vn_ref[0], pad], axis=0)
    ko_ref[0] = kwin[T:T + KV_WIN]
    vo_ref[0] = vwin[T:T + KV_WIN]

    def store(col, val):
        o_ref[0, :, col:col + LANES] = val

    _attn_core(q_ref[0], kwin, vwin, lambda kv: bias_ref[kv], sinks_ref, store)


def _attn_sample_call(qa, ka, va, cache_k, cache_v, sinks):
    N, T, _ = qa.shape
    qpos = PAST_LEN + np.arange(T)
    kpos = PAST_LEN - KV_WIN + np.arange(ATT_W)
    bias = _attn_bias(qpos, kpos, np.arange(ATT_W) < KV_WIN + T)
    new = pl.BlockSpec((1, T, KV_A_W), lambda n: (n, 0, 0))
    cache = pl.BlockSpec((1, KV_WIN, KV_A_W), lambda n: (n, 0, 0))
    cache_shape = jax.ShapeDtypeStruct((N, KV_WIN, KV_A_W), F32)
    return pl.pallas_call(
        functools.partial(_attn_sample_kernel, T=T),
        grid=(N,),
        in_specs=[pl.BlockSpec(memory_space=pltpu.SMEM),
                  pl.BlockSpec((1, T, Q_A), lambda n: (n, 0, 0)),
                  new, new, cache, cache,
                  _const_spec(bias.shape)],
        out_specs=[pl.BlockSpec((1, T, Q_A), lambda n: (n, 0, 0)), cache, cache],
        out_shape=[jax.ShapeDtypeStruct((N, T, Q_A), BF16), cache_shape, cache_shape],
        compiler_params=pltpu.CompilerParams(dimension_semantics=("parallel",)),
        name="attn_sample",
    )(sinks, qa, ka, va, cache_k, cache_v, jnp.asarray(bias))


def _mlstm_kernel(q_ref, kt_ref, v_ref, og_ref, ar_ref, bc_ref, ac_ref, nw_ref, c0_ref, n0_ref, m0_ref,
                  y_ref, c_ref, n_ref, m_ref, ct_ref, *, SB, L):
    j = pl.program_id(1)
    heads = [(sb, hd) for sb in range(SB) for hd in range(N_HEADS_B)]

    @pl.when(j == 0)
    def _():
        for sb, hd in heads:
            ct_ref[sb, hd, :, :DV_B] = c0_ref[sb, hd].T
            ct_ref[sb, hd, :, DV_B:] = jnp.broadcast_to(n0_ref[sb, hd], (DQK_B, LANES))
        m_ref[...] = m0_ref[...]

    causal = (lax.broadcasted_iota(jnp.int32, (L, L), 0) >= lax.broadcasted_iota(jnp.int32, (L, L), 1))
    ones = jnp.ones((L, LANES), BF16)
    for sb in range(SB):
        a4 = ac_ref[sb]
        b4 = bc_ref[sb]
        m_row = m_ref[sb]
        m4 = jnp.maximum(a4, m_row)
        r4 = jnp.exp(a4 - m4)
        g4 = jnp.exp(m_row - m4)
        e4 = jnp.exp(-(b4 + m4))
        a_last = a4[L - 1:L]
        b_last = b4[L - 1:L]
        mx = jnp.maximum(a_last, m_row)
        m_new = b_last + mx
        decay4 = jnp.exp(b_last + m_row - m_new)
        rho4 = jnp.exp(a_last - mx)
        m_ref[sb] = m_new
        for hd in range(N_HEADS_B):
            q = q_ref[sb, :, hd * DQK_B:(hd + 1) * DQK_B]
            kt = kt_ref[sb, hd * DQK_B:(hd + 1) * DQK_B, :]
            v_aug = jnp.concatenate([v_ref[sb, :, hd * DV_B:(hd + 1) * DV_B], ones], axis=1)
            a_row = ar_ref[sb, hd:hd + 1, :]
            w = jnp.exp(jnp.where(causal, a_row - a4[:, hd:hd + 1], -jnp.inf))
            s = (jnp.dot(q, kt, preferred_element_type=F32) * w).astype(BF16)
            kw = (kt.astype(F32) * jnp.exp(a_row - a_last[:, hd:hd + 1])).astype(BF16)
            pu = jnp.dot(jnp.concatenate([s, kw], axis=0), v_aug, preferred_element_type=F32)
            ct = ct_ref[sb, hd]
            qc = jnp.dot(q, ct.astype(BF16), preferred_element_type=F32)
            comb = r4[:, hd:hd + 1] * pu[:L] + g4[:, hd:hd + 1] * qc
            inv = 1.0 / jnp.maximum(jnp.abs(comb[:, DV_B:]), e4[:, hd:hd + 1])
            hh = jnp.concatenate([comb[:, :LANES] * inv, comb[:, LANES:DV_B] * inv], axis=1)

            mu = jnp.mean(hh, axis=1, keepdims=True)
            hc = hh - mu
            var = jnp.mean(jnp.square(hc), axis=1, keepdims=True)
            yn = hc * lax.rsqrt(var + LN_EPS) * nw_ref[:, hd * DV_B:(hd + 1) * DV_B]
            og = og_ref[sb, :, hd * DV_B:(hd + 1) * DV_B]
            y_ref[sb, :, hd * DV_B:(hd + 1) * DV_B] = (jax.nn.sigmoid(og) * yn).astype(BF16)

            ct_ref[sb, hd] = decay4[:, hd:hd + 1] * ct + rho4[:, hd:hd + 1] * pu[L:]

    @pl.when(j == pl.num_programs(1) - 1)
    def _():
        for sb, hd in heads:
            c_ref[sb, hd] = ct_ref[sb, hd, :, :DV_B].T
            n_ref[sb, hd] = ct_ref[sb, hd, :, DV_B:DV_B + 1]


def _mlstm_call(qm, kt, vm, og, a_row, b_col, a_col, norm_w, c0, n0, m0, *, SB, L):
    N, T, _ = qm.shape

    def tok(c):
        return pl.BlockSpec((SB, L, c), lambda i, j: (i, j, 0))

    def tr(c):
        return pl.BlockSpec((SB, c, L), lambda i, j: (i, 0, j))

    st_c = pl.BlockSpec((SB, N_HEADS_B, DV_B, DQK_B), lambda i, j: (i, 0, 0, 0))
    st_n = pl.BlockSpec((SB, N_HEADS_B, DQK_B, 1), lambda i, j: (i, 0, 0, 0))
    st_m = pl.BlockSpec((SB, 1, LANES), lambda i, j: (i, 0, 0))
    return pl.pallas_call(
        functools.partial(_mlstm_kernel, SB=SB, L=L),
        grid=(N // SB, T // L),
        in_specs=[tok(QK_B), tr(QK_B), tok(V_B), tok(V_B), tr(8), tok(LANES), tok(LANES),
                  _const_spec((1, V_B)), st_c, st_n, st_m],
        out_specs=[tok(V_B), st_c, st_n, st_m],
        out_shape=[jax.ShapeDtypeStruct((N, T, V_B), BF16),
                   jax.ShapeDtypeStruct((N, N_HEADS_B, DV_B, DQK_B), F32),
                   jax.ShapeDtypeStruct((N, N_HEADS_B, DQK_B, 1), F32),
                   jax.ShapeDtypeStruct((N, 1, LANES), F32)],
        scratch_shapes=[pltpu.VMEM((SB, N_HEADS_B, DQK_B, DV_B + LANES), F32)],
        compiler_params=pltpu.CompilerParams(dimension_semantics=("parallel", "arbitrary")),
        name="mlstm",
    )(qm, kt, vm, og, a_row, b_col, a_col, norm_w, c0, n0, m0)


def _post_kernel(x_ref, mod_ref, ya_ref, yb_ref, ga_ref, gb_ref, cin_ref,
                 wpa_ref, wpb_ref, wo_ref, l1w_ref, l1b_ref, wup_ref, bup_ref, cw_ref, cb_ref,
                 wdn_ref, l2w_ref, l2b_ref,
                 y_ref, cout_ref, ctx_ref, act_ref, *, S, R):
    SR = S * R
    FC = D_FF // FFN_CHUNKS

    @pl.when(pl.program_id(1) == 0)
    def _():
        cout_ref[...] = cin_ref[...]

    g1 = mod_ref[:, 2:3, :]
    sh2 = mod_ref[:, 3:4, :]
    sc2 = mod_ref[:, 4:5, :]
    g2 = mod_ref[:, 5:6, :]
    x = x_ref[...]

    pa = jnp.dot(ya_ref[...].reshape(SR, Q_A), wpa_ref[...], preferred_element_type=F32)
    pb = jnp.dot(yb_ref[...].reshape(SR, V_B), wpb_ref[...], preferred_element_type=F32)
    merged = (jax.nn.sigmoid(ga_ref[...].reshape(SR, D_MODEL)) * pa
              + jax.nn.sigmoid(gb_ref[...].reshape(SR, D_MODEL)) * pb)
    mo = _bdot(merged, wo_ref[...]).reshape(S, R, D_MODEL)
    x1 = _layer_norm(ALPHA * x + g1 * mo, l1w_ref[...], l1b_ref[...])
    h2 = (x1 * (1.0 + sc2) + sh2).reshape(SR, D_MODEL).astype(BF16)

    def conv(lo):
        u = (jnp.dot(h2, wup_ref[:, lo:lo + FC], preferred_element_type=F32)
             + bup_ref[:, lo:lo + FC]).reshape(S, R, FC)
        ctx_ref[:, 6:8, :] = cout_ref[:, :, lo:lo + FC]
        ctx_ref[:, 8:, :] = u
        cout_ref[:, :, lo:lo + FC] = ctx_ref[:, R + 6:R + 8, :]
        return (cb_ref[:, lo:lo + FC]
                + cw_ref[0:1, lo:lo + FC] * ctx_ref[:, 6:R + 6, :]
                + cw_ref[1:2, lo:lo + FC] * ctx_ref[:, 7:R + 7, :]
                + cw_ref[2:3, lo:lo + FC] * u)

    for ci in range(FFN_CHUNKS):
        lo = ci * FC
        a = conv(lo)
        g = conv(D_FF + lo)
        act_ref[:, lo:lo + FC] = (a * jax.nn.gelu(g, approximate=True)).reshape(SR, FC).astype(BF16)

    f = jnp.dot(act_ref[...], wdn_ref[...], preferred_element_type=F32).reshape(S, R, D_MODEL)
    y_ref[...] = _layer_norm(ALPHA * x1 + g2 * f, l2w_ref[...], l2b_ref[...])


def _post_call(x, mod, ya, yb, ga, gb, conv_in, wpa, wpb, wo, l1w, l1b, wup, bup, cw, cb, wdn, l2w, l2b,
               *, S, R):
    N, T, _ = x.shape
    nsb, nrb = N // S, T // R
    FC = D_FF // FFN_CHUNKS

    def tok(c):
        return pl.BlockSpec((S, R, c), lambda i, j: (i, j, 0))

    conv_spec = pl.BlockSpec((S, CONV_W - 1, 2 * D_FF), lambda i, j: (i, 0, 0))
    return pl.pallas_call(
        functools.partial(_post_kernel, S=S, R=R),
        grid=(nsb, nrb),
        in_specs=[tok(D_MODEL), pl.BlockSpec((S, 6, D_MODEL), lambda i, j: (i, 0, 0)),
                  tok(Q_A), tok(V_B), tok(D_MODEL), tok(D_MODEL), conv_spec,
                  _const_spec((Q_A, D_MODEL)), _const_spec((V_B, D_MODEL)), _const_spec((D_MODEL, D_MODEL)),
                  _const_spec((1, D_MODEL)), _const_spec((1, D_MODEL)),
                  _const_spec((D_MODEL, 2 * D_FF)), _const_spec((1, 2 * D_FF)),
                  _const_spec((CONV_W, 2 * D_FF)), _const_spec((1, 2 * D_FF)),
                  _const_spec((D_FF, D_MODEL)), _const_spec((1, D_MODEL)), _const_spec((1, D_MODEL))],
        out_specs=[tok(D_MODEL), conv_spec],
        out_shape=[jax.ShapeDtypeStruct((N, T, D_MODEL), F32),
                   jax.ShapeDtypeStruct((N, CONV_W - 1, 2 * D_FF), F32)],
        scratch_shapes=[pltpu.VMEM((S, R + 8, FC), F32), pltpu.VMEM((S * R, D_FF), BF16)],
        compiler_params=pltpu.CompilerParams(dimension_semantics=("parallel", "arbitrary")),
        name="post",
    )(x, mod, ya, yb, ga, gb, conv_in, wpa, wpb, wo, l1w, l1b, wup, bup, cw, cb, wdn, l2w, l2b)


def _pad_rows(a, rows):
    return jnp.pad(a, ((0, rows - a.shape[0]),) + ((0, 0),) * (a.ndim - 1))


def kernel(x_prompt, x_sample, cache_attn_k, cache_attn_v, state_mlstm_C, state_mlstm_n, state_mlstm_m, state_ffn_conv, c_prompt, c_sample, w_ada, b_ada, w_in, b_igate, b_fgate, attn_sinks, mlstm_norm_w, w_proj_a, w_proj_b, w_out, ln1_w, ln1_b, w_up, b_up, conv_w, conv_b, w_down, ln2_w, ln2_b):
    Bp, Tp, D = x_prompt.shape
    Bs, Ts, _ = x_sample.shape
    l = 0

    mod = _mod_call(jnp.concatenate([c_prompt, c_sample], axis=0), w_ada[l], b_ada[l][None])
    mod_p = mod[:Bp].reshape(Bp, 6, D)
    mod_s = mod[Bp:].reshape(Bs, 6, D)

    w = w_in[l]
    o_km, o_vm, o_i, o_f, o_o = Z_OFF[4], Z_OFF[5], Z_OFF[6], Z_OFF[7], Z_OFF[8]
    lane_pad = ((0, 0), (0, LANES - N_HEADS_B))
    w_main = jnp.concatenate([w[:, :o_km], w[:, o_vm:o_i], w[:, o_o:],
                              jnp.pad(w[:, o_i:o_f], lane_pad), jnp.pad(w[:, o_f:o_o], lane_pad)],
                             axis=1).astype(BF16)
    w_kt = w[:, o_km:o_vm].T.astype(BF16)
    w_i = _pad_rows(w[:, o_i:o_f].T, 8).astype(BF16)
    w_f = _pad_rows(w[:, o_f:o_o].T, 8).astype(BF16)
    b_i = _pad_rows(b_igate[l][:, None], 8)
    b_f = _pad_rows(b_fgate[l][:, None], 8)
    b_ic = jnp.pad(b_igate[l][None], lane_pad)
    b_fc = jnp.pad(b_fgate[l][None], lane_pad)
    inproj_w = (w_main, w_kt, w_i, w_f, b_i, b_f, b_ic, b_fc)
    sinks = attn_sinks[l]
    norm_w = mlstm_norm_w[l][None]
    post_w = (w_proj_a[l].astype(BF16), w_proj_b[l].astype(BF16), w_out[l].astype(BF16),
              ln1_w[l][None], ln1_b[l][None], w_up[l].astype(BF16), b_up[l][None], conv_w[l], conv_b[l][None],
              w_down[l].astype(BF16), ln2_w[l][None], ln2_b[l][None])

    qa, ka, va, qm, kt, vm, og, ga, gb, a_row, b_col, a_col = _inproj_call(
        x_prompt, mod_p, *inproj_w, S=1, R=MLSTM_CHUNK, L=MLSTM_CHUNK)
    ya = _attn_prompt_call(qa, ka, va, sinks)
    yb, p_c, p_n, p_m = _mlstm_call(
        qm, kt, vm, og, a_row, b_col, a_col, norm_w,
        jnp.zeros((Bp, N_HEADS_B, DV_B, DQK_B), F32), jnp.zeros((Bp, N_HEADS_B, DQK_B, 1), F32),
        jnp.zeros((Bp, 1, LANES), F32), SB=2, L=MLSTM_CHUNK)
    p_m = p_m[:, 0, :N_HEADS_B]
    y_prompt, p_conv = _post_call(x_prompt, mod_p, ya, yb, ga, gb,
                                  jnp.zeros((Bp, CONV_W - 1, 2 * D_FF), F32), *post_w, S=1, R=256)
    p_k = ka[:, Tp - KV_WIN:].reshape(Bp, KV_WIN, N_KV_A, HEAD_DIM_A)
    p_v = va[:, Tp - KV_WIN:].reshape(Bp, KV_WIN, N_KV_A, HEAD_DIM_A)

    qa, ka, va, qm, kt, vm, og, ga, gb, a_row, b_col, a_col = _inproj_call(
        x_sample, mod_s, *inproj_w, S=8, R=Ts, L=Ts)
    ya, s_k, s_v = _attn_sample_call(qa, ka, va, cache_attn_k[l].reshape(Bs, KV_WIN, KV_A_W),
                                     cache_attn_v[l].reshape(Bs, KV_WIN, KV_A_W), sinks)
    yb, s_c, s_n, s_m = _mlstm_call(
        qm, kt, vm, og, a_row, b_col, a_col, norm_w,
        state_mlstm_C[l], state_mlstm_n[l][:, :, :, None], jnp.pad(state_mlstm_m[l][:, None, :], ((0, 0),) + lane_pad),
        SB=4, L=Ts)
    s_m = s_m[:, 0, :N_HEADS_B]
    y_sample, s_conv = _post_call(x_sample, mod_s, ya, yb, ga, gb, state_ffn_conv[l], *post_w, S=8, R=Ts)

    def st(a, shape):
        return a.reshape((1,) + shape)

    return (y_prompt, y_sample,
            st(p_k, (Bp, KV_WIN, N_KV_A, HEAD_DIM_A)), st(p_v, (Bp, KV_WIN, N_KV_A, HEAD_DIM_A)),
            st(p_c, (Bp, N_HEADS_B, DV_B, DQK_B)), st(p_n, (Bp, N_HEADS_B, DQK_B)), st(p_m, (Bp, N_HEADS_B)),
            st(p_conv, (Bp, CONV_W - 1, 2 * D_FF)),
            st(s_k, (Bs, KV_WIN, N_KV_A, HEAD_DIM_A)), st(s_v, (Bs, KV_WIN, N_KV_A, HEAD_DIM_A)),
            st(s_c, (Bs, N_HEADS_B, DV_B, DQK_B)), st(s_n, (Bs, N_HEADS_B, DQK_B)), st(s_m, (Bs, N_HEADS_B)),
            st(s_conv, (Bs, CONV_W - 1, 2 * D_FF)))
```

```python
import functools

import numpy as np
import jax
import jax.numpy as jnp
from jax import lax
from jax.experimental import pallas as pl
from jax.experimental.pallas import tpu as pltpu

F32 = jnp.float32
BF16 = jnp.bfloat16

D_MODEL = 1024
DEPTH = 1
PAST_LEN = 2048
CHUNK = 64
N_HEADS_A = 16
N_KV_A = 2
HEAD_DIM_A = 64
GROUP_A = N_HEADS_A // N_KV_A
WINDOW = 128
WINDOW_CHUNKS = WINDOW // CHUNK
KV_WIN = min(WINDOW, PAST_LEN)
N_HEADS_B = 4
DQK_B = 128
DV_B = 256
D_FF = 2816
CONV_W = 3
LN_EPS = 1e-5
ALPHA = (2 * DEPTH) ** 0.25
Q_A = N_HEADS_A * HEAD_DIM_A
KV_A_W = N_KV_A * HEAD_DIM_A
QK_B = N_HEADS_B * DQK_B
V_B = N_HEADS_B * DV_B
Z_PARTS = (Q_A, KV_A_W, KV_A_W, QK_B, QK_B, V_B, N_HEADS_B, N_HEADS_B, V_B, D_MODEL, D_MODEL)
Z_OFF = tuple(int(v) for v in np.cumsum((0,) + Z_PARTS))

LANES = 128

M_QA = 0
M_KA = M_QA + Q_A
M_VA = M_KA + KV_A_W
M_QM = M_VA + KV_A_W
M_VM = M_QM + QK_B
M_OG = M_VM + V_B
M_GA = M_OG + V_B
M_GB = M_GA + D_MODEL
M_GI = M_GB + D_MODEL
M_GF = M_GI + LANES
M_END = M_GF + LANES
ATT_QB = 128
ATT_W = 256
FFN_CHUNKS = 2
MLSTM_CHUNK = 256


def _bdot(a, b):
    return jnp.dot(a.astype(BF16), b.astype(BF16), preferred_element_type=F32)


def _bdot_nt(a, b):
    return lax.dot_general(a.astype(BF16), b.astype(BF16), (((1,), (1,)), ((), ())),
                           preferred_element_type=F32)


def _bdot_tn(a, b):
    return lax.dot_general(a.astype(BF16), b.astype(BF16), (((0,), (0,)), ((), ())),
                           preferred_element_type=F32)


def _const_spec(shape):
    nd = len(shape)
    return pl.BlockSpec(shape, lambda *_: (0,) * nd, pipeline_mode=pl.Buffered(1))


def _layer_norm(x, w, b):
    mu = jnp.mean(x, -1, keepdims=True)
    xc = x - mu
    var = jnp.mean(jnp.square(xc), -1, keepdims=True)
    return xc * lax.rsqrt(var + LN_EPS) * w + b


def _mod_kernel(c_ref, w_ref, b_ref, o_ref):
    c = c_ref[...]
    s = c * jax.nn.sigmoid(c)
    o_ref[...] = _bdot(s, w_ref[...]) + b_ref[...]


def _mod_call(c_all, w_ada, b_ada):
    n = c_all.shape[0]
    tn = 1536
    return pl.pallas_call(
        _mod_kernel,
        grid=(6 * D_MODEL // tn,),
        in_specs=[pl.BlockSpec((n, D_MODEL), lambda j: (0, 0)),
                  pl.BlockSpec((D_MODEL, tn), lambda j: (0, j)),
                  pl.BlockSpec((1, tn), lambda j: (0, j))],
        out_specs=pl.BlockSpec((n, tn), lambda j: (0, j)),
        out_shape=jax.ShapeDtypeStruct((n, 6 * D_MODEL), F32),
        name="mod",
    )(c_all, w_ada, b_ada)


def _log_sigmoid(x):
    return jnp.minimum(x, 0.0) - jnp.log1p(jnp.exp(-jnp.abs(x)))


def _seg_scan(x, L, axis, op, ident):
    pos = lax.broadcasted_iota(jnp.int32, x.shape, axis) % L
    step = 1
    while step < L:
        x = op(x, jnp.where(pos >= step, pltpu.roll(x, step, axis=axis), ident))
        step *= 2
    return x


def _inproj_kernel(x_ref, mod_ref, w_ref, wq_ref, wk_ref, wi_ref, wf_ref, bi_ref, bf_ref, bic_ref, bfc_ref,
                   qa_ref, ka_ref, va_ref, qm_ref, kt_ref, vm_ref, og_ref, ga_ref, gb_ref,
                   ar_ref, bc_ref, ac_ref, *, S, R, L, q_transposed):
    SR = S * R
    sh = mod_ref[:, 0:1, :]
    sc = mod_ref[:, 1:2, :]
    h = (x_ref[...] * (1.0 + sc) + sh).reshape(SR, D_MODEL).astype(BF16)

    def proj(lo, hi):
        return jnp.dot(h, w_ref[:, lo:hi], preferred_element_type=F32)

    if q_transposed:
        qt = (_bdot_nt(wq_ref[...], h) * (HEAD_DIM_A ** -0.5)).astype(BF16)
        for s in range(S):
            qa_ref[s] = qt[:, s * R:(s + 1) * R]
    else:
        qa_ref[...] = (proj(M_QA, M_KA) * (HEAD_DIM_A ** -0.5)).astype(BF16).reshape(S, R, Q_A)
    ka_ref[...] = proj(M_KA, M_VA).reshape(S, R, KV_A_W)
    va_ref[...] = proj(M_VA, M_QM).reshape(S, R, KV_A_W)
    qm_ref[...] = proj(M_QM, M_VM).astype(BF16).reshape(S, R, QK_B)
    vm_ref[...] = proj(M_VM, M_OG).astype(BF16).reshape(S, R, V_B)
    og_ref[...] = proj(M_OG, M_GA).reshape(S, R, V_B)
    ga_ref[...] = proj(M_GA, M_GB).reshape(S, R, D_MODEL)
    gb_ref[...] = proj(M_GB, M_GI).reshape(S, R, D_MODEL)

    kt = (_bdot_nt(wk_ref[...], h) * (DQK_B ** -0.5)).astype(BF16)
    i_row = _bdot_nt(wi_ref[...], h) + bi_ref[...]
    logf_row = _log_sigmoid(_bdot_nt(wf_ref[...], h) + bf_ref[...])
    a_row = i_row - _seg_scan(logf_row, L, 1, jnp.add, 0.0)
    for s in range(S):
        kt_ref[s] = kt[:, s * R:(s + 1) * R]
        ar_ref[s] = a_row[:, s * R:(s + 1) * R]
    i_col = proj(M_GI, M_GF) + bic_ref[...]
    logf_col = _log_sigmoid(proj(M_GF, M_END) + bfc_ref[...])
    b_col = _seg_scan(logf_col, L, 0, jnp.add, 0.0)
    a_col = _seg_scan(i_col - b_col, L, 0, jnp.maximum, -jnp.inf)
    bc_ref[...] = b_col.reshape(S, R, LANES)
    ac_ref[...] = a_col.reshape(S, R, LANES)


def _inproj_call(x, mod, w_main, w_qt, w_kt, w_i, w_f, b_i, b_f, b_ic, b_fc, *, S, R, L, q_transposed):
    N, T, _ = x.shape
    nsb, nrb = N // S, T // R

    def tok(c):
        return pl.BlockSpec((S, R, c), lambda i, j: (i, j, 0))

    def tok_shape(c, dt):
        return jax.ShapeDtypeStruct((N, T, c), dt)

    def tr(c):
        return pl.BlockSpec((S, c, R), lambda i, j: (i, 0, j))

    return pl.pallas_call(
        functools.partial(_inproj_kernel, S=S, R=R, L=L, q_transposed=q_transposed),
        grid=(nsb, nrb),
        in_specs=[tok(D_MODEL),
                  pl.BlockSpec((S, 6, D_MODEL), lambda i, j: (i, 0, 0)),
                  _const_spec((D_MODEL, M_END)), _const_spec((Q_A, D_MODEL)), _const_spec((QK_B, D_MODEL)),
                  _const_spec((8, D_MODEL)), _const_spec((8, D_MODEL)),
                  _const_spec((8, 1)), _const_spec((8, 1)),
                  _const_spec((1, LANES)), _const_spec((1, LANES))],
        out_specs=[tr(Q_A) if q_transposed else tok(Q_A), tok(KV_A_W), tok(KV_A_W), tok(QK_B), tr(QK_B), tok(V_B),
                   tok(V_B), tok(D_MODEL), tok(D_MODEL), tr(8), tok(LANES), tok(LANES)],
        out_shape=[jax.ShapeDtypeStruct((N, Q_A, T), BF16) if q_transposed else tok_shape(Q_A, BF16),
                   tok_shape(KV_A_W, F32), tok_shape(KV_A_W, F32),
                   tok_shape(QK_B, BF16), jax.ShapeDtypeStruct((N, QK_B, T), BF16), tok_shape(V_B, BF16),
                   tok_shape(V_B, F32), tok_shape(D_MODEL, F32), tok_shape(D_MODEL, F32),
                   jax.ShapeDtypeStruct((N, 8, T), F32), tok_shape(LANES, F32), tok_shape(LANES, F32)],
        compiler_params=pltpu.CompilerParams(dimension_semantics=("parallel", "parallel")),
        name="inproj",
    )(x, mod, w_main, w_qt, w_kt, w_i, w_f, b_i, b_f, b_ic, b_fc)


def _alibi_slopes():
    return 2.0 ** (-8.0 * np.arange(1, N_HEADS_A + 1, dtype=np.float64) / N_HEADS_A)


def _attn_bias(qpos, kpos, kvalid, transposed=False):
    qpos = np.asarray(qpos)[:, None]
    kpos = np.asarray(kpos)[None, :]
    qc, kc = qpos // CHUNK, kpos // CHUNK
    visible = (kpos >= 0) & (kc <= qc) & (kc >= qc - WINDOW_CHUNKS) & np.asarray(kvalid)[None, :]
    dist = np.abs(qpos - kpos).astype(np.float64)
    slopes = _alibi_slopes()
    out = np.zeros((N_KV_A, GROUP_A // 2, qpos.shape[0], 2, kpos.shape[1]), np.float32)
    for kv in range(N_KV_A):
        for p in range(GROUP_A // 2):
            for half in range(2):
                hd = kv * GROUP_A + 2 * p + half
                out[kv, p, :, half, :] = np.where(visible, -slopes[hd] * dist, -np.inf)
    if transposed:
        return out.transpose(0, 3, 4, 1, 2).reshape(N_KV_A, 2 * kpos.shape[1], (GROUP_A // 2) * qpos.shape[0])
    return out.reshape(N_KV_A, (GROUP_A // 2) * qpos.shape[0], 2 * kpos.shape[1])


def _attn_core(q, kwin, vwin, bias_of, sinks_ref, store):
    Mq = q.shape[0]
    W = kwin.shape[0]
    npair = GROUP_A // 2
    lane = lax.broadcasted_iota(jnp.int32, (npair * Mq, LANES), 1)
    for kv in range(N_KV_A):
        kk = kwin[:, kv * HEAD_DIM_A:(kv + 1) * HEAD_DIM_A].astype(BF16)
        vv = vwin[:, kv * HEAD_DIM_A:(kv + 1) * HEAD_DIM_A].astype(BF16)
        zero = jnp.zeros_like(kk)
        one = jnp.ones_like(vv)
        k2 = jnp.concatenate([jnp.concatenate([kk, zero], axis=1),
                              jnp.concatenate([zero, kk], axis=1)], axis=0)
        v_aug = jnp.concatenate([vv, one, one, vv], axis=1)
        qs = jnp.concatenate([q[:, (kv * npair + p) * LANES:(kv * npair + p + 1) * LANES]
                              for p in range(npair)], axis=0)
        s = _bdot_nt(qs, k2) + bias_of(kv)
        probs, sink_terms = [], []
        for half in range(2):
            sh = s[:, half * W:(half + 1) * W]
            sink = jnp.concatenate(
                [jnp.full((Mq, 1), sinks_ref[kv * GROUP_A + 2 * p + half], F32) for p in range(npair)], axis=0)
            mx = jnp.maximum(jnp.max(sh, axis=1, keepdims=True), sink)
            probs.append(jnp.exp(sh - mx).astype(BF16))
            sink_terms.append(jnp.exp(sink - mx))
        o = jnp.dot(jnp.concatenate(probs, axis=0), v_aug, preferred_element_type=F32)
        oe, oo = o[:npair * Mq], o[npair * Mq:]
        ye = oe[:, :LANES] / (oe[:, LANES:] + sink_terms[0])
        yo = oo[:, LANES:] / (oo[:, :LANES] + sink_terms[1])
        y = jnp.where(lane < HEAD_DIM_A, ye, yo).astype(BF16)
        for p in range(npair):
            store((kv * npair + p) * LANES, y[p * Mq:(p + 1) * Mq])


def _attn_prompt_kernel(sinks_ref, qt_ref, kp_ref, kc_ref, vp_ref, vc_ref, bias_ref, o_ref):
    Mq = ATT_QB
    npair = GROUP_A // 2
    kwin = jnp.concatenate([kp_ref[0], kc_ref[0]], axis=0)
    vt_all = jnp.concatenate([vp_ref[0], vc_ref[0]], axis=0).T.astype(BF16)
    ones = jnp.ones((HEAD_DIM_A, ATT_W), BF16)
    pair_of_lane = lax.broadcasted_iota(jnp.int32, (1, npair * Mq), 1) // Mq
    for kv in range(N_KV_A):
        kk = kwin[:, kv * HEAD_DIM_A:(kv + 1) * HEAD_DIM_A].astype(BF16)
        zero = jnp.zeros_like(kk)
        k2 = jnp.concatenate([jnp.concatenate([kk, zero], axis=1),
                              jnp.concatenate([zero, kk], axis=1)], axis=0)
        vt = vt_all[kv * HEAD_DIM_A:(kv + 1) * HEAD_DIM_A]
        qt = jnp.concatenate([qt_ref[0, (kv * npair + p) * LANES:(kv * npair + p + 1) * LANES, :]
                              for p in range(npair)], axis=1)
        st = jnp.dot(k2, qt, preferred_element_type=F32) + bias_ref[0, kv]
        ys = []
        for half in range(2):
            sh = st[half * ATT_W:(half + 1) * ATT_W]
            sink = jnp.zeros((1, npair * Mq), F32)
            for p in range(npair):
                sink = jnp.where(pair_of_lane == p, sinks_ref[kv * GROUP_A + 2 * p + half], sink)
            mx = jnp.maximum(jnp.max(sh, axis=0, keepdims=True), sink)
            pt = jnp.exp(sh - mx).astype(BF16)
            e = jnp.exp(sink - mx)
            lhs = jnp.concatenate([vt, ones] if half == 0 else [ones, vt], axis=0)
            ot = jnp.dot(lhs, pt, preferred_element_type=F32)
            if half == 0:
                ys.append(ot[:HEAD_DIM_A] / (ot[HEAD_DIM_A:] + e))
            else:
                ys.append(ot[HEAD_DIM_A:] / (ot[:HEAD_DIM_A] + e))
        yt = jnp.concatenate(ys, axis=0)
        for p in range(npair):
            col = (kv * npair + p) * LANES
            o_ref[0, :, col:col + LANES] = yt[:, p * Mq:(p + 1) * Mq].T.astype(BF16)


def _attn_prompt_call(qa_t, ka, va, sinks):
    N, _, T = qa_t.shape
    nq = T // ATT_QB
    rel_q = ATT_QB + np.arange(ATT_QB)
    rel_k = np.arange(ATT_W)
    bias = np.stack([_attn_bias(rel_q, rel_k, rel_k >= ATT_QB, transposed=True),
                     _attn_bias(rel_q, rel_k, rel_k >= 0, transposed=True)])
    kv_prev = pl.BlockSpec((1, ATT_QB, KV_A_W), lambda n, j: (n, jnp.maximum(j - 1, 0), 0))
    kv_cur = pl.BlockSpec((1, ATT_QB, KV_A_W), lambda n, j: (n, j, 0))
    return pl.pallas_call(
        _attn_prompt_kernel,
        grid=(N, nq),
        in_specs=[pl.BlockSpec(memory_space=pltpu.SMEM),
                  pl.BlockSpec((1, Q_A, ATT_QB), lambda n, j: (n, 0, j)),
                  kv_prev, kv_cur, kv_prev, kv_cur,
                  pl.BlockSpec((1,) + bias.shape[1:], lambda n, j: (jnp.minimum(j, 1), 0, 0, 0))],
        out_specs=pl.BlockSpec((1, ATT_QB, Q_A), lambda n, j: (n, j, 0)),
        out_shape=jax.ShapeDtypeStruct((N, T, Q_A), BF16),
        compiler_params=pltpu.CompilerParams(dimension_semantics=("parallel", "parallel")),
        name="attn_prompt",
    )(sinks, qa_t, ka, ka, va, va, jnp.asarray(bias))


def _attn_sample_kernel(sinks_ref, q_ref, kn_ref, vn_ref, kc_ref, vc_ref, bias_ref,
                        o_ref, ko_ref, vo_ref, *, T):
    pad = jnp.zeros((ATT_W - KV_WIN - T, KV_A_W), F32)
    kwin = jnp.concatenate([kc_ref[0], kn_ref[0], pad], axis=0)
    vwin = jnp.concatenate([vc_ref[0], vn_ref[0], pad], axis=0)
    ko_ref[0] = kwin[T:T + KV_WIN]
    vo_ref[0] = vwin[T:T + KV_WIN]

    def store(col, val):
        o_ref[0, :, col:col + LANES] = val

    _attn_core(q_ref[0], kwin, vwin, lambda kv: bias_ref[kv], sinks_ref, store)


def _attn_sample_call(qa, ka, va, cache_k, cache_v, sinks):
    N, T, _ = qa.shape
    qpos = PAST_LEN + np.arange(T)
    kpos = PAST_LEN - KV_WIN + np.arange(ATT_W)
    bias = _attn_bias(qpos, kpos, np.arange(ATT_W) < KV_WIN + T)
    new = pl.BlockSpec((1, T, KV_A_W), lambda n: (n, 0, 0))
    cache = pl.BlockSpec((1, KV_WIN, KV_A_W), lambda n: (n, 0, 0))
    cache_shape = jax.ShapeDtypeStruct((N, KV_WIN, KV_A_W), F32)
    return pl.pallas_call(
        functools.partial(_attn_sample_kernel, T=T),
        grid=(N,),
        in_specs=[pl.BlockSpec(memory_space=pltpu.SMEM),
                  pl.BlockSpec((1, T, Q_A), lambda n: (n, 0, 0)),
                  new, new, cache, cache,
                  _const_spec(bias.shape)],
        out_specs=[pl.BlockSpec((1, T, Q_A), lambda n: (n, 0, 0)), cache, cache],
        out_shape=[jax.ShapeDtypeStruct((N, T, Q_A), BF16), cache_shape, cache_shape],
        compiler_params=pltpu.CompilerParams(dimension_semantics=("parallel",)),
        name="attn_sample",
    )(sinks, qa, ka, va, cache_k, cache_v, jnp.asarray(bias))


def _mlstm_kernel(q_ref, kt_ref, v_ref, og_ref, ar_ref, bc_ref, ac_ref, nw_ref, c0_ref, n0_ref, m0_ref,
                  y_ref, c_ref, n_ref, m_ref, ct_ref, *, SB, L):
    j = pl.program_id(1)
    heads = [(sb, hd) for sb in range(SB) for hd in range(N_HEADS_B)]

    @pl.when(j == 0)
    def _():
        for sb, hd in heads:
            ct_ref[sb, hd, :, :DV_B] = c0_ref[sb, hd].T
            ct_ref[sb, hd, :, DV_B:] = jnp.broadcast_to(n0_ref[sb, hd], (DQK_B, LANES))
        m_ref[...] = m0_ref[...]

    causal = (lax.broadcasted_iota(jnp.int32, (L, L), 0) >= lax.broadcasted_iota(jnp.int32, (L, L), 1))
    ones = jnp.ones((L, LANES), BF16)
    for sb in range(SB):
        a4 = ac_ref[sb]
        b4 = bc_ref[sb]
        m_row = m_ref[sb]
        m4 = jnp.maximum(a4, m_row)
        r4 = jnp.exp(a4 - m4)
        g4 = jnp.exp(m_row - m4)
        e4 = jnp.exp(-(b4 + m4))
        a_last = a4[L - 1:L]
        b_last = b4[L - 1:L]
        mx = jnp.maximum(a_last, m_row)
        m_new = b_last + mx
        decay4 = jnp.exp(b_last + m_row - m_new)
        rho4 = jnp.exp(a_last - mx)
        m_ref[sb] = m_new
        for hd in range(N_HEADS_B):
            q = q_ref[sb, :, hd * DQK_B:(hd + 1) * DQK_B]
            kt = kt_ref[sb, hd * DQK_B:(hd + 1) * DQK_B, :]
            v_aug = jnp.concatenate([v_ref[sb, :, hd * DV_B:(hd + 1) * DV_B], ones], axis=1)
            a_row = ar_ref[sb, hd:hd + 1, :]
            w = jnp.exp(jnp.where(causal, a_row - a4[:, hd:hd + 1], -jnp.inf))
            s = (jnp.dot(q, kt, preferred_element_type=F32) * w).astype(BF16)
            kw = (kt.astype(F32) * jnp.exp(a_row - a_last[:, hd:hd + 1])).astype(BF16)
            pu = jnp.dot(jnp.concatenate([s, kw], axis=0), v_aug, preferred_element_type=F32)
            ct = ct_ref[sb, hd]
            qc = jnp.dot(q, ct.astype(BF16), preferred_element_type=F32)
            comb = r4[:, hd:hd + 1] * pu[:L] + g4[:, hd:hd + 1] * qc
            inv = 1.0 / jnp.maximum(jnp.abs(comb[:, DV_B:]), e4[:, hd:hd + 1])
            hh = jnp.concatenate([comb[:, :LANES] * inv, comb[:, LANES:DV_B] * inv], axis=1)

            mu = jnp.mean(hh, axis=1, keepdims=True)
            hc = hh - mu
            var = jnp.mean(jnp.square(hc), axis=1, keepdims=True)
            yn = hc * lax.rsqrt(var + LN_EPS) * nw_ref[:, hd * DV_B:(hd + 1) * DV_B]
            og = og_ref[sb, :, hd * DV_B:(hd + 1) * DV_B]
            y_ref[sb, :, hd * DV_B:(hd + 1) * DV_B] = (jax.nn.sigmoid(og) * yn).astype(BF16)

            ct_ref[sb, hd] = decay4[:, hd:hd + 1] * ct + rho4[:, hd:hd + 1] * pu[L:]

    @pl.when(j == pl.num_programs(1) - 1)
    def _():
        for sb, hd in heads:
            c_ref[sb, hd] = ct_ref[sb, hd, :, :DV_B].T
            n_ref[sb, hd] = ct_ref[sb, hd, :, DV_B:DV_B + 1]


def _mlstm_call(qm, kt, vm, og, a_row, b_col, a_col, norm_w, c0, n0, m0, *, SB, L):
    N, T, _ = qm.shape

    def tok(c):
        return pl.BlockSpec((SB, L, c), lambda i, j: (i, j, 0))

    def tr(c):
        return pl.BlockSpec((SB, c, L), lambda i, j: (i, 0, j))

    st_c = pl.BlockSpec((SB, N_HEADS_B, DV_B, DQK_B), lambda i, j: (i, 0, 0, 0))
    st_n = pl.BlockSpec((SB, N_HEADS_B, DQK_B, 1), lambda i, j: (i, 0, 0, 0))
    st_m = pl.BlockSpec((SB, 1, LANES), lambda i, j: (i, 0, 0))
    return pl.pallas_call(
        functools.partial(_mlstm_kernel, SB=SB, L=L),
        grid=(N // SB, T // L),
        in_specs=[tok(QK_B), tr(QK_B), tok(V_B), tok(V_B), tr(8), tok(LANES), tok(LANES),
                  _const_spec((1, V_B)), st_c, st_n, st_m],
        out_specs=[tok(V_B), st_c, st_n, st_m],
        out_shape=[jax.ShapeDtypeStruct((N, T, V_B), BF16),
                   jax.ShapeDtypeStruct((N, N_HEADS_B, DV_B, DQK_B), F32),
                   jax.ShapeDtypeStruct((N, N_HEADS_B, DQK_B, 1), F32),
                   jax.ShapeDtypeStruct((N, 1, LANES), F32)],
        scratch_shapes=[pltpu.VMEM((SB, N_HEADS_B, DQK_B, DV_B + LANES), F32)],
        compiler_params=pltpu.CompilerParams(dimension_semantics=("parallel", "arbitrary")),
        name="mlstm",
    )(qm, kt, vm, og, a_row, b_col, a_col, norm_w, c0, n0, m0)


def _post_kernel(x_ref, mod_ref, ya_ref, yb_ref, ga_ref, gb_ref, cin_ref,
                 wpa_ref, wpb_ref, wo_ref, l1w_ref, l1b_ref, wup_ref, bup_ref, cw_ref, cb_ref,
                 wdn_ref, l2w_ref, l2b_ref,
                 y_ref, cout_ref, ctx_ref, act_ref, *, S, R):
    SR = S * R
    FC = D_FF // FFN_CHUNKS

    @pl.when(pl.program_id(1) == 0)
    def _():
        cout_ref[...] = cin_ref[...]

    g1 = mod_ref[:, 2:3, :]
    sh2 = mod_ref[:, 3:4, :]
    sc2 = mod_ref[:, 4:5, :]
    g2 = mod_ref[:, 5:6, :]
    x = x_ref[...]

    pa = jnp.dot(ya_ref[...].reshape(SR, Q_A), wpa_ref[...], preferred_element_type=F32)
    pb = jnp.dot(yb_ref[...].reshape(SR, V_B), wpb_ref[...], preferred_element_type=F32)
    merged = (jax.nn.sigmoid(ga_ref[...].reshape(SR, D_MODEL)) * pa
              + jax.nn.sigmoid(gb_ref[...].reshape(SR, D_MODEL)) * pb)
    mo = _bdot(merged, wo_ref[...]).reshape(S, R, D_MODEL)
    x1 = _layer_norm(ALPHA * x + g1 * mo, l1w_ref[...], l1b_ref[...])
    h2 = (x1 * (1.0 + sc2) + sh2).reshape(SR, D_MODEL).astype(BF16)

    def conv(lo):
        u = (jnp.dot(h2, wup_ref[:, lo:lo + FC], preferred_element_type=F32)
             + bup_ref[:, lo:lo + FC]).reshape(S, R, FC)
        ctx_ref[:, 6:8, :] = cout_ref[:, :, lo:lo + FC]
        ctx_ref[:, 8:, :] = u
        cout_ref[:, :, lo:lo + FC] = ctx_ref[:, R + 6:R + 8, :]
        return (cb_ref[:, lo:lo + FC]
                + cw_ref[0:1, lo:lo + FC] * ctx_ref[:, 6:R + 6, :]
                + cw_ref[1:2, lo:lo + FC] * ctx_ref[:, 7:R + 7, :]
                + cw_ref[2:3, lo:lo + FC] * u)

    for ci in range(FFN_CHUNKS):
        lo = ci * FC
        a = conv(lo)
        g = conv(D_FF + lo)
        act_ref[:, lo:lo + FC] = (a * jax.nn.gelu(g, approximate=True)).reshape(SR, FC).astype(BF16)

    f = jnp.dot(act_ref[...], wdn_ref[...], preferred_element_type=F32).reshape(S, R, D_MODEL)
    y_ref[...] = _layer_norm(ALPHA * x1 + g2 * f, l2w_ref[...], l2b_ref[...])


def _post_call(x, mod, ya, yb, ga, gb, conv_in, wpa, wpb, wo, l1w, l1b, wup, bup, cw, cb, wdn, l2w, l2b,
               *, S, R):
    N, T, _ = x.shape
    nsb, nrb = N // S, T // R
    FC = D_FF // FFN_CHUNKS

    def tok(c):
        return pl.BlockSpec((S, R, c), lambda i, j: (i, j, 0))

    conv_spec = pl.BlockSpec((S, CONV_W - 1, 2 * D_FF), lambda i, j: (i, 0, 0))
    return pl.pallas_call(
        functools.partial(_post_kernel, S=S, R=R),
        grid=(nsb, nrb),
        in_specs=[tok(D_MODEL), pl.BlockSpec((S, 6, D_MODEL), lambda i, j: (i, 0, 0)),
                  tok(Q_A), tok(V_B), tok(D_MODEL), tok(D_MODEL), conv_spec,
                  _const_spec((Q_A, D_MODEL)), _const_spec((V_B, D_MODEL)), _const_spec((D_MODEL, D_MODEL)),
                  _const_spec((1, D_MODEL)), _const_spec((1, D_MODEL)),
                  _const_spec((D_MODEL, 2 * D_FF)), _const_spec((1, 2 * D_FF)),
                  _const_spec((CONV_W, 2 * D_FF)), _const_spec((1, 2 * D_FF)),
                  _const_spec((D_FF, D_MODEL)), _const_spec((1, D_MODEL)), _const_spec((1, D_MODEL))],
        out_specs=[tok(D_MODEL), conv_spec],
        out_shape=[jax.ShapeDtypeStruct((N, T, D_MODEL), F32),
                   jax.ShapeDtypeStruct((N, CONV_W - 1, 2 * D_FF), F32)],
        scratch_shapes=[pltpu.VMEM((S, R + 8, FC), F32), pltpu.VMEM((S * R, D_FF), BF16)],
        compiler_params=pltpu.CompilerParams(dimension_semantics=("parallel", "arbitrary")),
        name="post",
    )(x, mod, ya, yb, ga, gb, conv_in, wpa, wpb, wo, l1w, l1b, wup, bup, cw, cb, wdn, l2w, l2b)


def _pad_rows(a, rows):
    return jnp.pad(a, ((0, rows - a.shape[0]),) + ((0, 0),) * (a.ndim - 1))


def kernel(x_prompt, x_sample, cache_attn_k, cache_attn_v, state_mlstm_C, state_mlstm_n, state_mlstm_m, state_ffn_conv, c_prompt, c_sample, w_ada, b_ada, w_in, b_igate, b_fgate, attn_sinks, mlstm_norm_w, w_proj_a, w_proj_b, w_out, ln1_w, ln1_b, w_up, b_up, conv_w, conv_b, w_down, ln2_w, ln2_b):
    Bp, Tp, D = x_prompt.shape
    Bs, Ts, _ = x_sample.shape
    l = 0

    mod = _mod_call(jnp.concatenate([c_prompt, c_sample], axis=0), w_ada[l], b_ada[l][None])
    mod_p = mod[:Bp].reshape(Bp, 6, D)
    mod_s = mod[Bp:].reshape(Bs, 6, D)

    w = w_in[l]
    o_km, o_vm, o_i, o_f, o_o = Z_OFF[4], Z_OFF[5], Z_OFF[6], Z_OFF[7], Z_OFF[8]
    lane_pad = ((0, 0), (0, LANES - N_HEADS_B))
    w_main = jnp.concatenate([w[:, :o_km], w[:, o_vm:o_i], w[:, o_o:],
                              jnp.pad(w[:, o_i:o_f], lane_pad), jnp.pad(w[:, o_f:o_o], lane_pad)],
                             axis=1).astype(BF16)
    w_qt = w[:, :Z_OFF[1]].T.astype(BF16)
    w_kt = w[:, o_km:o_vm].T.astype(BF16)
    w_i = _pad_rows(w[:, o_i:o_f].T, 8).astype(BF16)
    w_f = _pad_rows(w[:, o_f:o_o].T, 8).astype(BF16)
    b_i = _pad_rows(b_igate[l][:, None], 8)
    b_f = _pad_rows(b_fgate[l][:, None], 8)
    b_ic = jnp.pad(b_igate[l][None], lane_pad)
    b_fc = jnp.pad(b_fgate[l][None], lane_pad)
    inproj_w = (w_main, w_qt, w_kt, w_i, w_f, b_i, b_f, b_ic, b_fc)
    sinks = attn_sinks[l]
    norm_w = mlstm_norm_w[l][None]
    post_w = (w_proj_a[l].astype(BF16), w_proj_b[l].astype(BF16), w_out[l].astype(BF16),
              ln1_w[l][None], ln1_b[l][None], w_up[l].astype(BF16), b_up[l][None], conv_w[l], conv_b[l][None],
              w_down[l].astype(BF16), ln2_w[l][None], ln2_b[l][None])

    qa, ka, va, qm, kt, vm, og, ga, gb, a_row, b_col, a_col = _inproj_call(
        x_prompt, mod_p, *inproj_w, S=1, R=MLSTM_CHUNK, L=MLSTM_CHUNK, q_transposed=True)
    ya = _attn_prompt_call(qa, ka, va, sinks)
    yb, p_c, p_n, p_m = _mlstm_call(
        qm, kt, vm, og, a_row, b_col, a_col, norm_w,
        jnp.zeros((Bp, N_HEADS_B, DV_B, DQK_B), F32), jnp.zeros((Bp, N_HEADS_B, DQK_B, 1), F32),
        jnp.zeros((Bp, 1, LANES), F32), SB=2, L=MLSTM_CHUNK)
    p_m = p_m[:, 0, :N_HEADS_B]
    y_prompt, p_conv = _post_call(x_prompt, mod_p, ya, yb, ga, gb,
                                  jnp.zeros((Bp, CONV_W - 1, 2 * D_FF), F32), *post_w, S=1, R=256)
    p_k = ka[:, Tp - KV_WIN:].reshape(Bp, KV_WIN, N_KV_A, HEAD_DIM_A)
    p_v = va[:, Tp - KV_WIN:].reshape(Bp, KV_WIN, N_KV_A, HEAD_DIM_A)

    qa, ka, va, qm, kt, vm, og, ga, gb, a_row, b_col, a_col = _inproj_call(
        x_sample, mod_s, *inproj_w, S=8, R=Ts, L=Ts, q_transposed=False)
    ya, s_k, s_v = _attn_sample_call(qa, ka, va, cache_attn_k[l].reshape(Bs, KV_WIN, KV_A_W),
                                     cache_attn_v[l].reshape(Bs, KV_WIN, KV_A_W), sinks)
    yb, s_c, s_n, s_m = _mlstm_call(
        qm, kt, vm, og, a_row, b_col, a_col, norm_w,
        state_mlstm_C[l], state_mlstm_n[l][:, :, :, None], jnp.pad(state_mlstm_m[l][:, None, :], ((0, 0),) + lane_pad),
        SB=4, L=Ts)
    s_m = s_m[:, 0, :N_HEADS_B]
    y_sample, s_conv = _post_call(x_sample, mod_s, ya, yb, ga, gb, state_ffn_conv[l], *post_w, S=8, R=Ts)

    def st(a, shape):
        return a.reshape((1,) + shape)

    return (y_prompt, y_sample,
            st(p_k, (Bp, KV_WIN, N_KV_A, HEAD_DIM_A)), st(p_v, (Bp, KV_WIN, N_KV_A, HEAD_DIM_A)),
            st(p_c, (Bp, N_HEADS_B, DV_B, DQK_B)), st(p_n, (Bp, N_HEADS_B, DQK_B)), st(p_m, (Bp, N_HEADS_B)),
            st(p_conv, (Bp, CONV_W - 1, 2 * D_FF)),
            st(s_k, (Bs, KV_WIN, N_KV_A, HEAD_DIM_A)), st(s_v, (Bs, KV_WIN, N_KV_A, HEAD_DIM_A)),
            st(s_c, (Bs, N_HEADS_B, DV_B, DQK_B)), st(s_n, (Bs, N_HEADS_B, DQK_B)), st(s_m, (Bs, N_HEADS_B)),
            st(s_conv, (Bs, CONV_W - 1, 2 * D_FF)))
```

```python
import functools

import numpy as np
import jax
import jax.numpy as jnp
from jax import lax
from jax.experimental import pallas as pl
from jax.experimental.pallas import tpu as pltpu

F32 = jnp.float32
BF16 = jnp.bfloat16

D_MODEL = 1024
DEPTH = 1
PAST_LEN = 2048
CHUNK = 64
N_HEADS_A = 16
N_KV_A = 2
HEAD_DIM_A = 64
GROUP_A = N_HEADS_A // N_KV_A
WINDOW = 128
WINDOW_CHUNKS = WINDOW // CHUNK
KV_WIN = min(WINDOW, PAST_LEN)
N_HEADS_B = 4
DQK_B = 128
DV_B = 256
D_FF = 2816
CONV_W = 3
LN_EPS = 1e-5
ALPHA = (2 * DEPTH) ** 0.25
Q_A = N_HEADS_A * HEAD_DIM_A
KV_A_W = N_KV_A * HEAD_DIM_A
QK_B = N_HEADS_B * DQK_B
V_B = N_HEADS_B * DV_B
Z_PARTS = (Q_A, KV_A_W, KV_A_W, QK_B, QK_B, V_B, N_HEADS_B, N_HEADS_B, V_B, D_MODEL, D_MODEL)
Z_OFF = tuple(int(v) for v in np.cumsum((0,) + Z_PARTS))

LANES = 128

M_QA = 0
M_KA = M_QA + Q_A
M_VA = M_KA + KV_A_W
M_QM = M_VA + KV_A_W
M_VM = M_QM + QK_B
M_OG = M_VM + V_B
M_GA = M_OG + V_B
M_GB = M_GA + D_MODEL
M_GI = M_GB + D_MODEL
M_GF = M_GI + LANES
M_END = M_GF + LANES
ATT_QB = 128
ATT_SUB = 4
ATT_W = 256
FFN_CHUNKS = 11
MLSTM_CHUNK = 256


def _bdot(a, b):
    return jnp.dot(a.astype(BF16), b.astype(BF16), preferred_element_type=F32)


def _bdot_nt(a, b):
    return lax.dot_general(a.astype(BF16), b.astype(BF16), (((1,), (1,)), ((), ())),
                           preferred_element_type=F32)


def _const_spec(shape):
    nd = len(shape)
    return pl.BlockSpec(shape, lambda *_: (0,) * nd, pipeline_mode=pl.Buffered(1))


def _layer_norm(x, w, b):
    mu = jnp.mean(x, -1, keepdims=True)
    xc = x - mu
    var = jnp.mean(jnp.square(xc), -1, keepdims=True)
    return xc * lax.rsqrt(var + LN_EPS) * w + b


def _mod_kernel(c_ref, w_ref, b_ref, o_ref):
    c = c_ref[...]
    s = c * jax.nn.sigmoid(c)
    o_ref[...] = _bdot(s, w_ref[...]) + b_ref[...]


def _mod_call(c_all, w_ada, b_ada):
    n = c_all.shape[0]
    tn = 1536
    return pl.pallas_call(
        _mod_kernel,
        grid=(6 * D_MODEL // tn,),
        in_specs=[pl.BlockSpec((n, D_MODEL), lambda j: (0, 0)),
                  pl.BlockSpec((D_MODEL, tn), lambda j: (0, j)),
                  pl.BlockSpec((1, tn), lambda j: (0, j))],
        out_specs=pl.BlockSpec((n, tn), lambda j: (0, j)),
        out_shape=jax.ShapeDtypeStruct((n, 6 * D_MODEL), F32),
        name="mod",
    )(c_all, w_ada, b_ada)


def _log_sigmoid(x):
    return jnp.minimum(x, 0.0) - jnp.log1p(jnp.exp(-jnp.abs(x)))


def _seg_scan(x, L, axis, op, ident):
    pos = lax.broadcasted_iota(jnp.int32, x.shape, axis) % L
    step = 1
    while step < L:
        x = op(x, jnp.where(pos >= step, pltpu.roll(x, step, axis=axis), ident))
        step *= 2
    return x


def _inproj_kernel(x_ref, mod_ref, w_ref, wq_ref, wk_ref, wi_ref, wf_ref, bi_ref, bf_ref, bic_ref, bfc_ref,
                   qa_ref, ka_ref, va_ref, qm_ref, kt_ref, vm_ref, og_ref, ga_ref, gb_ref,
                   ar_ref, bc_ref, ac_ref, *, S, R, L, q_transposed):
    SR = S * R
    sh = mod_ref[:, 0:1, :]
    sc = mod_ref[:, 1:2, :]
    h = (x_ref[...] * (1.0 + sc) + sh).reshape(SR, D_MODEL).astype(BF16)

    def proj(lo, hi):
        return jnp.dot(h, w_ref[:, lo:hi], preferred_element_type=F32)

    if q_transposed:
        qt = (_bdot_nt(wq_ref[...], h) * (HEAD_DIM_A ** -0.5)).astype(BF16)
        for s in range(S):
            qa_ref[s] = qt[:, s * R:(s + 1) * R]
    else:
        qa_ref[...] = (proj(M_QA, M_KA) * (HEAD_DIM_A ** -0.5)).astype(BF16).reshape(S, R, Q_A)
    ka_ref[...] = proj(M_KA, M_VA).reshape(S, R, KV_A_W)
    va_ref[...] = proj(M_VA, M_QM).reshape(S, R, KV_A_W)
    qm_ref[...] = proj(M_QM, M_VM).astype(BF16).reshape(S, R, QK_B)
    vm_ref[...] = proj(M_VM, M_OG).astype(BF16).reshape(S, R, V_B)
    og_ref[...] = proj(M_OG, M_GA).reshape(S, R, V_B)
    ga_ref[...] = proj(M_GA, M_GB).reshape(S, R, D_MODEL)
    gb_ref[...] = proj(M_GB, M_GI).reshape(S, R, D_MODEL)

    kt = (_bdot_nt(wk_ref[...], h) * (DQK_B ** -0.5)).astype(BF16)
    i_row = _bdot_nt(wi_ref[...], h) + bi_ref[...]
    logf_row = _log_sigmoid(_bdot_nt(wf_ref[...], h) + bf_ref[...])
    a_row = i_row - _seg_scan(logf_row, L, 1, jnp.add, 0.0)
    for s in range(S):
        kt_ref[s] = kt[:, s * R:(s + 1) * R]
        ar_ref[s] = a_row[:, s * R:(s + 1) * R]
    i_col = proj(M_GI, M_GF) + bic_ref[...]
    logf_col = _log_sigmoid(proj(M_GF, M_END) + bfc_ref[...])
    b_col = _seg_scan(logf_col, L, 0, jnp.add, 0.0)
    a_col = _seg_scan(i_col - b_col, L, 0, jnp.maximum, -jnp.inf)
    bc_ref[...] = b_col.reshape(S, R, LANES)
    ac_ref[...] = a_col.reshape(S, R, LANES)


def _inproj_call(x, mod, w_main, w_qt, w_kt, w_i, w_f, b_i, b_f, b_ic, b_fc, *, S, R, L, q_transposed):
    N, T, _ = x.shape
    nsb, nrb = N // S, T // R

    def tok(c):
        return pl.BlockSpec((S, R, c), lambda i, j: (i, j, 0))

    def tok_shape(c, dt):
        return jax.ShapeDtypeStruct((N, T, c), dt)

    def tr(c):
        return pl.BlockSpec((S, c, R), lambda i, j: (i, 0, j))

    return pl.pallas_call(
        functools.partial(_inproj_kernel, S=S, R=R, L=L, q_transposed=q_transposed),
        grid=(nsb, nrb),
        in_specs=[tok(D_MODEL),
                  pl.BlockSpec((S, 6, D_MODEL), lambda i, j: (i, 0, 0)),
                  _const_spec((D_MODEL, M_END)), _const_spec((Q_A, D_MODEL)), _const_spec((QK_B, D_MODEL)),
                  _const_spec((8, D_MODEL)), _const_spec((8, D_MODEL)),
                  _const_spec((8, 1)), _const_spec((8, 1)),
                  _const_spec((1, LANES)), _const_spec((1, LANES))],
        out_specs=[tr(Q_A) if q_transposed else tok(Q_A), tok(KV_A_W), tok(KV_A_W), tok(QK_B), tr(QK_B), tok(V_B),
                   tok(V_B), tok(D_MODEL), tok(D_MODEL), tr(8), tok(LANES), tok(LANES)],
        out_shape=[jax.ShapeDtypeStruct((N, Q_A, T), BF16) if q_transposed else tok_shape(Q_A, BF16),
                   tok_shape(KV_A_W, F32), tok_shape(KV_A_W, F32),
                   tok_shape(QK_B, BF16), jax.ShapeDtypeStruct((N, QK_B, T), BF16), tok_shape(V_B, BF16),
                   tok_shape(V_B, F32), tok_shape(D_MODEL, F32), tok_shape(D_MODEL, F32),
                   jax.ShapeDtypeStruct((N, 8, T), F32), tok_shape(LANES, F32), tok_shape(LANES, F32)],
        compiler_params=pltpu.CompilerParams(dimension_semantics=("parallel", "parallel")),
        name="inproj",
    )(x, mod, w_main, w_qt, w_kt, w_i, w_f, b_i, b_f, b_ic, b_fc)


def _alibi_slopes():
    return 2.0 ** (-8.0 * np.arange(1, N_HEADS_A + 1, dtype=np.float64) / N_HEADS_A)


def _attn_bias(qpos, kpos, kvalid, transposed=False):
    qpos = np.asarray(qpos)[:, None]
    kpos = np.asarray(kpos)[None, :]
    qc, kc = qpos // CHUNK, kpos // CHUNK
    visible = (kpos >= 0) & (kc <= qc) & (kc >= qc - WINDOW_CHUNKS) & np.asarray(kvalid)[None, :]
    dist = np.abs(qpos - kpos).astype(np.float64)
    slopes = _alibi_slopes()
    out = np.zeros((N_KV_A, GROUP_A // 2, qpos.shape[0], 2, kpos.shape[1]), np.float32)
    for kv in range(N_KV_A):
        for p in range(GROUP_A // 2):
            for half in range(2):
                hd = kv * GROUP_A + 2 * p + half
                out[kv, p, :, half, :] = np.where(visible, -slopes[hd] * dist, -np.inf)
    if transposed:
        return out.transpose(0, 3, 4, 1, 2).reshape(N_KV_A, 2 * kpos.shape[1], (GROUP_A // 2) * qpos.shape[0])
    return out.reshape(N_KV_A, (GROUP_A // 2) * qpos.shape[0], 2 * kpos.shape[1])


def _attn_core(q, kwin, vwin, bias_of, sinks_ref, store):
    Mq = q.shape[0]
    W = kwin.shape[0]
    npair = GROUP_A // 2
    lane = lax.broadcasted_iota(jnp.int32, (npair * Mq, LANES), 1)
    for kv in range(N_KV_A):
        kk = kwin[:, kv * HEAD_DIM_A:(kv + 1) * HEAD_DIM_A].astype(BF16)
        vv = vwin[:, kv * HEAD_DIM_A:(kv + 1) * HEAD_DIM_A].astype(BF16)
        zero = jnp.zeros_like(kk)
        one = jnp.ones_like(vv)
        k2 = jnp.concatenate([jnp.concatenate([kk, zero], axis=1),
                              jnp.concatenate([zero, kk], axis=1)], axis=0)
        v_aug = jnp.concatenate([vv, one, one, vv], axis=1)
        qs = jnp.concatenate([q[:, (kv * npair + p) * LANES:(kv * npair + p + 1) * LANES]
                              for p in range(npair)], axis=0)
        s = _bdot_nt(qs, k2) + bias_of(kv)
        probs, sink_terms = [], []
        for half in range(2):
            sh = s[:, half * W:(half + 1) * W]
            sink = jnp.concatenate(
                [jnp.full((Mq, 1), sinks_ref[kv * GROUP_A + 2 * p + half], F32) for p in range(npair)], axis=0)
            mx = jnp.maximum(jnp.max(sh, axis=1, keepdims=True), sink)
            probs.append(jnp.exp(sh - mx).astype(BF16))
            sink_terms.append(jnp.exp(sink - mx))
        o = jnp.dot(jnp.concatenate(probs, axis=0), v_aug, preferred_element_type=F32)
        oe, oo = o[:npair * Mq], o[npair * Mq:]
        ye = oe[:, :LANES] / (oe[:, LANES:] + sink_terms[0])
        yo = oo[:, LANES:] / (oo[:, :LANES] + sink_terms[1])
        y = jnp.where(lane < HEAD_DIM_A, ye, yo).astype(BF16)
        for p in range(npair):
            store((kv * npair + p) * LANES, y[p * Mq:(p + 1) * Mq])


def _attn_prompt_kernel(sinks_ref, qt_ref, kp_ref, kc_ref, vp_ref, vc_ref, bias0_ref, bias1_ref, o_ref):
    Mq = ATT_QB
    npair = GROUP_A // 2
    keys = jnp.concatenate([kp_ref[0], kc_ref[0]], axis=0)
    vals_t = jnp.concatenate([vp_ref[0], vc_ref[0]], axis=0).T.astype(BF16)
    ones = jnp.ones((HEAD_DIM_A, ATT_W), BF16)
    pair_of_lane = lax.broadcasted_iota(jnp.int32, (1, npair * Mq), 1) // Mq
    units = [(sub, kv) for sub in range(ATT_SUB) for kv in range(N_KV_A)]
    sts = {}
    for sub, kv in units:
        kk = keys[sub * Mq:sub * Mq + ATT_W, kv * HEAD_DIM_A:(kv + 1) * HEAD_DIM_A].astype(BF16)
        zero = jnp.zeros_like(kk)
        k2 = jnp.concatenate([jnp.concatenate([kk, zero], axis=1),
                              jnp.concatenate([zero, kk], axis=1)], axis=0)
        qt = jnp.concatenate([qt_ref[0, (kv * npair + p) * LANES:(kv * npair + p + 1) * LANES, sub * Mq:(sub + 1) * Mq]
                              for p in range(npair)], axis=1)
        sts[sub, kv] = jnp.dot(k2, qt, preferred_element_type=F32)
    probs, sink_terms = {}, {}
    for sub, kv in units:
        st = sts[sub, kv] + (bias0_ref[0, kv] if sub == 0 else bias1_ref[0, kv])
        for half in range(2):
            sh = st[half * ATT_W:(half + 1) * ATT_W]
            sink = jnp.zeros((1, npair * Mq), F32)
            for p in range(npair):
                sink = jnp.where(pair_of_lane == p, sinks_ref[kv * GROUP_A + 2 * p + half], sink)
            mx = jnp.maximum(jnp.max(sh, axis=0, keepdims=True), sink)
            probs[sub, kv, half] = jnp.exp(sh - mx).astype(BF16)
            sink_terms[sub, kv, half] = jnp.exp(sink - mx)
    ots = {}
    for sub, kv in units:
        vt = vals_t[kv * HEAD_DIM_A:(kv + 1) * HEAD_DIM_A, sub * Mq:sub * Mq + ATT_W]
        for half in range(2):
            lhs = jnp.concatenate([vt, ones] if half == 0 else [ones, vt], axis=0)
            ots[sub, kv, half] = jnp.dot(lhs, probs[sub, kv, half], preferred_element_type=F32)
    for sub, kv in units:
        oe, oo = ots[sub, kv, 0], ots[sub, kv, 1]
        ye = oe[:HEAD_DIM_A] / (oe[HEAD_DIM_A:] + sink_terms[sub, kv, 0])
        yo = oo[HEAD_DIM_A:] / (oo[:HEAD_DIM_A] + sink_terms[sub, kv, 1])
        yt = jnp.concatenate([ye, yo], axis=0)
        for p in range(npair):
            col = (kv * npair + p) * LANES
            o_ref[0, sub * Mq:(sub + 1) * Mq, col:col + LANES] = yt[:, p * Mq:(p + 1) * Mq].T.astype(BF16)


def _attn_prompt_call(qa_t, ka, va, sinks):
    N, _, T = qa_t.shape
    rows = ATT_QB * ATT_SUB
    rel_q = ATT_QB + np.arange(ATT_QB)
    rel_k = np.arange(ATT_W)
    bias = np.stack([_attn_bias(rel_q, rel_k, rel_k >= ATT_QB, transposed=True),
                     _attn_bias(rel_q, rel_k, rel_k >= 0, transposed=True)])
    kv_prev = pl.BlockSpec((1, ATT_QB, KV_A_W), lambda n, j: (n, jnp.maximum(ATT_SUB * j - 1, 0), 0))
    kv_cur = pl.BlockSpec((1, rows, KV_A_W), lambda n, j: (n, j, 0))
    return pl.pallas_call(
        _attn_prompt_kernel,
        grid=(N, T // rows),
        in_specs=[pl.BlockSpec(memory_space=pltpu.SMEM),
                  pl.BlockSpec((1, Q_A, rows), lambda n, j: (n, 0, j)),
                  kv_prev, kv_cur, kv_prev, kv_cur,
                  pl.BlockSpec((1,) + bias.shape[1:], lambda n, j: (jnp.minimum(j, 1), 0, 0, 0)),
                  pl.BlockSpec((1,) + bias.shape[1:], lambda n, j: (1, 0, 0, 0), pipeline_mode=pl.Buffered(1))],
        out_specs=pl.BlockSpec((1, rows, Q_A), lambda n, j: (n, j, 0)),
        out_shape=jax.ShapeDtypeStruct((N, T, Q_A), BF16),
        compiler_params=pltpu.CompilerParams(dimension_semantics=("parallel", "parallel")),
        name="attn_prompt",
    )(sinks, qa_t, ka, ka, va, va, jnp.asarray(bias), jnp.asarray(bias))


def _attn_sample_kernel(sinks_ref, q_ref, kn_ref, vn_ref, kc_ref, vc_ref, bias_ref,
                        o_ref, ko_ref, vo_ref, *, T):
    pad = jnp.zeros((ATT_W - KV_WIN - T, KV_A_W), F32)
    kwin = jnp.concatenate([kc_ref[0], kn_ref[0], pad], axis=0)
    vwin = jnp.concatenate([vc_ref[0], vn_ref[0], pad], axis=0)
    ko_ref[0] = kwin[T:T + KV_WIN]
    vo_ref[0] = vwin[T:T + KV_WIN]

    def store(col, val):
        o_ref[0, :, col:col + LANES] = val

    _attn_core(q_ref[0], kwin, vwin, lambda kv: bias_ref[kv], sinks_ref, store)


def _attn_sample_call(qa, ka, va, cache_k, cache_v, sinks):
    N, T, _ = qa.shape
    qpos = PAST_LEN + np.arange(T)
    kpos = PAST_LEN - KV_WIN + np.arange(ATT_W)
    bias = _attn_bias(qpos, kpos, np.arange(ATT_W) < KV_WIN + T)
    new = pl.BlockSpec((1, T, KV_A_W), lambda n: (n, 0, 0))
    cache = pl.BlockSpec((1, KV_WIN, KV_A_W), lambda n: (n, 0, 0))
    cache_shape = jax.ShapeDtypeStruct((N, KV_WIN, KV_A_W), F32)
    return pl.pallas_call(
        functools.partial(_attn_sample_kernel, T=T),
        grid=(N,),
        in_specs=[pl.BlockSpec(memory_space=pltpu.SMEM),
                  pl.BlockSpec((1, T, Q_A), lambda n: (n, 0, 0)),
                  new, new, cache, cache,
                  _const_spec(bias.shape)],
        out_specs=[pl.BlockSpec((1, T, Q_A), lambda n: (n, 0, 0)), cache, cache],
        out_shape=[jax.ShapeDtypeStruct((N, T, Q_A), BF16), cache_shape, cache_shape],
        compiler_params=pltpu.CompilerParams(dimension_semantics=("parallel",)),
        name="attn_sample",
    )(sinks, qa, ka, va, cache_k, cache_v, jnp.asarray(bias))


def _mlstm_kernel(q_ref, kt_ref, v_ref, og_ref, ar_ref, bc_ref, ac_ref, nw_ref, c0_ref, n0_ref, m0_ref,
                  y_ref, c_ref, n_ref, m_ref, ct_ref, *, SB, L):
    j = pl.program_id(1)
    heads = [(sb, hd) for sb in range(SB) for hd in range(N_HEADS_B)]

    @pl.when(j == 0)
    def _():
        for sb, hd in heads:
            ct_ref[sb, hd, :, :DV_B] = c0_ref[sb, hd].T
            ct_ref[sb, hd, :, DV_B:] = jnp.broadcast_to(n0_ref[sb, hd], (DQK_B, LANES))
        m_ref[...] = m0_ref[...]

    causal = (lax.broadcasted_iota(jnp.int32, (L, L), 0) >= lax.broadcasted_iota(jnp.int32, (L, L), 1))
    ones = jnp.ones((L, LANES), BF16)
    gates = []
    for sb in range(SB):
        a4 = ac_ref[sb]
        b4 = bc_ref[sb]
        m_row = m_ref[sb]
        a_last = a4[L - 1:L]
        b_last = b4[L - 1:L]
        mx = jnp.maximum(a_last, m_row)
        m_new = b_last + mx
        gates.append(dict(a4=a4, b4=b4, a_last=a_last, m_row=m_row, decay4=jnp.exp(b_last + m_row - m_new),
                          rho4=jnp.exp(a_last - mx)))
        m_ref[sb] = m_new

    def lanes(x):
        return jnp.broadcast_to(x, (x.shape[0], LANES))

    def col(x, hd):
        return x[:, hd:hd + 1]

    qs = {h: q_ref[h[0], :, h[1] * DQK_B:(h[1] + 1) * DQK_B] for h in heads}
    kts = {h: kt_ref[h[0], h[1] * DQK_B:(h[1] + 1) * DQK_B, :] for h in heads}
    cts = {h: ct_ref[h[0], h[1]] for h in heads}
    qk = {h: jnp.dot(qs[h], kts[h], preferred_element_type=F32) for h in heads}
    qc = {h: jnp.dot(qs[h], cts[h].astype(BF16), preferred_element_type=F32) for h in heads}
    lhs, a_bs = {}, {}
    for h in heads:
        sb, hd = h
        gt = gates[sb]
        a_row = ar_ref[sb, hd:hd + 1, :]
        a_bs[h] = lanes(col(gt["a4"], hd))
        a_full = jnp.concatenate([a_bs[h]] * (L // LANES), axis=1) if L >= LANES else a_bs[h][:, :L]
        w = jnp.exp(jnp.where(causal, a_row - a_full, -jnp.inf))
        s = (qk[h] * w).astype(BF16)
        kw = (kts[h].astype(F32) * jnp.exp(a_row - col(gt["a_last"], hd))).astype(BF16)
        lhs[h] = jnp.concatenate([s, kw], axis=0)
    pu = {}
    for h in heads:
        sb, hd = h
        v_aug = jnp.concatenate([v_ref[sb, :, hd * DV_B:(hd + 1) * DV_B], ones], axis=1)
        pu[h] = jnp.dot(lhs[h], v_aug, preferred_element_type=F32)
    ones_sum = jnp.ones((DV_B, LANES), BF16)
    for h in heads:
        sb, hd = h
        gt = gates[sb]
        a_b = a_bs[h]
        m_b = lanes(col(gt["m_row"], hd))
        big = jnp.maximum(a_b, m_b)
        r_b = jnp.exp(a_b - big)
        g_b = jnp.exp(m_b - big)
        e_b = jnp.exp(-(lanes(col(gt["b4"], hd)) + big))
        p, c = pu[h], qc[h]
        den = r_b * p[:L, DV_B:] + g_b * c[:, DV_B:]
        inv = 1.0 / jnp.maximum(jnp.abs(den), e_b)
        hh = jnp.concatenate([(r_b * p[:L, k * LANES:(k + 1) * LANES] + g_b * c[:, k * LANES:(k + 1) * LANES]) * inv
                              for k in range(DV_B // LANES)], axis=1)
        mu = jnp.dot(hh.astype(BF16), ones_sum, preferred_element_type=F32) * (1.0 / DV_B)
        hc = hh - jnp.concatenate([mu] * (DV_B // LANES), axis=1)
        var = jnp.dot(jnp.square(hc).astype(BF16), ones_sum, preferred_element_type=F32) * (1.0 / DV_B)
        rstd = lax.rsqrt(var + LN_EPS)
        yn = hc * jnp.concatenate([rstd] * (DV_B // LANES), axis=1) * nw_ref[:, hd * DV_B:(hd + 1) * DV_B]
        og = og_ref[sb, :, hd * DV_B:(hd + 1) * DV_B]
        y_ref[sb, :, hd * DV_B:(hd + 1) * DV_B] = (jax.nn.sigmoid(og) * yn).astype(BF16)
        ct_ref[sb, hd] = col(gt["decay4"], hd) * cts[h] + col(gt["rho4"], hd) * pu[h][L:]

    @pl.when(j == pl.num_programs(1) - 1)
    def _():
        for sb, hd in heads:
            c_ref[sb, hd] = ct_ref[sb, hd, :, :DV_B].T
            n_ref[sb, hd] = ct_ref[sb, hd, :, DV_B:DV_B + 1]


def _mlstm_call(qm, kt, vm, og, a_row, b_col, a_col, norm_w, c0, n0, m0, *, SB, L):
    N, T, _ = qm.shape

    def tok(c):
        return pl.BlockSpec((SB, L, c), lambda i, j: (i, j, 0))

    def tr(c):
        return pl.BlockSpec((SB, c, L), lambda i, j: (i, 0, j))

    st_c = pl.BlockSpec((SB, N_HEADS_B, DV_B, DQK_B), lambda i, j: (i, 0, 0, 0))
    st_n = pl.BlockSpec((SB, N_HEADS_B, DQK_B, 1), lambda i, j: (i, 0, 0, 0))
    st_m = pl.BlockSpec((SB, 1, LANES), lambda i, j: (i, 0, 0))
    return pl.pallas_call(
        functools.partial(_mlstm_kernel, SB=SB, L=L),
        grid=(N // SB, T // L),
        in_specs=[tok(QK_B), tr(QK_B), tok(V_B), tok(V_B), tr(8), tok(LANES), tok(LANES),
                  _const_spec((1, V_B)), st_c, st_n, st_m],
        out_specs=[tok(V_B), st_c, st_n, st_m],
        out_shape=[jax.ShapeDtypeStruct((N, T, V_B), BF16),
                   jax.ShapeDtypeStruct((N, N_HEADS_B, DV_B, DQK_B), F32),
                   jax.ShapeDtypeStruct((N, N_HEADS_B, DQK_B, 1), F32),
                   jax.ShapeDtypeStruct((N, 1, LANES), F32)],
        scratch_shapes=[pltpu.VMEM((SB, N_HEADS_B, DQK_B, DV_B + LANES), F32)],
        compiler_params=pltpu.CompilerParams(dimension_semantics=("parallel", "arbitrary")),
        name="mlstm",
    )(qm, kt, vm, og, a_row, b_col, a_col, norm_w, c0, n0, m0)


def _post_kernel(x_ref, mod_ref, ya_ref, yb_ref, ga_ref, gb_ref, cin_ref,
                 wpa_ref, wpb_ref, wo_ref, l1w_ref, l1b_ref, wup_ref, bup_ref, cw_ref, cb_ref,
                 wdn_ref, l2w_ref, l2b_ref,
                 y_ref, cout_ref, ctx_ref, act_ref, *, S, R):
    SR = S * R
    FC = D_FF // FFN_CHUNKS

    @pl.when(pl.program_id(1) == 0)
    def _():
        cout_ref[...] = cin_ref[...]

    g1 = mod_ref[:, 2:3, :]
    sh2 = mod_ref[:, 3:4, :]
    sc2 = mod_ref[:, 4:5, :]
    g2 = mod_ref[:, 5:6, :]
    x = x_ref[...]

    pa = jnp.dot(ya_ref[...].reshape(SR, Q_A), wpa_ref[...], preferred_element_type=F32)
    pb = jnp.dot(yb_ref[...].reshape(SR, V_B), wpb_ref[...], preferred_element_type=F32)
    merged = (jax.nn.sigmoid(ga_ref[...].reshape(SR, D_MODEL)) * pa
              + jax.nn.sigmoid(gb_ref[...].reshape(SR, D_MODEL)) * pb)
    mo = _bdot(merged, wo_ref[...]).reshape(S, R, D_MODEL)
    x1 = _layer_norm(ALPHA * x + g1 * mo, l1w_ref[...], l1b_ref[...])
    h2 = (x1 * (1.0 + sc2) + sh2).reshape(SR, D_MODEL).astype(BF16)

    def conv(lo):
        u = (jnp.dot(h2, wup_ref[:, lo:lo + FC], preferred_element_type=F32)
             + bup_ref[:, lo:lo + FC]).reshape(S, R, FC)
        ctx_ref[:, 6:8, :] = cout_ref[:, :, lo:lo + FC]
        ctx_ref[:, 8:, :] = u
        cout_ref[:, :, lo:lo + FC] = ctx_ref[:, R + 6:R + 8, :]
        return (cb_ref[:, lo:lo + FC]
                + cw_ref[0:1, lo:lo + FC] * ctx_ref[:, 6:R + 6, :]
                + cw_ref[1:2, lo:lo + FC] * ctx_ref[:, 7:R + 7, :]
                + cw_ref[2:3, lo:lo + FC] * u)

    for ci in range(FFN_CHUNKS):
        lo = ci * FC
        a = conv(lo)
        g = conv(D_FF + lo)
        act_ref[:, lo:lo + FC] = (a * jax.nn.gelu(g, approximate=True)).reshape(SR, FC).astype(BF16)

    f = jnp.dot(act_ref[...], wdn_ref[...], preferred_element_type=F32).reshape(S, R, D_MODEL)
    y_ref[...] = _layer_norm(ALPHA * x1 + g2 * f, l2w_ref[...], l2b_ref[...])


def _post_call(x, mod, ya, yb, ga, gb, conv_in, wpa, wpb, wo, l1w, l1b, wup, bup, cw, cb, wdn, l2w, l2b,
               *, S, R):
    N, T, _ = x.shape
    nsb, nrb = N // S, T // R
    FC = D_FF // FFN_CHUNKS

    def tok(c):
        return pl.BlockSpec((S, R, c), lambda i, j: (i, j, 0))

    conv_spec = pl.BlockSpec((S, CONV_W - 1, 2 * D_FF), lambda i, j: (i, 0, 0))
    return pl.pallas_call(
        functools.partial(_post_kernel, S=S, R=R),
        grid=(nsb, nrb),
        in_specs=[tok(D_MODEL), pl.BlockSpec((S, 6, D_MODEL), lambda i, j: (i, 0, 0)),
                  tok(Q_A), tok(V_B), tok(D_MODEL), tok(D_MODEL), conv_spec,
                  _const_spec((Q_A, D_MODEL)), _const_spec((V_B, D_MODEL)), _const_spec((D_MODEL, D_MODEL)),
                  _const_spec((1, D_MODEL)), _const_spec((1, D_MODEL)),
                  _const_spec((D_MODEL, 2 * D_FF)), _const_spec((1, 2 * D_FF)),
                  _const_spec((CONV_W, 2 * D_FF)), _const_spec((1, 2 * D_FF)),
                  _const_spec((D_FF, D_MODEL)), _const_spec((1, D_MODEL)), _const_spec((1, D_MODEL))],
        out_specs=[tok(D_MODEL), conv_spec],
        out_shape=[jax.ShapeDtypeStruct((N, T, D_MODEL), F32),
                   jax.ShapeDtypeStruct((N, CONV_W - 1, 2 * D_FF), F32)],
        scratch_shapes=[pltpu.VMEM((S, R + 8, FC), F32), pltpu.VMEM((S * R, D_FF), BF16)],
        compiler_params=pltpu.CompilerParams(dimension_semantics=("parallel", "arbitrary")),
        name="post",
    )(x, mod, ya, yb, ga, gb, conv_in, wpa, wpb, wo, l1w, l1b, wup, bup, cw, cb, wdn, l2w, l2b)


def _pad_rows(a, rows):
    return jnp.pad(a, ((0, rows - a.shape[0]),) + ((0, 0),) * (a.ndim - 1))


def kernel(x_prompt, x_sample, cache_attn_k, cache_attn_v, state_mlstm_C, state_mlstm_n, state_mlstm_m, state_ffn_conv, c_prompt, c_sample, w_ada, b_ada, w_in, b_igate, b_fgate, attn_sinks, mlstm_norm_w, w_proj_a, w_proj_b, w_out, ln1_w, ln1_b, w_up, b_up, conv_w, conv_b, w_down, ln2_w, ln2_b):
    Bp, Tp, D = x_prompt.shape
    Bs, Ts, _ = x_sample.shape
    l = 0

    mod = _mod_call(jnp.concatenate([c_prompt, c_sample], axis=0), w_ada[l], b_ada[l][None])
    mod_p = mod[:Bp].reshape(Bp, 6, D)
    mod_s = mod[Bp:].reshape(Bs, 6, D)

    w = w_in[l]
    o_km, o_vm, o_i, o_f, o_o = Z_OFF[4], Z_OFF[5], Z_OFF[6], Z_OFF[7], Z_OFF[8]
    lane_pad = ((0, 0), (0, LANES - N_HEADS_B))
    w_main = jnp.concatenate([w[:, :o_km], w[:, o_vm:o_i], w[:, o_o:],
                              jnp.pad(w[:, o_i:o_f], lane_pad), jnp.pad(w[:, o_f:o_o], lane_pad)],
                             axis=1).astype(BF16)
    w_qt = w[:, :Z_OFF[1]].T.astype(BF16)
    w_kt = w[:, o_km:o_vm].T.astype(BF16)
    w_i = _pad_rows(w[:, o_i:o_f].T, 8).astype(BF16)
    w_f = _pad_rows(w[:, o_f:o_o].T, 8).astype(BF16)
    b_i = _pad_rows(b_igate[l][:, None], 8)
    b_f = _pad_rows(b_fgate[l][:, None], 8)
    b_ic = jnp.pad(b_igate[l][None], lane_pad)
    b_fc = jnp.pad(b_fgate[l][None], lane_pad)
    inproj_w = (w_main, w_qt, w_kt, w_i, w_f, b_i, b_f, b_ic, b_fc)
    sinks = attn_sinks[l]
    norm_w = mlstm_norm_w[l][None]
    post_w = (w_proj_a[l].astype(BF16), w_proj_b[l].astype(BF16), w_out[l].astype(BF16),
              ln1_w[l][None], ln1_b[l][None], w_up[l].astype(BF16), b_up[l][None], conv_w[l], conv_b[l][None],
              w_down[l].astype(BF16), ln2_w[l][None], ln2_b[l][None])

    qa, ka, va, qm, kt, vm, og, ga, gb, a_row, b_col, a_col = _inproj_call(
        x_prompt, mod_p, *inproj_w, S=1, R=MLSTM_CHUNK, L=MLSTM_CHUNK, q_transposed=True)
    ya = _attn_prompt_call(qa, ka, va, sinks)
    yb, p_c, p_n, p_m = _mlstm_call(
        qm, kt, vm, og, a_row, b_col, a_col, norm_w,
        jnp.zeros((Bp, N_HEADS_B, DV_B, DQK_B), F32), jnp.zeros((Bp, N_HEADS_B, DQK_B, 1), F32),
        jnp.zeros((Bp, 1, LANES), F32), SB=2, L=MLSTM_CHUNK)
    p_m = p_m[:, 0, :N_HEADS_B]
    y_prompt, p_conv = _post_call(x_prompt, mod_p, ya, yb, ga, gb,
                                  jnp.zeros((Bp, CONV_W - 1, 2 * D_FF), F32), *post_w, S=1, R=256)
    p_k = ka[:, Tp - KV_WIN:].reshape(Bp, KV_WIN, N_KV_A, HEAD_DIM_A)
    p_v = va[:, Tp - KV_WIN:].reshape(Bp, KV_WIN, N_KV_A, HEAD_DIM_A)

    qa, ka, va, qm, kt, vm, og, ga, gb, a_row, b_col, a_col = _inproj_call(
        x_sample, mod_s, *inproj_w, S=8, R=Ts, L=Ts, q_transposed=False)
    ya, s_k, s_v = _attn_sample_call(qa, ka, va, cache_attn_k[l].reshape(Bs, KV_WIN, KV_A_W),
                                     cache_attn_v[l].reshape(Bs, KV_WIN, KV_A_W), sinks)
    yb, s_c, s_n, s_m = _mlstm_call(
        qm, kt, vm, og, a_row, b_col, a_col, norm_w,
        state_mlstm_C[l], state_mlstm_n[l][:, :, :, None], jnp.pad(state_mlstm_m[l][:, None, :], ((0, 0),) + lane_pad),
        SB=4, L=Ts)
    s_m = s_m[:, 0, :N_HEADS_B]
    y_sample, s_conv = _post_call(x_sample, mod_s, ya, yb, ga, gb, state_ffn_conv[l], *post_w, S=8, R=Ts)

    def st(a, shape):
        return a.reshape((1,) + shape)

    return (y_prompt, y_sample,
            st(p_k, (Bp, KV_WIN, N_KV_A, HEAD_DIM_A)), st(p_v, (Bp, KV_WIN, N_KV_A, HEAD_DIM_A)),
            st(p_c, (Bp, N_HEADS_B, DV_B, DQK_B)), st(p_n, (Bp, N_HEADS_B, DQK_B)), st(p_m, (Bp, N_HEADS_B)),
            st(p_conv, (Bp, CONV_W - 1, 2 * D_FF)),
            st(s_k, (Bs, KV_WIN, N_KV_A, HEAD_DIM_A)), st(s_v, (Bs, KV_WIN, N_KV_A, HEAD_DIM_A)),
            st(s_c, (Bs, N_HEADS_B, DV_B, DQK_B)), st(s_n, (Bs, N_HEADS_B, DQK_B)), st(s_m, (Bs, N_HEADS_B)),
            st(s_conv, (Bs, CONV_W - 1, 2 * D_FF)))
```

```python
import functools

import numpy as np
import jax
import jax.numpy as jnp
from jax import lax
from jax.experimental import pallas as pl
from jax.experimental.pallas import tpu as pltpu

F32 = jnp.float32
BF16 = jnp.bfloat16

D_MODEL = 1024
DEPTH = 1
PAST_LEN = 2048
CHUNK = 64
N_HEADS_A = 16
N_KV_A = 2
HEAD_DIM_A = 64
GROUP_A = N_HEADS_A // N_KV_A
WINDOW = 128
WINDOW_CHUNKS = WINDOW // CHUNK
KV_WIN = min(WINDOW, PAST_LEN)
N_HEADS_B = 4
DQK_B = 128
DV_B = 256
D_FF = 2816
CONV_W = 3
LN_EPS = 1e-5
ALPHA = (2 * DEPTH) ** 0.25
Q_A = N_HEADS_A * HEAD_DIM_A
KV_A_W = N_KV_A * HEAD_DIM_A
QK_B = N_HEADS_B * DQK_B
V_B = N_HEADS_B * DV_B
Z_PARTS = (Q_A, KV_A_W, KV_A_W, QK_B, QK_B, V_B, N_HEADS_B, N_HEADS_B, V_B, D_MODEL, D_MODEL)
Z_OFF = tuple(int(v) for v in np.cumsum((0,) + Z_PARTS))

LANES = 128

M_QA = 0
M_KA = M_QA + Q_A
M_VA = M_KA + KV_A_W
M_QM = M_VA + KV_A_W
M_VM = M_QM + QK_B
M_OG = M_VM + V_B
M_GA = M_OG + V_B
M_GB = M_GA + D_MODEL
M_GI = M_GB + D_MODEL
M_GF = M_GI + LANES
M_END = M_GF + LANES
ATT_QB = 128
ATT_SUB = 4
ATT_W = 256
FFN_CHUNKS = 11
MLSTM_CHUNK = 256
INPROJ_ROWS = 512


def _bdot(a, b):
    return jnp.dot(a.astype(BF16), b.astype(BF16), preferred_element_type=F32)


def _bdot_nt(a, b):
    return lax.dot_general(a.astype(BF16), b.astype(BF16), (((1,), (1,)), ((), ())),
                           preferred_element_type=F32)


def _const_spec(shape):
    nd = len(shape)
    return pl.BlockSpec(shape, lambda *_: (0,) * nd, pipeline_mode=pl.Buffered(1))


def _layer_norm(x, w, b):
    mu = jnp.mean(x, -1, keepdims=True)
    xc = x - mu
    var = jnp.mean(jnp.square(xc), -1, keepdims=True)
    return xc * lax.rsqrt(var + LN_EPS) * w + b


def _mod_kernel(c_ref, w_ref, b_ref, o_ref):
    c = c_ref[...]
    s = c * jax.nn.sigmoid(c)
    o_ref[...] = _bdot(s, w_ref[...]) + b_ref[...]


def _mod_call(c_all, w_ada, b_ada):
    n = c_all.shape[0]
    tn = 1536
    return pl.pallas_call(
        _mod_kernel,
        grid=(6 * D_MODEL // tn,),
        in_specs=[pl.BlockSpec((n, D_MODEL), lambda j: (0, 0)),
                  pl.BlockSpec((D_MODEL, tn), lambda j: (0, j)),
                  pl.BlockSpec((1, tn), lambda j: (0, j))],
        out_specs=pl.BlockSpec((n, tn), lambda j: (0, j)),
        out_shape=jax.ShapeDtypeStruct((n, 6 * D_MODEL), F32),
        name="mod",
    )(c_all, w_ada, b_ada)


def _log_sigmoid(x):
    return jnp.minimum(x, 0.0) - jnp.log1p(jnp.exp(-jnp.abs(x)))


def _seg_scan(x, L, axis, op, ident):
    pos = lax.broadcasted_iota(jnp.int32, x.shape, axis) % L
    step = 1
    while step < L:
        x = op(x, jnp.where(pos >= step, pltpu.roll(x, step, axis=axis), ident))
        step *= 2
    return x


def _inproj_kernel(x_ref, mod_ref, w_ref, wq_ref, wk_ref, wi_ref, wf_ref, bi_ref, bf_ref, bic_ref, bfc_ref,
                   qa_ref, ka_ref, va_ref, qm_ref, kt_ref, vm_ref, og_ref, ga_ref, gb_ref,
                   ar_ref, bc_ref, ac_ref, *, S, R, L, q_transposed):
    SR = S * R
    sh = mod_ref[:, 0:1, :]
    sc = mod_ref[:, 1:2, :]
    h = (x_ref[...] * (1.0 + sc) + sh).reshape(SR, D_MODEL).astype(BF16)

    def proj(lo, hi):
        return jnp.dot(h, w_ref[:, lo:hi], preferred_element_type=F32)

    if q_transposed:
        qt = (_bdot_nt(wq_ref[...], h) * (HEAD_DIM_A ** -0.5)).astype(BF16)
        for s in range(S):
            qa_ref[s] = qt[:, s * R:(s + 1) * R]
    else:
        qa_ref[...] = (proj(M_QA, M_KA) * (HEAD_DIM_A ** -0.5)).astype(BF16).reshape(S, R, Q_A)
    ka_ref[...] = proj(M_KA, M_VA).reshape(S, R, KV_A_W)
    va_ref[...] = proj(M_VA, M_QM).reshape(S, R, KV_A_W)
    qm_ref[...] = proj(M_QM, M_VM).astype(BF16).reshape(S, R, QK_B)
    vm_ref[...] = proj(M_VM, M_OG).astype(BF16).reshape(S, R, V_B)
    og_ref[...] = proj(M_OG, M_GA).reshape(S, R, V_B)
    ga_ref[...] = proj(M_GA, M_GB).reshape(S, R, D_MODEL)
    gb_ref[...] = proj(M_GB, M_GI).reshape(S, R, D_MODEL)

    kt = (_bdot_nt(wk_ref[...], h) * (DQK_B ** -0.5)).astype(BF16)
    i_row = _bdot_nt(wi_ref[...], h) + bi_ref[...]
    logf_row = _log_sigmoid(_bdot_nt(wf_ref[...], h) + bf_ref[...])
    a_row = i_row - _seg_scan(logf_row, L, 1, jnp.add, 0.0)
    for s in range(S):
        kt_ref[s] = kt[:, s * R:(s + 1) * R]
        ar_ref[s] = a_row[:, s * R:(s + 1) * R]
    i_col = proj(M_GI, M_GF) + bic_ref[...]
    logf_col = _log_sigmoid(proj(M_GF, M_END) + bfc_ref[...])
    b_col = _seg_scan(logf_col, L, 0, jnp.add, 0.0)
    a_col = _seg_scan(i_col - b_col, L, 0, jnp.maximum, -jnp.inf)
    bc_ref[...] = b_col.reshape(S, R, LANES)
    ac_ref[...] = a_col.reshape(S, R, LANES)


def _inproj_call(x, mod, w_main, w_qt, w_kt, w_i, w_f, b_i, b_f, b_ic, b_fc, *, S, R, L, q_transposed):
    N, T, _ = x.shape
    nsb, nrb = N // S, T // R

    def tok(c):
        return pl.BlockSpec((S, R, c), lambda i, j: (i, j, 0))

    def tok_shape(c, dt):
        return jax.ShapeDtypeStruct((N, T, c), dt)

    def tr(c):
        return pl.BlockSpec((S, c, R), lambda i, j: (i, 0, j))

    return pl.pallas_call(
        functools.partial(_inproj_kernel, S=S, R=R, L=L, q_transposed=q_transposed),
        grid=(nsb, nrb),
        in_specs=[tok(D_MODEL),
                  pl.BlockSpec((S, 6, D_MODEL), lambda i, j: (i, 0, 0)),
                  _const_spec((D_MODEL, M_END)), _const_spec((Q_A, D_MODEL)), _const_spec((QK_B, D_MODEL)),
                  _const_spec((8, D_MODEL)), _const_spec((8, D_MODEL)),
                  _const_spec((8, 1)), _const_spec((8, 1)),
                  _const_spec((1, LANES)), _const_spec((1, LANES))],
        out_specs=[tr(Q_A) if q_transposed else tok(Q_A), tok(KV_A_W), tok(KV_A_W), tok(QK_B), tr(QK_B), tok(V_B),
                   tok(V_B), tok(D_MODEL), tok(D_MODEL), tr(8), tok(LANES), tok(LANES)],
        out_shape=[jax.ShapeDtypeStruct((N, Q_A, T), BF16) if q_transposed else tok_shape(Q_A, BF16),
                   tok_shape(KV_A_W, F32), tok_shape(KV_A_W, F32),
                   tok_shape(QK_B, BF16), jax.ShapeDtypeStruct((N, QK_B, T), BF16), tok_shape(V_B, BF16),
                   tok_shape(V_B, F32), tok_shape(D_MODEL, F32), tok_shape(D_MODEL, F32),
                   jax.ShapeDtypeStruct((N, 8, T), F32), tok_shape(LANES, F32), tok_shape(LANES, F32)],
        compiler_params=pltpu.CompilerParams(dimension_semantics=("parallel", "parallel")),
        name="inproj",
    )(x, mod, w_main, w_qt, w_kt, w_i, w_f, b_i, b_f, b_ic, b_fc)


def _alibi_slopes():
    return 2.0 ** (-8.0 * np.arange(1, N_HEADS_A + 1, dtype=np.float64) / N_HEADS_A)


def _attn_bias(qpos, kpos, kvalid, transposed=False):
    qpos = np.asarray(qpos)[:, None]
    kpos = np.asarray(kpos)[None, :]
    qc, kc = qpos // CHUNK, kpos // CHUNK
    visible = (kpos >= 0) & (kc <= qc) & (kc >= qc - WINDOW_CHUNKS) & np.asarray(kvalid)[None, :]
    dist = np.abs(qpos - kpos).astype(np.float64)
    slopes = _alibi_slopes()
    out = np.zeros((N_KV_A, GROUP_A // 2, qpos.shape[0], 2, kpos.shape[1]), np.float32)
    for kv in range(N_KV_A):
        for p in range(GROUP_A // 2):
            for half in range(2):
                hd = kv * GROUP_A + 2 * p + half
                out[kv, p, :, half, :] = np.where(visible, -slopes[hd] * dist, -np.inf)
    if transposed:
        return out.transpose(0, 3, 4, 1, 2).reshape(N_KV_A, 2 * kpos.shape[1], (GROUP_A // 2) * qpos.shape[0])
    return out.reshape(N_KV_A, (GROUP_A // 2) * qpos.shape[0], 2 * kpos.shape[1])


def _attn_core(q, kwin, vwin, bias_of, sinks_ref, store):
    Mq = q.shape[0]
    W = kwin.shape[0]
    npair = GROUP_A // 2
    lane = lax.broadcasted_iota(jnp.int32, (npair * Mq, LANES), 1)
    for kv in range(N_KV_A):
        kk = kwin[:, kv * HEAD_DIM_A:(kv + 1) * HEAD_DIM_A].astype(BF16)
        vv = vwin[:, kv * HEAD_DIM_A:(kv + 1) * HEAD_DIM_A].astype(BF16)
        zero = jnp.zeros_like(kk)
        one = jnp.ones_like(vv)
        k2 = jnp.concatenate([jnp.concatenate([kk, zero], axis=1),
                              jnp.concatenate([zero, kk], axis=1)], axis=0)
        v_aug = jnp.concatenate([vv, one, one, vv], axis=1)
        qs = jnp.concatenate([q[:, (kv * npair + p) * LANES:(kv * npair + p + 1) * LANES]
                              for p in range(npair)], axis=0)
        s = _bdot_nt(qs, k2) + bias_of(kv)
        probs, sink_terms = [], []
        for half in range(2):
            sh = s[:, half * W:(half + 1) * W]
            sink = jnp.concatenate(
                [jnp.full((Mq, 1), sinks_ref[kv * GROUP_A + 2 * p + half], F32) for p in range(npair)], axis=0)
            mx = jnp.maximum(jnp.max(sh, axis=1, keepdims=True), sink)
            probs.append(jnp.exp(sh - mx).astype(BF16))
            sink_terms.append(jnp.exp(sink - mx))
        o = jnp.dot(jnp.concatenate(probs, axis=0), v_aug, preferred_element_type=F32)
        oe, oo = o[:npair * Mq], o[npair * Mq:]
        ye = oe[:, :LANES] / (oe[:, LANES:] + sink_terms[0])
        yo = oo[:, LANES:] / (oo[:, :LANES] + sink_terms[1])
        y = jnp.where(lane < HEAD_DIM_A, ye, yo).astype(BF16)
        for p in range(npair):
            store((kv * npair + p) * LANES, y[p * Mq:(p + 1) * Mq])


def _attn_prompt_kernel(sinks_ref, qt_ref, kp_ref, kc_ref, vp_ref, vc_ref, bias0_ref, bias1_ref, o_ref):
    Mq = ATT_QB
    npair = GROUP_A // 2
    keys = jnp.concatenate([kp_ref[0], kc_ref[0]], axis=0)
    vals_t = jnp.concatenate([vp_ref[0], vc_ref[0]], axis=0).T.astype(BF16)
    ones = jnp.ones((HEAD_DIM_A, ATT_W), BF16)
    pair_of_lane = lax.broadcasted_iota(jnp.int32, (1, npair * Mq), 1) // Mq
    units = [(sub, kv) for sub in range(ATT_SUB) for kv in range(N_KV_A)]
    sts = {}
    for sub, kv in units:
        kk = keys[sub * Mq:sub * Mq + ATT_W, kv * HEAD_DIM_A:(kv + 1) * HEAD_DIM_A].astype(BF16)
        zero = jnp.zeros_like(kk)
        k2 = jnp.concatenate([jnp.concatenate([kk, zero], axis=1),
                              jnp.concatenate([zero, kk], axis=1)], axis=0)
        qt = jnp.concatenate([qt_ref[0, (kv * npair + p) * LANES:(kv * npair + p + 1) * LANES, sub * Mq:(sub + 1) * Mq]
                              for p in range(npair)], axis=1)
        sts[sub, kv] = jnp.dot(k2, qt, preferred_element_type=F32)
    probs, sink_terms = {}, {}
    for sub, kv in units:
        st = sts[sub, kv] + (bias0_ref[0, kv] if sub == 0 else bias1_ref[0, kv])
        for half in range(2):
            sh = st[half * ATT_W:(half + 1) * ATT_W]
            sink = jnp.zeros((1, npair * Mq), F32)
            for p in range(npair):
                sink = jnp.where(pair_of_lane == p, sinks_ref[kv * GROUP_A + 2 * p + half], sink)
            mx = jnp.maximum(jnp.max(sh, axis=0, keepdims=True), sink)
            probs[sub, kv, half] = jnp.exp(sh - mx).astype(BF16)
            sink_terms[sub, kv, half] = jnp.exp(sink - mx)
    ots = {}
    for sub, kv in units:
        vt = vals_t[kv * HEAD_DIM_A:(kv + 1) * HEAD_DIM_A, sub * Mq:sub * Mq + ATT_W]
        for half in range(2):
            lhs = jnp.concatenate([vt, ones] if half == 0 else [ones, vt], axis=0)
            ots[sub, kv, half] = jnp.dot(lhs, probs[sub, kv, half], preferred_element_type=F32)
    for sub, kv in units:
        oe, oo = ots[sub, kv, 0], ots[sub, kv, 1]
        ye = oe[:HEAD_DIM_A] / (oe[HEAD_DIM_A:] + sink_terms[sub, kv, 0])
        yo = oo[HEAD_DIM_A:] / (oo[:HEAD_DIM_A] + sink_terms[sub, kv, 1])
        yt = jnp.concatenate([ye, yo], axis=0)
        for p in range(npair):
            col = (kv * npair + p) * LANES
            o_ref[0, sub * Mq:(sub + 1) * Mq, col:col + LANES] = yt[:, p * Mq:(p + 1) * Mq].T.astype(BF16)


def _attn_prompt_call(qa_t, ka, va, sinks):
    N, _, T = qa_t.shape
    rows = ATT_QB * ATT_SUB
    rel_q = ATT_QB + np.arange(ATT_QB)
    rel_k = np.arange(ATT_W)
    bias = np.stack([_attn_bias(rel_q, rel_k, rel_k >= ATT_QB, transposed=True),
                     _attn_bias(rel_q, rel_k, rel_k >= 0, transposed=True)])
    kv_prev = pl.BlockSpec((1, ATT_QB, KV_A_W), lambda n, j: (n, jnp.maximum(ATT_SUB * j - 1, 0), 0))
    kv_cur = pl.BlockSpec((1, rows, KV_A_W), lambda n, j: (n, j, 0))
    return pl.pallas_call(
        _attn_prompt_kernel,
        grid=(N, T // rows),
        in_specs=[pl.BlockSpec(memory_space=pltpu.SMEM),
                  pl.BlockSpec((1, Q_A, rows), lambda n, j: (n, 0, j)),
                  kv_prev, kv_cur, kv_prev, kv_cur,
                  pl.BlockSpec((1,) + bias.shape[1:], lambda n, j: (jnp.minimum(j, 1), 0, 0, 0)),
                  pl.BlockSpec((1,) + bias.shape[1:], lambda n, j: (1, 0, 0, 0), pipeline_mode=pl.Buffered(1))],
        out_specs=pl.BlockSpec((1, rows, Q_A), lambda n, j: (n, j, 0)),
        out_shape=jax.ShapeDtypeStruct((N, T, Q_A), BF16),
        compiler_params=pltpu.CompilerParams(dimension_semantics=("parallel", "parallel")),
        name="attn_prompt",
    )(sinks, qa_t, ka, ka, va, va, jnp.asarray(bias), jnp.asarray(bias))


def _attn_sample_kernel(sinks_ref, q_ref, kn_ref, vn_ref, kc_ref, vc_ref, bias_ref,
                        o_ref, ko_ref, vo_ref, *, T):
    pad = jnp.zeros((ATT_W - KV_WIN - T, KV_A_W), F32)
    kwin = jnp.concatenate([kc_ref[0], kn_ref[0], pad], axis=0)
    vwin = jnp.concatenate([vc_ref[0], vn_ref[0], pad], axis=0)
    ko_ref[0] = kwin[T:T + KV_WIN]
    vo_ref[0] = vwin[T:T + KV_WIN]

    def store(col, val):
        o_ref[0, :, col:col + LANES] = val

    _attn_core(q_ref[0], kwin, vwin, lambda kv: bias_ref[kv], sinks_ref, store)


def _attn_sample_call(qa, ka, va, cache_k, cache_v, sinks):
    N, T, _ = qa.shape
    qpos = PAST_LEN + np.arange(T)
    kpos = PAST_LEN - KV_WIN + np.arange(ATT_W)
    bias = _attn_bias(qpos, kpos, np.arange(ATT_W) < KV_WIN + T)
    new = pl.BlockSpec((1, T, KV_A_W), lambda n: (n, 0, 0))
    cache = pl.BlockSpec((1, KV_WIN, KV_A_W), lambda n: (n, 0, 0))
    cache_shape = jax.ShapeDtypeStruct((N, KV_WIN, KV_A_W), F32)
    return pl.pallas_call(
        functools.partial(_attn_sample_kernel, T=T),
        grid=(N,),
        in_specs=[pl.BlockSpec(memory_space=pltpu.SMEM),
                  pl.BlockSpec((1, T, Q_A), lambda n: (n, 0, 0)),
                  new, new, cache, cache,
                  _const_spec(bias.shape)],
        out_specs=[pl.BlockSpec((1, T, Q_A), lambda n: (n, 0, 0)), cache, cache],
        out_shape=[jax.ShapeDtypeStruct((N, T, Q_A), BF16), cache_shape, cache_shape],
        compiler_params=pltpu.CompilerParams(dimension_semantics=("parallel",)),
        name="attn_sample",
    )(sinks, qa, ka, va, cache_k, cache_v, jnp.asarray(bias))


def _mlstm_kernel(q_ref, kt_ref, v_ref, og_ref, ar_ref, bc_ref, ac_ref, nw_ref, c0_ref, n0_ref, m0_ref,
                  y_ref, c_ref, n_ref, m_ref, ct_ref, *, SB, L):
    j = pl.program_id(1)
    heads = [(sb, hd) for sb in range(SB) for hd in range(N_HEADS_B)]

    @pl.when(j == 0)
    def _():
        for sb, hd in heads:
            ct_ref[sb, hd, :, :DV_B] = c0_ref[sb, hd].T
            ct_ref[sb, hd, :, DV_B:] = jnp.broadcast_to(n0_ref[sb, hd], (DQK_B, LANES))
        m_ref[...] = m0_ref[...]

    causal = (lax.broadcasted_iota(jnp.int32, (L, L), 0) >= lax.broadcasted_iota(jnp.int32, (L, L), 1))
    ones = jnp.ones((L, LANES), BF16)
    gates = []
    for sb in range(SB):
        a4 = ac_ref[sb]
        b4 = bc_ref[sb]
        m_row = m_ref[sb]
        a_last = a4[L - 1:L]
        b_last = b4[L - 1:L]
        mx = jnp.maximum(a_last, m_row)
        m_new = b_last + mx
        gates.append(dict(a4=a4, b4=b4, a_last=a_last, m_row=m_row, decay4=jnp.exp(b_last + m_row - m_new),
                          rho4=jnp.exp(a_last - mx)))
        m_ref[sb] = m_new

    def lanes(x):
        return jnp.broadcast_to(x, (x.shape[0], LANES))

    def col(x, hd):
        return x[:, hd:hd + 1]

    qs = {h: q_ref[h[0], :, h[1] * DQK_B:(h[1] + 1) * DQK_B] for h in heads}
    kts = {h: kt_ref[h[0], h[1] * DQK_B:(h[1] + 1) * DQK_B, :] for h in heads}
    cts = {h: ct_ref[h[0], h[1]] for h in heads}
    qk = {h: jnp.dot(qs[h], kts[h], preferred_element_type=F32) for h in heads}
    qc = {h: jnp.dot(qs[h], cts[h].astype(BF16), preferred_element_type=F32) for h in heads}
    lhs, a_bs = {}, {}
    for h in heads:
        sb, hd = h
        gt = gates[sb]
        a_row = ar_ref[sb, hd:hd + 1, :]
        a_bs[h] = lanes(col(gt["a4"], hd))
        a_full = jnp.concatenate([a_bs[h]] * (L // LANES), axis=1) if L >= LANES else a_bs[h][:, :L]
        w = jnp.exp(jnp.where(causal, a_row - a_full, -jnp.inf))
        s = (qk[h] * w).astype(BF16)
        kw = (kts[h].astype(F32) * jnp.exp(a_row - col(gt["a_last"], hd))).astype(BF16)
        lhs[h] = jnp.concatenate([s, kw], axis=0)
    pu = {}
    for h in heads:
        sb, hd = h
        v_aug = jnp.concatenate([v_ref[sb, :, hd * DV_B:(hd + 1) * DV_B], ones], axis=1)
        pu[h] = jnp.dot(lhs[h], v_aug, preferred_element_type=F32)
    ones_sum = jnp.ones((DV_B, LANES), BF16)
    for h in heads:
        sb, hd = h
        gt = gates[sb]
        a_b = a_bs[h]
        m_b = lanes(col(gt["m_row"], hd))
        big = jnp.maximum(a_b, m_b)
        r_b = jnp.exp(a_b - big)
        g_b = jnp.exp(m_b - big)
        e_b = jnp.exp(-(lanes(col(gt["b4"], hd)) + big))
        p, c = pu[h], qc[h]
        den = r_b * p[:L, DV_B:] + g_b * c[:, DV_B:]
        inv = 1.0 / jnp.maximum(jnp.abs(den), e_b)
        hh = jnp.concatenate([(r_b * p[:L, k * LANES:(k + 1) * LANES] + g_b * c[:, k * LANES:(k + 1) * LANES]) * inv
                              for k in range(DV_B // LANES)], axis=1)
        mu = jnp.dot(hh.astype(BF16), ones_sum, preferred_element_type=F32) * (1.0 / DV_B)
        hc = hh - jnp.concatenate([mu] * (DV_B // LANES), axis=1)
        var = jnp.dot(jnp.square(hc).astype(BF16), ones_sum, preferred_element_type=F32) * (1.0 / DV_B)
        rstd = lax.rsqrt(var + LN_EPS)
        yn = hc * jnp.concatenate([rstd] * (DV_B // LANES), axis=1) * nw_ref[:, hd * DV_B:(hd + 1) * DV_B]
        og = og_ref[sb, :, hd * DV_B:(hd + 1) * DV_B]
        y_ref[sb, :, hd * DV_B:(hd + 1) * DV_B] = (jax.nn.sigmoid(og) * yn).astype(BF16)
        ct_ref[sb, hd] = col(gt["decay4"], hd) * cts[h] + col(gt["rho4"], hd) * pu[h][L:]

    @pl.when(j == pl.num_programs(1) - 1)
    def _():
        for sb, hd in heads:
            c_ref[sb, hd] = ct_ref[sb, hd, :, :DV_B].T
            n_ref[sb, hd] = ct_ref[sb, hd, :, DV_B:DV_B + 1]


def _mlstm_call(qm, kt, vm, og, a_row, b_col, a_col, norm_w, c0, n0, m0, *, SB, L):
    N, T, _ = qm.shape

    def tok(c):
        return pl.BlockSpec((SB, L, c), lambda i, j: (i, j, 0))

    def tr(c):
        return pl.BlockSpec((SB, c, L), lambda i, j: (i, 0, j))

    st_c = pl.BlockSpec((SB, N_HEADS_B, DV_B, DQK_B), lambda i, j: (i, 0, 0, 0))
    st_n = pl.BlockSpec((SB, N_HEADS_B, DQK_B, 1), lambda i, j: (i, 0, 0, 0))
    st_m = pl.BlockSpec((SB, 1, LANES), lambda i, j: (i, 0, 0))
    return pl.pallas_call(
        functools.partial(_mlstm_kernel, SB=SB, L=L),
        grid=(N // SB, T // L),
        in_specs=[tok(QK_B), tr(QK_B), tok(V_B), tok(V_B), tr(8), tok(LANES), tok(LANES),
                  _const_spec((1, V_B)), st_c, st_n, st_m],
        out_specs=[tok(V_B), st_c, st_n, st_m],
        out_shape=[jax.ShapeDtypeStruct((N, T, V_B), BF16),
                   jax.ShapeDtypeStruct((N, N_HEADS_B, DV_B, DQK_B), F32),
                   jax.ShapeDtypeStruct((N, N_HEADS_B, DQK_B, 1), F32),
                   jax.ShapeDtypeStruct((N, 1, LANES), F32)],
        scratch_shapes=[pltpu.VMEM((SB, N_HEADS_B, DQK_B, DV_B + LANES), F32)],
        compiler_params=pltpu.CompilerParams(dimension_semantics=("parallel", "arbitrary")),
        name="mlstm",
    )(qm, kt, vm, og, a_row, b_col, a_col, norm_w, c0, n0, m0)


def _post_kernel(x_ref, mod_ref, ya_ref, yb_ref, ga_ref, gb_ref, cin_ref,
                 wpa_ref, wpb_ref, wo_ref, l1w_ref, l1b_ref, wup_ref, bup_ref, cw_ref, cb_ref,
                 wdn_ref, l2w_ref, l2b_ref,
                 y_ref, cout_ref, ctx_ref, act_ref, *, S, R):
    SR = S * R
    FC = D_FF // FFN_CHUNKS

    @pl.when(pl.program_id(1) == 0)
    def _():
        cout_ref[...] = cin_ref[...]

    g1 = mod_ref[:, 2:3, :]
    sh2 = mod_ref[:, 3:4, :]
    sc2 = mod_ref[:, 4:5, :]
    g2 = mod_ref[:, 5:6, :]
    x = x_ref[...]

    pa = jnp.dot(ya_ref[...].reshape(SR, Q_A), wpa_ref[...], preferred_element_type=F32)
    pb = jnp.dot(yb_ref[...].reshape(SR, V_B), wpb_ref[...], preferred_element_type=F32)
    merged = (jax.nn.sigmoid(ga_ref[...].reshape(SR, D_MODEL)) * pa
              + jax.nn.sigmoid(gb_ref[...].reshape(SR, D_MODEL)) * pb)
    mo = _bdot(merged, wo_ref[...]).reshape(S, R, D_MODEL)
    x1 = _layer_norm(ALPHA * x + g1 * mo, l1w_ref[...], l1b_ref[...])
    h2 = (x1 * (1.0 + sc2) + sh2).reshape(SR, D_MODEL).astype(BF16)

    def conv(lo):
        u = (jnp.dot(h2, wup_ref[:, lo:lo + FC], preferred_element_type=F32)
             + bup_ref[:, lo:lo + FC]).reshape(S, R, FC)
        ctx_ref[:, 6:8, :] = cout_ref[:, :, lo:lo + FC]
        ctx_ref[:, 8:, :] = u
        cout_ref[:, :, lo:lo + FC] = ctx_ref[:, R + 6:R + 8, :]
        return (cb_ref[:, lo:lo + FC]
                + cw_ref[0:1, lo:lo + FC] * ctx_ref[:, 6:R + 6, :]
                + cw_ref[1:2, lo:lo + FC] * ctx_ref[:, 7:R + 7, :]
                + cw_ref[2:3, lo:lo + FC] * u)

    for ci in range(FFN_CHUNKS):
        lo = ci * FC
        a = conv(lo)
        g = conv(D_FF + lo)
        act_ref[:, lo:lo + FC] = (a * jax.nn.gelu(g, approximate=True)).reshape(SR, FC).astype(BF16)

    f = jnp.dot(act_ref[...], wdn_ref[...], preferred_element_type=F32).reshape(S, R, D_MODEL)
    y_ref[...] = _layer_norm(ALPHA * x1 + g2 * f, l2w_ref[...], l2b_ref[...])


def _post_call(x, mod, ya, yb, ga, gb, conv_in, wpa, wpb, wo, l1w, l1b, wup, bup, cw, cb, wdn, l2w, l2b,
               *, S, R):
    N, T, _ = x.shape
    nsb, nrb = N // S, T // R
    FC = D_FF // FFN_CHUNKS

    def tok(c):
        return pl.BlockSpec((S, R, c), lambda i, j: (i, j, 0))

    conv_spec = pl.BlockSpec((S, CONV_W - 1, 2 * D_FF), lambda i, j: (i, 0, 0))
    return pl.pallas_call(
        functools.partial(_post_kernel, S=S, R=R),
        grid=(nsb, nrb),
        in_specs=[tok(D_MODEL), pl.BlockSpec((S, 6, D_MODEL), lambda i, j: (i, 0, 0)),
                  tok(Q_A), tok(V_B), tok(D_MODEL), tok(D_MODEL), conv_spec,
                  _const_spec((Q_A, D_MODEL)), _const_spec((V_B, D_MODEL)), _const_spec((D_MODEL, D_MODEL)),
                  _const_spec((1, D_MODEL)), _const_spec((1, D_MODEL)),
                  _const_spec((D_MODEL, 2 * D_FF)), _const_spec((1, 2 * D_FF)),
                  _const_spec((CONV_W, 2 * D_FF)), _const_spec((1, 2 * D_FF)),
                  _const_spec((D_FF, D_MODEL)), _const_spec((1, D_MODEL)), _const_spec((1, D_MODEL))],
        out_specs=[tok(D_MODEL), conv_spec],
        out_shape=[jax.ShapeDtypeStruct((N, T, D_MODEL), F32),
                   jax.ShapeDtypeStruct((N, CONV_W - 1, 2 * D_FF), F32)],
        scratch_shapes=[pltpu.VMEM((S, R + 8, FC), F32), pltpu.VMEM((S * R, D_FF), BF16)],
        compiler_params=pltpu.CompilerParams(dimension_semantics=("parallel", "arbitrary")),
        name="post",
    )(x, mod, ya, yb, ga, gb, conv_in, wpa, wpb, wo, l1w, l1b, wup, bup, cw, cb, wdn, l2w, l2b)


def _pad_rows(a, rows):
    return jnp.pad(a, ((0, rows - a.shape[0]),) + ((0, 0),) * (a.ndim - 1))


def kernel(x_prompt, x_sample, cache_attn_k, cache_attn_v, state_mlstm_C, state_mlstm_n, state_mlstm_m, state_ffn_conv, c_prompt, c_sample, w_ada, b_ada, w_in, b_igate, b_fgate, attn_sinks, mlstm_norm_w, w_proj_a, w_proj_b, w_out, ln1_w, ln1_b, w_up, b_up, conv_w, conv_b, w_down, ln2_w, ln2_b):
    Bp, Tp, D = x_prompt.shape
    Bs, Ts, _ = x_sample.shape
    l = 0

    mod = _mod_call(jnp.concatenate([c_prompt, c_sample], axis=0), w_ada[l], b_ada[l][None])
    mod_p = mod[:Bp].reshape(Bp, 6, D)
    mod_s = mod[Bp:].reshape(Bs, 6, D)

    w = w_in[l]
    o_km, o_vm, o_i, o_f, o_o = Z_OFF[4], Z_OFF[5], Z_OFF[6], Z_OFF[7], Z_OFF[8]
    lane_pad = ((0, 0), (0, LANES - N_HEADS_B))
    w_main = jnp.concatenate([w[:, :o_km], w[:, o_vm:o_i], w[:, o_o:],
                              jnp.pad(w[:, o_i:o_f], lane_pad), jnp.pad(w[:, o_f:o_o], lane_pad)],
                             axis=1).astype(BF16)
    w_qt = w[:, :Z_OFF[1]].T.astype(BF16)
    w_kt = w[:, o_km:o_vm].T.astype(BF16)
    w_i = _pad_rows(w[:, o_i:o_f].T, 8).astype(BF16)
    w_f = _pad_rows(w[:, o_f:o_o].T, 8).astype(BF16)
    b_i = _pad_rows(b_igate[l][:, None], 8)
    b_f = _pad_rows(b_fgate[l][:, None], 8)
    b_ic = jnp.pad(b_igate[l][None], lane_pad)
    b_fc = jnp.pad(b_fgate[l][None], lane_pad)
    inproj_w = (w_main, w_qt, w_kt, w_i, w_f, b_i, b_f, b_ic, b_fc)
    sinks = attn_sinks[l]
    norm_w = mlstm_norm_w[l][None]
    post_w = (w_proj_a[l].astype(BF16), w_proj_b[l].astype(BF16), w_out[l].astype(BF16),
              ln1_w[l][None], ln1_b[l][None], w_up[l].astype(BF16), b_up[l][None], conv_w[l], conv_b[l][None],
              w_down[l].astype(BF16), ln2_w[l][None], ln2_b[l][None])

    qa, ka, va, qm, kt, vm, og, ga, gb, a_row, b_col, a_col = _inproj_call(
        x_prompt, mod_p, *inproj_w, S=1, R=INPROJ_ROWS, L=MLSTM_CHUNK, q_transposed=True)
    ya = _attn_prompt_call(qa, ka, va, sinks)
    yb, p_c, p_n, p_m = _mlstm_call(
        qm, kt, vm, og, a_row, b_col, a_col, norm_w,
        jnp.zeros((Bp, N_HEADS_B, DV_B, DQK_B), F32), jnp.zeros((Bp, N_HEADS_B, DQK_B, 1), F32),
        jnp.zeros((Bp, 1, LANES), F32), SB=2, L=MLSTM_CHUNK)
    p_m = p_m[:, 0, :N_HEADS_B]
    y_prompt, p_conv = _post_call(x_prompt, mod_p, ya, yb, ga, gb,
                                  jnp.zeros((Bp, CONV_W - 1, 2 * D_FF), F32), *post_w, S=1, R=256)
    p_k = ka[:, Tp - KV_WIN:].reshape(Bp, KV_WIN, N_KV_A, HEAD_DIM_A)
    p_v = va[:, Tp - KV_WIN:].reshape(Bp, KV_WIN, N_KV_A, HEAD_DIM_A)

    qa, ka, va, qm, kt, vm, og, ga, gb, a_row, b_col, a_col = _inproj_call(
        x_sample, mod_s, *inproj_w, S=8, R=Ts, L=Ts, q_transposed=False)
    ya, s_k, s_v = _attn_sample_call(qa, ka, va, cache_attn_k[l].reshape(Bs, KV_WIN, KV_A_W),
                                     cache_attn_v[l].reshape(Bs, KV_WIN, KV_A_W), sinks)
    yb, s_c, s_n, s_m = _mlstm_call(
        qm, kt, vm, og, a_row, b_col, a_col, norm_w,
        state_mlstm_C[l], state_mlstm_n[l][:, :, :, None], jnp.pad(state_mlstm_m[l][:, None, :], ((0, 0),) + lane_pad),
        SB=4, L=Ts)
    s_m = s_m[:, 0, :N_HEADS_B]
    y_sample, s_conv = _post_call(x_sample, mod_s, ya, yb, ga, gb, state_ffn_conv[l], *post_w, S=8, R=Ts)

    def st(a, shape):
        return a.reshape((1,) + shape)

    return (y_prompt, y_sample,
            st(p_k, (Bp, KV_WIN, N_KV_A, HEAD_DIM_A)), st(p_v, (Bp, KV_WIN, N_KV_A, HEAD_DIM_A)),
            st(p_c, (Bp, N_HEADS_B, DV_B, DQK_B)), st(p_n, (Bp, N_HEADS_B, DQK_B)), st(p_m, (Bp, N_HEADS_B)),
            st(p_conv, (Bp, CONV_W - 1, 2 * D_FF)),
            st(s_k, (Bs, KV_WIN, N_KV_A, HEAD_DIM_A)), st(s_v, (Bs, KV_WIN, N_KV_A, HEAD_DIM_A)),
            st(s_c, (Bs, N_HEADS_B, DV_B, DQK_B)), st(s_n, (Bs, N_HEADS_B, DQK_B)), st(s_m, (Bs, N_HEADS_B)),
            st(s_conv, (Bs, CONV_W - 1, 2 * D_FF)))
```

```python
import functools

import numpy as np
import jax
import jax.numpy as jnp
from jax import lax
from jax.experimental import pallas as pl
from jax.experimental.pallas import tpu as pltpu

F32 = jnp.float32
BF16 = jnp.bfloat16

D_MODEL = 1024
DEPTH = 1
PAST_LEN = 2048
CHUNK = 64
N_HEADS_A = 16
N_KV_A = 2
HEAD_DIM_A = 64
GROUP_A = N_HEADS_A // N_KV_A
WINDOW = 128
WINDOW_CHUNKS = WINDOW // CHUNK
KV_WIN = min(WINDOW, PAST_LEN)
N_HEADS_B = 4
DQK_B = 128
DV_B = 256
D_FF = 2816
CONV_W = 3
LN_EPS = 1e-5
ALPHA = (2 * DEPTH) ** 0.25
Q_A = N_HEADS_A * HEAD_DIM_A
KV_A_W = N_KV_A * HEAD_DIM_A
QK_B = N_HEADS_B * DQK_B
V_B = N_HEADS_B * DV_B
Z_PARTS = (Q_A, KV_A_W, KV_A_W, QK_B, QK_B, V_B, N_HEADS_B, N_HEADS_B, V_B, D_MODEL, D_MODEL)
Z_OFF = tuple(int(v) for v in np.cumsum((0,) + Z_PARTS))

LANES = 128

M_QA = 0
M_KA = M_QA + Q_A
M_VA = M_KA + KV_A_W
M_QM = M_VA + KV_A_W
M_VM = M_QM + QK_B
M_OG = M_VM + V_B
M_GA = M_OG + V_B
M_GB = M_GA + D_MODEL
M_GI = M_GB + D_MODEL
M_GF = M_GI + LANES
M_END = M_GF + LANES
ATT_QB = 128
ATT_SUB = 8
ATT_W = 256
FFN_CHUNKS = 11
MLSTM_CHUNK = 256
INPROJ_ROWS = 512


def _bdot(a, b):
    return jnp.dot(a.astype(BF16), b.astype(BF16), preferred_element_type=F32)


def _bdot_nt(a, b):
    return lax.dot_general(a.astype(BF16), b.astype(BF16), (((1,), (1,)), ((), ())),
                           preferred_element_type=F32)


def _const_spec(shape):
    nd = len(shape)
    return pl.BlockSpec(shape, lambda *_: (0,) * nd, pipeline_mode=pl.Buffered(1))


def _layer_norm(x, w, b):
    mu = jnp.mean(x, -1, keepdims=True)
    xc = x - mu
    var = jnp.mean(jnp.square(xc), -1, keepdims=True)
    return xc * lax.rsqrt(var + LN_EPS) * w + b


def _mod_kernel(c_ref, w_ref, b_ref, o_ref):
    c = c_ref[...]
    s = c * jax.nn.sigmoid(c)
    o_ref[...] = _bdot(s, w_ref[...]) + b_ref[...]


def _mod_call(c_all, w_ada, b_ada):
    n = c_all.shape[0]
    tn = 1536
    return pl.pallas_call(
        _mod_kernel,
        grid=(6 * D_MODEL // tn,),
        in_specs=[pl.BlockSpec((n, D_MODEL), lambda j: (0, 0)),
                  pl.BlockSpec((D_MODEL, tn), lambda j: (0, j)),
                  pl.BlockSpec((1, tn), lambda j: (0, j))],
        out_specs=pl.BlockSpec((n, tn), lambda j: (0, j)),
        out_shape=jax.ShapeDtypeStruct((n, 6 * D_MODEL), F32),
        name="mod",
    )(c_all, w_ada, b_ada)


def _log_sigmoid(x):
    return jnp.minimum(x, 0.0) - jnp.log1p(jnp.exp(-jnp.abs(x)))


def _seg_scan(x, L, axis, op, ident):
    pos = lax.broadcasted_iota(jnp.int32, x.shape, axis) % L
    step = 1
    while step < L:
        x = op(x, jnp.where(pos >= step, pltpu.roll(x, step, axis=axis), ident))
        step *= 2
    return x


def _inproj_kernel(x_ref, mod_ref, w_ref, wq_ref, wk_ref, wi_ref, wf_ref, bi_ref, bf_ref, bic_ref, bfc_ref,
                   qa_ref, ka_ref, va_ref, qm_ref, kt_ref, vm_ref, og_ref, ga_ref, gb_ref,
                   ar_ref, bc_ref, ac_ref, *, S, R, L, q_transposed):
    SR = S * R
    sh = mod_ref[:, 0:1, :]
    sc = mod_ref[:, 1:2, :]
    h = (x_ref[...] * (1.0 + sc) + sh).reshape(SR, D_MODEL).astype(BF16)

    def proj(lo, hi):
        return jnp.dot(h, w_ref[:, lo:hi], preferred_element_type=F32)

    if q_transposed:
        qt = (_bdot_nt(wq_ref[...], h) * (HEAD_DIM_A ** -0.5)).astype(BF16)
        for s in range(S):
            qa_ref[s] = qt[:, s * R:(s + 1) * R]
    else:
        qa_ref[...] = (proj(M_QA, M_KA) * (HEAD_DIM_A ** -0.5)).astype(BF16).reshape(S, R, Q_A)
    kv = proj(M_KA, M_QM)
    ka_ref[...] = kv[:, :KV_A_W].reshape(S, R, KV_A_W)
    va_ref[...] = kv[:, KV_A_W:].reshape(S, R, KV_A_W)
    qm_ref[...] = proj(M_QM, M_VM).astype(BF16).reshape(S, R, QK_B)
    vm_ref[...] = proj(M_VM, M_OG).astype(BF16).reshape(S, R, V_B)
    og_ref[...] = proj(M_OG, M_GA).reshape(S, R, V_B)
    ga_ref[...] = proj(M_GA, M_GB).reshape(S, R, D_MODEL)
    gb_ref[...] = proj(M_GB, M_GI).reshape(S, R, D_MODEL)

    kt = (_bdot_nt(wk_ref[...], h) * (DQK_B ** -0.5)).astype(BF16)
    i_row = _bdot_nt(wi_ref[...], h) + bi_ref[...]
    logf_row = _log_sigmoid(_bdot_nt(wf_ref[...], h) + bf_ref[...])
    a_row = i_row - _seg_scan(logf_row, L, 1, jnp.add, 0.0)
    for s in range(S):
        kt_ref[s] = kt[:, s * R:(s + 1) * R]
        ar_ref[s] = a_row[:, s * R:(s + 1) * R]
    gates = proj(M_GI, M_END)
    i_col = gates[:, :LANES] + bic_ref[...]
    logf_col = _log_sigmoid(gates[:, LANES:] + bfc_ref[...])
    b_col = _seg_scan(logf_col, L, 0, jnp.add, 0.0)
    a_col = _seg_scan(i_col - b_col, L, 0, jnp.maximum, -jnp.inf)
    bc_ref[...] = b_col.reshape(S, R, LANES)
    ac_ref[...] = a_col.reshape(S, R, LANES)


def _inproj_call(x, mod, w_main, w_qt, w_kt, w_i, w_f, b_i, b_f, b_ic, b_fc, *, S, R, L, q_transposed):
    N, T, _ = x.shape
    nsb, nrb = N // S, T // R

    def tok(c):
        return pl.BlockSpec((S, R, c), lambda i, j: (i, j, 0))

    def tok_shape(c, dt):
        return jax.ShapeDtypeStruct((N, T, c), dt)

    def tr(c):
        return pl.BlockSpec((S, c, R), lambda i, j: (i, 0, j))

    return pl.pallas_call(
        functools.partial(_inproj_kernel, S=S, R=R, L=L, q_transposed=q_transposed),
        grid=(nsb, nrb),
        in_specs=[tok(D_MODEL),
                  pl.BlockSpec((S, 6, D_MODEL), lambda i, j: (i, 0, 0)),
                  _const_spec((D_MODEL, M_END)), _const_spec((Q_A, D_MODEL)), _const_spec((QK_B, D_MODEL)),
                  _const_spec((8, D_MODEL)), _const_spec((8, D_MODEL)),
                  _const_spec((8, 1)), _const_spec((8, 1)),
                  _const_spec((1, LANES)), _const_spec((1, LANES))],
        out_specs=[tr(Q_A) if q_transposed else tok(Q_A), tok(KV_A_W), tok(KV_A_W), tok(QK_B), tr(QK_B), tok(V_B),
                   tok(V_B), tok(D_MODEL), tok(D_MODEL), tr(8), tok(LANES), tok(LANES)],
        out_shape=[jax.ShapeDtypeStruct((N, Q_A, T), BF16) if q_transposed else tok_shape(Q_A, BF16),
                   tok_shape(KV_A_W, F32), tok_shape(KV_A_W, F32),
                   tok_shape(QK_B, BF16), jax.ShapeDtypeStruct((N, QK_B, T), BF16), tok_shape(V_B, BF16),
                   tok_shape(V_B, F32), tok_shape(D_MODEL, F32), tok_shape(D_MODEL, F32),
                   jax.ShapeDtypeStruct((N, 8, T), F32), tok_shape(LANES, F32), tok_shape(LANES, F32)],
        compiler_params=pltpu.CompilerParams(dimension_semantics=("parallel", "parallel")),
        name="inproj",
    )(x, mod, w_main, w_qt, w_kt, w_i, w_f, b_i, b_f, b_ic, b_fc)


def _alibi_slopes():
    return 2.0 ** (-8.0 * np.arange(1, N_HEADS_A + 1, dtype=np.float64) / N_HEADS_A)


def _attn_bias(qpos, kpos, kvalid, transposed=False):
    qpos = np.asarray(qpos)[:, None]
    kpos = np.asarray(kpos)[None, :]
    qc, kc = qpos // CHUNK, kpos // CHUNK
    visible = (kpos >= 0) & (kc <= qc) & (kc >= qc - WINDOW_CHUNKS) & np.asarray(kvalid)[None, :]
    dist = np.abs(qpos - kpos).astype(np.float64)
    slopes = _alibi_slopes()
    out = np.zeros((N_KV_A, GROUP_A // 2, qpos.shape[0], 2, kpos.shape[1]), np.float32)
    for kv in range(N_KV_A):
        for p in range(GROUP_A // 2):
            for half in range(2):
                hd = kv * GROUP_A + 2 * p + half
                out[kv, p, :, half, :] = np.where(visible, -slopes[hd] * dist, -np.inf)
    if transposed:
        return out.transpose(0, 3, 4, 1, 2).reshape(N_KV_A, 2 * kpos.shape[1], (GROUP_A // 2) * qpos.shape[0])
    return out.reshape(N_KV_A, (GROUP_A // 2) * qpos.shape[0], 2 * kpos.shape[1])


def _attn_core(q, kwin, vwin, bias_of, sinks_ref, store):
    Mq = q.shape[0]
    W = kwin.shape[0]
    npair = GROUP_A // 2
    lane = lax.broadcasted_iota(jnp.int32, (npair * Mq, LANES), 1)
    for kv in range(N_KV_A):
        kk = kwin[:, kv * HEAD_DIM_A:(kv + 1) * HEAD_DIM_A].astype(BF16)
        vv = vwin[:, kv * HEAD_DIM_A:(kv + 1) * HEAD_DIM_A].astype(BF16)
        zero = jnp.zeros_like(kk)
        one = jnp.ones_like(vv)
        k2 = jnp.concatenate([jnp.concatenate([kk, zero], axis=1),
                              jnp.concatenate([zero, kk], axis=1)], axis=0)
        v_aug = jnp.concatenate([vv, one, one, vv], axis=1)
        qs = jnp.concatenate([q[:, (kv * npair + p) * LANES:(kv * npair + p + 1) * LANES]
                              for p in range(npair)], axis=0)
        s = _bdot_nt(qs, k2) + bias_of(kv)
        probs, sink_terms = [], []
        for half in range(2):
            sh = s[:, half * W:(half + 1) * W]
            sink = jnp.concatenate(
                [jnp.full((Mq, 1), sinks_ref[kv * GROUP_A + 2 * p + half], F32) for p in range(npair)], axis=0)
            mx = jnp.maximum(jnp.max(sh, axis=1, keepdims=True), sink)
            probs.append(jnp.exp(sh - mx).astype(BF16))
            sink_terms.append(jnp.exp(sink - mx))
        o = jnp.dot(jnp.concatenate(probs, axis=0), v_aug, preferred_element_type=F32)
        oe, oo = o[:npair * Mq], o[npair * Mq:]
        ye = oe[:, :LANES] / (oe[:, LANES:] + sink_terms[0])
        yo = oo[:, LANES:] / (oo[:, :LANES] + sink_terms[1])
        y = jnp.where(lane < HEAD_DIM_A, ye, yo).astype(BF16)
        for p in range(npair):
            store((kv * npair + p) * LANES, y[p * Mq:(p + 1) * Mq])


def _attn_prompt_kernel(sinks_ref, qt_ref, kp_ref, kc_ref, vp_ref, vc_ref, bias0_ref, bias1_ref, o_ref):
    Mq = ATT_QB
    npair = GROUP_A // 2
    keys = jnp.concatenate([kp_ref[0], kc_ref[0]], axis=0)
    vals_t = jnp.concatenate([vp_ref[0], vc_ref[0]], axis=0).T.astype(BF16)
    ones = jnp.ones((HEAD_DIM_A, ATT_W), BF16)
    pair_of_lane = lax.broadcasted_iota(jnp.int32, (1, npair * Mq), 1) // Mq
    units = [(sub, kv) for sub in range(ATT_SUB) for kv in range(N_KV_A)]
    sts = {}
    for sub, kv in units:
        kk = keys[sub * Mq:sub * Mq + ATT_W, kv * HEAD_DIM_A:(kv + 1) * HEAD_DIM_A].astype(BF16)
        zero = jnp.zeros_like(kk)
        k2 = jnp.concatenate([jnp.concatenate([kk, zero], axis=1),
                              jnp.concatenate([zero, kk], axis=1)], axis=0)
        qt = jnp.concatenate([qt_ref[0, (kv * npair + p) * LANES:(kv * npair + p + 1) * LANES, sub * Mq:(sub + 1) * Mq]
                              for p in range(npair)], axis=1)
        sts[sub, kv] = jnp.dot(k2, qt, preferred_element_type=F32)
    probs, sink_terms = {}, {}
    for sub, kv in units:
        st = sts[sub, kv] + (bias0_ref[0, kv] if sub == 0 else bias1_ref[0, kv])
        for half in range(2):
            sh = st[half * ATT_W:(half + 1) * ATT_W]
            sink = jnp.zeros((1, npair * Mq), F32)
            for p in range(npair):
                sink = jnp.where(pair_of_lane == p, sinks_ref[kv * GROUP_A + 2 * p + half], sink)
            mx = jnp.maximum(jnp.max(sh, axis=0, keepdims=True), sink)
            probs[sub, kv, half] = jnp.exp(sh - mx).astype(BF16)
            sink_terms[sub, kv, half] = jnp.exp(sink - mx)
    ots = {}
    for sub, kv in units:
        vt = vals_t[kv * HEAD_DIM_A:(kv + 1) * HEAD_DIM_A, sub * Mq:sub * Mq + ATT_W]
        for half in range(2):
            lhs = jnp.concatenate([vt, ones] if half == 0 else [ones, vt], axis=0)
            ots[sub, kv, half] = jnp.dot(lhs, probs[sub, kv, half], preferred_element_type=F32)
    for sub, kv in units:
        oe, oo = ots[sub, kv, 0], ots[sub, kv, 1]
        ye = oe[:HEAD_DIM_A] / (oe[HEAD_DIM_A:] + sink_terms[sub, kv, 0])
        yo = oo[HEAD_DIM_A:] / (oo[:HEAD_DIM_A] + sink_terms[sub, kv, 1])
        yt = jnp.concatenate([ye, yo], axis=0)
        for p in range(npair):
            col = (kv * npair + p) * LANES
            o_ref[0, sub * Mq:(sub + 1) * Mq, col:col + LANES] = yt[:, p * Mq:(p + 1) * Mq].T.astype(BF16)


def _attn_prompt_call(qa_t, ka, va, sinks):
    N, _, T = qa_t.shape
    rows = ATT_QB * ATT_SUB
    rel_q = ATT_QB + np.arange(ATT_QB)
    rel_k = np.arange(ATT_W)
    bias = np.stack([_attn_bias(rel_q, rel_k, rel_k >= ATT_QB, transposed=True),
                     _attn_bias(rel_q, rel_k, rel_k >= 0, transposed=True)])
    kv_prev = pl.BlockSpec((1, ATT_QB, KV_A_W), lambda n, j: (n, jnp.maximum(ATT_SUB * j - 1, 0), 0))
    kv_cur = pl.BlockSpec((1, rows, KV_A_W), lambda n, j: (n, j, 0))
    return pl.pallas_call(
        _attn_prompt_kernel,
        grid=(N, T // rows),
        in_specs=[pl.BlockSpec(memory_space=pltpu.SMEM),
                  pl.BlockSpec((1, Q_A, rows), lambda n, j: (n, 0, j)),
                  kv_prev, kv_cur, kv_prev, kv_cur,
                  pl.BlockSpec((1,) + bias.shape[1:], lambda n, j: (jnp.minimum(j, 1), 0, 0, 0)),
                  pl.BlockSpec((1,) + bias.shape[1:], lambda n, j: (1, 0, 0, 0), pipeline_mode=pl.Buffered(1))],
        out_specs=pl.BlockSpec((1, rows, Q_A), lambda n, j: (n, j, 0)),
        out_shape=jax.ShapeDtypeStruct((N, T, Q_A), BF16),
        compiler_params=pltpu.CompilerParams(dimension_semantics=("parallel", "parallel")),
        name="attn_prompt",
    )(sinks, qa_t, ka, ka, va, va, jnp.asarray(bias), jnp.asarray(bias))


def _attn_sample_kernel(sinks_ref, q_ref, kn_ref, vn_ref, kc_ref, vc_ref, bias_ref,
                        o_ref, ko_ref, vo_ref, *, T, NS):
    pad = jnp.zeros((ATT_W - KV_WIN - T, KV_A_W), F32)
    for s in range(NS):
        kwin = jnp.concatenate([kc_ref[s], kn_ref[s], pad], axis=0)
        vwin = jnp.concatenate([vc_ref[s], vn_ref[s], pad], axis=0)
        ko_ref[s] = kwin[T:T + KV_WIN]
        vo_ref[s] = vwin[T:T + KV_WIN]

        def store(col, val, s=s):
            o_ref[s, :, col:col + LANES] = val

        _attn_core(q_ref[s], kwin, vwin, lambda kv: bias_ref[kv], sinks_ref, store)


def _attn_sample_call(qa, ka, va, cache_k, cache_v, sinks, *, NS):
    N, T, _ = qa.shape
    qpos = PAST_LEN + np.arange(T)
    kpos = PAST_LEN - KV_WIN + np.arange(ATT_W)
    bias = _attn_bias(qpos, kpos, np.arange(ATT_W) < KV_WIN + T)
    new = pl.BlockSpec((NS, T, KV_A_W), lambda n: (n, 0, 0))
    cache = pl.BlockSpec((NS, KV_WIN, KV_A_W), lambda n: (n, 0, 0))
    cache_shape = jax.ShapeDtypeStruct((N, KV_WIN, KV_A_W), F32)
    return pl.pallas_call(
        functools.partial(_attn_sample_kernel, T=T, NS=NS),
        grid=(N // NS,),
        in_specs=[pl.BlockSpec(memory_space=pltpu.SMEM),
                  pl.BlockSpec((NS, T, Q_A), lambda n: (n, 0, 0)),
                  new, new, cache, cache,
                  _const_spec(bias.shape)],
        out_specs=[pl.BlockSpec((NS, T, Q_A), lambda n: (n, 0, 0)), cache, cache],
        out_shape=[jax.ShapeDtypeStruct((N, T, Q_A), BF16), cache_shape, cache_shape],
        compiler_params=pltpu.CompilerParams(dimension_semantics=("parallel",)),
        name="attn_sample",
    )(sinks, qa, ka, va, cache_k, cache_v, jnp.asarray(bias))


def _mlstm_kernel(q_ref, kt_ref, v_ref, og_ref, ar_ref, bc_ref, ac_ref, nw_ref, c0_ref, n0_ref, m0_ref,
                  y_ref, c_ref, n_ref, m_ref, ct_ref, *, SB, L):
    j = pl.program_id(1)
    heads = [(sb, hd) for sb in range(SB) for hd in range(N_HEADS_B)]

    @pl.when(j == 0)
    def _():
        for sb, hd in heads:
            ct_ref[sb, hd, :, :DV_B] = c0_ref[sb, hd].T
            ct_ref[sb, hd, :, DV_B:] = jnp.broadcast_to(n0_ref[sb, hd], (DQK_B, LANES))
        m_ref[...] = m0_ref[...]

    causal = (lax.broadcasted_iota(jnp.int32, (L, L), 0) >= lax.broadcasted_iota(jnp.int32, (L, L), 1))
    ones = jnp.ones((L, LANES), BF16)
    gates = []
    for sb in range(SB):
        a4 = ac_ref[sb]
        b4 = bc_ref[sb]
        m_row = m_ref[sb]
        a_last = a4[L - 1:L]
        b_last = b4[L - 1:L]
        mx = jnp.maximum(a_last, m_row)
        m_new = b_last + mx
        gates.append(dict(a4=a4, b4=b4, a_last=a_last, m_row=m_row, decay4=jnp.exp(b_last + m_row - m_new),
                          rho4=jnp.exp(a_last - mx)))
        m_ref[sb] = m_new

    def lanes(x):
        return jnp.broadcast_to(x, (x.shape[0], LANES))

    def col(x, hd):
        return x[:, hd:hd + 1]

    qs = {h: q_ref[h[0], :, h[1] * DQK_B:(h[1] + 1) * DQK_B] for h in heads}
    kts = {h: kt_ref[h[0], h[1] * DQK_B:(h[1] + 1) * DQK_B, :] for h in heads}
    cts = {h: ct_ref[h[0], h[1]] for h in heads}
    qk = {h: jnp.dot(qs[h], kts[h], preferred_element_type=F32) for h in heads}
    qc = {h: jnp.dot(qs[h], cts[h].astype(BF16), preferred_element_type=F32) for h in heads}
    lhs, a_bs = {}, {}
    for h in heads:
        sb, hd = h
        gt = gates[sb]
        a_row = ar_ref[sb, hd:hd + 1, :]
        a_bs[h] = lanes(col(gt["a4"], hd))
        a_full = jnp.concatenate([a_bs[h]] * (L // LANES), axis=1) if L >= LANES else a_bs[h][:, :L]
        w = jnp.exp(jnp.where(causal, a_row - a_full, -jnp.inf))
        s = (qk[h] * w).astype(BF16)
        kw = (kts[h].astype(F32) * jnp.exp(a_row - col(gt["a_last"], hd))).astype(BF16)
        lhs[h] = jnp.concatenate([s, kw], axis=0)
    pu = {}
    for h in heads:
        sb, hd = h
        v_aug = jnp.concatenate([v_ref[sb, :, hd * DV_B:(hd + 1) * DV_B], ones], axis=1)
        pu[h] = jnp.dot(lhs[h], v_aug, preferred_element_type=F32)
    ones_sum = jnp.ones((DV_B, LANES), BF16)
    for h in heads:
        sb, hd = h
        gt = gates[sb]
        a_b = a_bs[h]
        m_b = lanes(col(gt["m_row"], hd))
        big = jnp.maximum(a_b, m_b)
        r_b = jnp.exp(a_b - big)
        g_b = jnp.exp(m_b - big)
        e_b = jnp.exp(-(lanes(col(gt["b4"], hd)) + big))
        p, c = pu[h], qc[h]
        den = r_b * p[:L, DV_B:] + g_b * c[:, DV_B:]
        inv = 1.0 / jnp.maximum(jnp.abs(den), e_b)
        hh = jnp.concatenate([(r_b * p[:L, k * LANES:(k + 1) * LANES] + g_b * c[:, k * LANES:(k + 1) * LANES]) * inv
                              for k in range(DV_B // LANES)], axis=1)
        mu = jnp.dot(hh.astype(BF16), ones_sum, preferred_element_type=F32) * (1.0 / DV_B)
        hc = hh - jnp.concatenate([mu] * (DV_B // LANES), axis=1)
        var = jnp.dot(jnp.square(hc).astype(BF16), ones_sum, preferred_element_type=F32) * (1.0 / DV_B)
        rstd = lax.rsqrt(var + LN_EPS)
        yn = hc * jnp.concatenate([rstd] * (DV_B // LANES), axis=1) * nw_ref[:, hd * DV_B:(hd + 1) * DV_B]
        og = og_ref[sb, :, hd * DV_B:(hd + 1) * DV_B]
        y_ref[sb, :, hd * DV_B:(hd + 1) * DV_B] = (jax.nn.sigmoid(og) * yn).astype(BF16)
        ct_ref[sb, hd] = col(gt["decay4"], hd) * cts[h] + col(gt["rho4"], hd) * pu[h][L:]

    @pl.when(j == pl.num_programs(1) - 1)
    def _():
        for sb, hd in heads:
            c_ref[sb, hd] = ct_ref[sb, hd, :, :DV_B].T
            n_ref[sb, hd] = ct_ref[sb, hd, :, DV_B:DV_B + 1]


def _mlstm_call(qm, kt, vm, og, a_row, b_col, a_col, norm_w, c0, n0, m0, *, SB, L):
    N, T, _ = qm.shape

    def tok(c):
        return pl.BlockSpec((SB, L, c), lambda i, j: (i, j, 0))

    def tr(c):
        return pl.BlockSpec((SB, c, L), lambda i, j: (i, 0, j))

    st_c = pl.BlockSpec((SB, N_HEADS_B, DV_B, DQK_B), lambda i, j: (i, 0, 0, 0))
    st_n = pl.BlockSpec((SB, N_HEADS_B, DQK_B, 1), lambda i, j: (i, 0, 0, 0))
    st_m = pl.BlockSpec((SB, 1, LANES), lambda i, j: (i, 0, 0))
    return pl.pallas_call(
        functools.partial(_mlstm_kernel, SB=SB, L=L),
        grid=(N // SB, T // L),
        in_specs=[tok(QK_B), tr(QK_B), tok(V_B), tok(V_B), tr(8), tok(LANES), tok(LANES),
                  _const_spec((1, V_B)), st_c, st_n, st_m],
        out_specs=[tok(V_B), st_c, st_n, st_m],
        out_shape=[jax.ShapeDtypeStruct((N, T, V_B), BF16),
                   jax.ShapeDtypeStruct((N, N_HEADS_B, DV_B, DQK_B), F32),
                   jax.ShapeDtypeStruct((N, N_HEADS_B, DQK_B, 1), F32),
                   jax.ShapeDtypeStruct((N, 1, LANES), F32)],
        scratch_shapes=[pltpu.VMEM((SB, N_HEADS_B, DQK_B, DV_B + LANES), F32)],
        compiler_params=pltpu.CompilerParams(dimension_semantics=("parallel", "arbitrary")),
        name="mlstm",
    )(qm, kt, vm, og, a_row, b_col, a_col, norm_w, c0, n0, m0)


def _post_kernel(x_ref, mod_ref, ya_ref, yb_ref, ga_ref, gb_ref, cin_ref,
                 wpa_ref, wpb_ref, wo_ref, l1w_ref, l1b_ref, wup_ref, bup_ref, cw_ref, cb_ref,
                 wdn_ref, l2w_ref, l2b_ref,
                 y_ref, cout_ref, ctx_ref, act_ref, *, S, R):
    SR = S * R
    FC = D_FF // FFN_CHUNKS

    @pl.when(pl.program_id(1) == 0)
    def _():
        cout_ref[...] = cin_ref[...]

    g1 = mod_ref[:, 2:3, :]
    sh2 = mod_ref[:, 3:4, :]
    sc2 = mod_ref[:, 4:5, :]
    g2 = mod_ref[:, 5:6, :]
    x = x_ref[...]

    pa = jnp.dot(ya_ref[...].reshape(SR, Q_A), wpa_ref[...], preferred_element_type=F32)
    pb = jnp.dot(yb_ref[...].reshape(SR, V_B), wpb_ref[...], preferred_element_type=F32)
    merged = (jax.nn.sigmoid(ga_ref[...].reshape(SR, D_MODEL)) * pa
              + jax.nn.sigmoid(gb_ref[...].reshape(SR, D_MODEL)) * pb)
    mo = _bdot(merged, wo_ref[...]).reshape(S, R, D_MODEL)
    x1 = _layer_norm(ALPHA * x + g1 * mo, l1w_ref[...], l1b_ref[...])
    h2 = (x1 * (1.0 + sc2) + sh2).reshape(SR, D_MODEL).astype(BF16)

    def conv(lo):
        u = (jnp.dot(h2, wup_ref[:, lo:lo + FC], preferred_element_type=F32)
             + bup_ref[:, lo:lo + FC]).reshape(S, R, FC)
        ctx_ref[:, 6:8, :] = cout_ref[:, :, lo:lo + FC]
        ctx_ref[:, 8:, :] = u
        cout_ref[:, :, lo:lo + FC] = ctx_ref[:, R + 6:R + 8, :]
        return (cb_ref[:, lo:lo + FC]
                + cw_ref[0:1, lo:lo + FC] * ctx_ref[:, 6:R + 6, :]
                + cw_ref[1:2, lo:lo + FC] * ctx_ref[:, 7:R + 7, :]
                + cw_ref[2:3, lo:lo + FC] * u)

    for ci in range(FFN_CHUNKS):
        lo = ci * FC
        a = conv(lo)
        g = conv(D_FF + lo)
        act_ref[:, lo:lo + FC] = (a * jax.nn.gelu(g, approximate=True)).reshape(SR, FC).astype(BF16)

    f = jnp.dot(act_ref[...], wdn_ref[...], preferred_element_type=F32).reshape(S, R, D_MODEL)
    y_ref[...] = _layer_norm(ALPHA * x1 + g2 * f, l2w_ref[...], l2b_ref[...])


def _post_call(x, mod, ya, yb, ga, gb, conv_in, wpa, wpb, wo, l1w, l1b, wup, bup, cw, cb, wdn, l2w, l2b,
               *, S, R):
    N, T, _ = x.shape
    nsb, nrb = N // S, T // R
    FC = D_FF // FFN_CHUNKS

    def tok(c):
        return pl.BlockSpec((S, R, c), lambda i, j: (i, j, 0))

    conv_spec = pl.BlockSpec((S, CONV_W - 1, 2 * D_FF), lambda i, j: (i, 0, 0))
    return pl.pallas_call(
        functools.partial(_post_kernel, S=S, R=R),
        grid=(nsb, nrb),
        in_specs=[tok(D_MODEL), pl.BlockSpec((S, 6, D_MODEL), lambda i, j: (i, 0, 0)),
                  tok(Q_A), tok(V_B), tok(D_MODEL), tok(D_MODEL), conv_spec,
                  _const_spec((Q_A, D_MODEL)), _const_spec((V_B, D_MODEL)), _const_spec((D_MODEL, D_MODEL)),
                  _const_spec((1, D_MODEL)), _const_spec((1, D_MODEL)),
                  _const_spec((D_MODEL, 2 * D_FF)), _const_spec((1, 2 * D_FF)),
                  _const_spec((CONV_W, 2 * D_FF)), _const_spec((1, 2 * D_FF)),
                  _const_spec((D_FF, D_MODEL)), _const_spec((1, D_MODEL)), _const_spec((1, D_MODEL))],
        out_specs=[tok(D_MODEL), conv_spec],
        out_shape=[jax.ShapeDtypeStruct((N, T, D_MODEL), F32),
                   jax.ShapeDtypeStruct((N, CONV_W - 1, 2 * D_FF), F32)],
        scratch_shapes=[pltpu.VMEM((S, R + 8, FC), F32), pltpu.VMEM((S * R, D_FF), BF16)],
        compiler_params=pltpu.CompilerParams(dimension_semantics=("parallel", "arbitrary")),
        name="post",
    )(x, mod, ya, yb, ga, gb, conv_in, wpa, wpb, wo, l1w, l1b, wup, bup, cw, cb, wdn, l2w, l2b)


def _pad_rows(a, rows):
    return jnp.pad(a, ((0, rows - a.shape[0]),) + ((0, 0),) * (a.ndim - 1))


def kernel(x_prompt, x_sample, cache_attn_k, cache_attn_v, state_mlstm_C, state_mlstm_n, state_mlstm_m, state_ffn_conv, c_prompt, c_sample, w_ada, b_ada, w_in, b_igate, b_fgate, attn_sinks, mlstm_norm_w, w_proj_a, w_proj_b, w_out, ln1_w, ln1_b, w_up, b_up, conv_w, conv_b, w_down, ln2_w, ln2_b):
    Bp, Tp, D = x_prompt.shape
    Bs, Ts, _ = x_sample.shape
    l = 0

    mod = _mod_call(jnp.concatenate([c_prompt, c_sample], axis=0), w_ada[l], b_ada[l][None])
    mod_p = mod[:Bp].reshape(Bp, 6, D)
    mod_s = mod[Bp:].reshape(Bs, 6, D)

    w = w_in[l]
    o_km, o_vm, o_i, o_f, o_o = Z_OFF[4], Z_OFF[5], Z_OFF[6], Z_OFF[7], Z_OFF[8]
    lane_pad = ((0, 0), (0, LANES - N_HEADS_B))
    w_main = jnp.concatenate([w[:, :o_km], w[:, o_vm:o_i], w[:, o_o:],
                              jnp.pad(w[:, o_i:o_f], lane_pad), jnp.pad(w[:, o_f:o_o], lane_pad)],
                             axis=1).astype(BF16)
    w_qt = w[:, :Z_OFF[1]].T.astype(BF16)
    w_kt = w[:, o_km:o_vm].T.astype(BF16)
    w_i = _pad_rows(w[:, o_i:o_f].T, 8).astype(BF16)
    w_f = _pad_rows(w[:, o_f:o_o].T, 8).astype(BF16)
    b_i = _pad_rows(b_igate[l][:, None], 8)
    b_f = _pad_rows(b_fgate[l][:, None], 8)
    b_ic = jnp.pad(b_igate[l][None], lane_pad)
    b_fc = jnp.pad(b_fgate[l][None], lane_pad)
    inproj_w = (w_main, w_qt, w_kt, w_i, w_f, b_i, b_f, b_ic, b_fc)
    sinks = attn_sinks[l]
    norm_w = mlstm_norm_w[l][None]
    post_w = (w_proj_a[l].astype(BF16), w_proj_b[l].astype(BF16), w_out[l].astype(BF16),
              ln1_w[l][None], ln1_b[l][None], w_up[l].astype(BF16), b_up[l][None], conv_w[l], conv_b[l][None],
              w_down[l].astype(BF16), ln2_w[l][None], ln2_b[l][None])

    qa, ka, va, qm, kt, vm, og, ga, gb, a_row, b_col, a_col = _inproj_call(
        x_prompt, mod_p, *inproj_w, S=1, R=INPROJ_ROWS, L=MLSTM_CHUNK, q_transposed=True)
    ya = _attn_prompt_call(qa, ka, va, sinks)
    yb, p_c, p_n, p_m = _mlstm_call(
        qm, kt, vm, og, a_row, b_col, a_col, norm_w,
        jnp.zeros((Bp, N_HEADS_B, DV_B, DQK_B), F32), jnp.zeros((Bp, N_HEADS_B, DQK_B, 1), F32),
        jnp.zeros((Bp, 1, LANES), F32), SB=4, L=MLSTM_CHUNK)
    p_m = p_m[:, 0, :N_HEADS_B]
    y_prompt, p_conv = _post_call(x_prompt, mod_p, ya, yb, ga, gb,
                                  jnp.zeros((Bp, CONV_W - 1, 2 * D_FF), F32), *post_w, S=1, R=256)
    p_k = ka[:, Tp - KV_WIN:].reshape(Bp, KV_WIN, N_KV_A, HEAD_DIM_A)
    p_v = va[:, Tp - KV_WIN:].reshape(Bp, KV_WIN, N_KV_A, HEAD_DIM_A)

    qa, ka, va, qm, kt, vm, og, ga, gb, a_row, b_col, a_col = _inproj_call(
        x_sample, mod_s, *inproj_w, S=16, R=Ts, L=Ts, q_transposed=False)
    ya, s_k, s_v = _attn_sample_call(qa, ka, va, cache_attn_k[l].reshape(Bs, KV_WIN, KV_A_W),
                                     cache_attn_v[l].reshape(Bs, KV_WIN, KV_A_W), sinks, NS=8)
    yb, s_c, s_n, s_m = _mlstm_call(
        qm, kt, vm, og, a_row, b_col, a_col, norm_w,
        state_mlstm_C[l], state_mlstm_n[l][:, :, :, None], jnp.pad(state_mlstm_m[l][:, None, :], ((0, 0),) + lane_pad),
        SB=4, L=Ts)
    s_m = s_m[:, 0, :N_HEADS_B]
    y_sample, s_conv = _post_call(x_sample, mod_s, ya, yb, ga, gb, state_ffn_conv[l], *post_w, S=8, R=Ts)

    def st(a, shape):
        return a.reshape((1,) + shape)

    return (y_prompt, y_sample,
            st(p_k, (Bp, KV_WIN, N_KV_A, HEAD_DIM_A)), st(p_v, (Bp, KV_WIN, N_KV_A, HEAD_DIM_A)),
            st(p_c, (Bp, N_HEADS_B, DV_B, DQK_B)), st(p_n, (Bp, N_HEADS_B, DQK_B)), st(p_m, (Bp, N_HEADS_B)),
            st(p_conv, (Bp, CONV_W - 1, 2 * D_FF)),
            st(s_k, (Bs, KV_WIN, N_KV_A, HEAD_DIM_A)), st(s_v, (Bs, KV_WIN, N_KV_A, HEAD_DIM_A)),
            st(s_c, (Bs, N_HEADS_B, DV_B, DQK_B)), st(s_n, (Bs, N_HEADS_B, DQK_B)), st(s_m, (Bs, N_HEADS_B)),
            st(s_conv, (Bs, CONV_W - 1, 2 * D_FF)))
```

```python
import functools

import numpy as np
import jax
import jax.numpy as jnp
from jax import lax
from jax.experimental import pallas as pl
from jax.experimental.pallas import tpu as pltpu

F32 = jnp.float32
BF16 = jnp.bfloat16

D_MODEL = 1024
DEPTH = 1
PAST_LEN = 2048
CHUNK = 64
N_HEADS_A = 16
N_KV_A = 2
HEAD_DIM_A = 64
GROUP_A = N_HEADS_A // N_KV_A
WINDOW = 128
WINDOW_CHUNKS = WINDOW // CHUNK
KV_WIN = min(WINDOW, PAST_LEN)
N_HEADS_B = 4
DQK_B = 128
DV_B = 256
D_FF = 2816
CONV_W = 3
LN_EPS = 1e-5
ALPHA = (2 * DEPTH) ** 0.25
Q_A = N_HEADS_A * HEAD_DIM_A
KV_A_W = N_KV_A * HEAD_DIM_A
QK_B = N_HEADS_B * DQK_B
V_B = N_HEADS_B * DV_B
Z_PARTS = (Q_A, KV_A_W, KV_A_W, QK_B, QK_B, V_B, N_HEADS_B, N_HEADS_B, V_B, D_MODEL, D_MODEL)
Z_OFF = tuple(int(v) for v in np.cumsum((0,) + Z_PARTS))

LANES = 128

M_QA = 0
M_KA = M_QA + Q_A
M_VA = M_KA + KV_A_W
M_QM = M_VA + KV_A_W
M_KM = M_QM + QK_B
M_VM = M_KM + QK_B
M_OG = M_VM + V_B
M_GA = M_OG + V_B
M_GB = M_GA + D_MODEL
M_GI = M_GB + D_MODEL
M_GF = M_GI + LANES
M_END = M_GF + LANES
ATT_QB = 128
ATT_SUB = 8
ATT_W = 256
FFN_CHUNKS = 11
MLSTM_CHUNK = 256
INPROJ_ROWS = 512


def _bdot(a, b):
    return jnp.dot(a.astype(BF16), b.astype(BF16), preferred_element_type=F32)


def _bdot_nt(a, b):
    return lax.dot_general(a.astype(BF16), b.astype(BF16), (((1,), (1,)), ((), ())),
                           preferred_element_type=F32)


def _const_spec(shape):
    nd = len(shape)
    return pl.BlockSpec(shape, lambda *_: (0,) * nd, pipeline_mode=pl.Buffered(1))


def _layer_norm(x, w, b):
    mu = jnp.mean(x, -1, keepdims=True)
    xc = x - mu
    var = jnp.mean(jnp.square(xc), -1, keepdims=True)
    return xc * lax.rsqrt(var + LN_EPS) * w + b


def _mod_kernel(c_ref, w_ref, b_ref, o_ref):
    c = c_ref[...]
    s = c * jax.nn.sigmoid(c)
    o_ref[...] = _bdot(s, w_ref[...]) + b_ref[...]


def _mod_call(c_all, w_ada, b_ada):
    n = c_all.shape[0]
    tn = 1536
    return pl.pallas_call(
        _mod_kernel,
        grid=(6 * D_MODEL // tn,),
        in_specs=[pl.BlockSpec((n, D_MODEL), lambda j: (0, 0)),
                  pl.BlockSpec((D_MODEL, tn), lambda j: (0, j)),
                  pl.BlockSpec((1, tn), lambda j: (0, j))],
        out_specs=pl.BlockSpec((n, tn), lambda j: (0, j)),
        out_shape=jax.ShapeDtypeStruct((n, 6 * D_MODEL), F32),
        name="mod",
    )(c_all, w_ada, b_ada)


def _log_sigmoid(x):
    return jnp.minimum(x, 0.0) - jnp.log1p(jnp.exp(-jnp.abs(x)))


def _seg_scan(x, L, axis, op, ident):
    pos = lax.broadcasted_iota(jnp.int32, x.shape, axis) % L
    step = 1
    while step < L:
        x = op(x, jnp.where(pos >= step, pltpu.roll(x, step, axis=axis), ident))
        step *= 2
    return x


def _inproj_kernel(x_ref, mod_ref, w_ref, bic_ref, bfc_ref,
                   qa_ref, ka_ref, va_ref, qm_ref, kt_ref, vm_ref, og_ref, ga_ref, gb_ref,
                   ar_ref, bc_ref, ac_ref, *, S, R, L, q_transposed):
    SR = S * R
    sh = mod_ref[:, 0:1, :]
    sc = mod_ref[:, 1:2, :]
    h = (x_ref[...] * (1.0 + sc) + sh).reshape(SR, D_MODEL).astype(BF16)

    def proj(lo, hi):
        return jnp.dot(h, w_ref[:, lo:hi], preferred_element_type=F32)

    q = proj(M_QA, M_KA) * (HEAD_DIM_A ** -0.5)
    if q_transposed:
        qt = q.T.astype(BF16)
        for s in range(S):
            qa_ref[s] = qt[:, s * R:(s + 1) * R]
    else:
        qa_ref[...] = q.astype(BF16).reshape(S, R, Q_A)
    kv = proj(M_KA, M_QM)
    ka_ref[...] = kv[:, :KV_A_W].reshape(S, R, KV_A_W)
    va_ref[...] = kv[:, KV_A_W:].reshape(S, R, KV_A_W)
    qm_ref[...] = proj(M_QM, M_KM).astype(BF16).reshape(S, R, QK_B)
    kt = (proj(M_KM, M_VM) * (DQK_B ** -0.5)).T.astype(BF16)
    vm_ref[...] = proj(M_VM, M_OG).astype(BF16).reshape(S, R, V_B)
    og_ref[...] = proj(M_OG, M_GA).reshape(S, R, V_B)
    ga_ref[...] = proj(M_GA, M_GB).reshape(S, R, D_MODEL)
    gb_ref[...] = proj(M_GB, M_GI).reshape(S, R, D_MODEL)

    gates = proj(M_GI, M_END)
    i_col = gates[:, :LANES] + bic_ref[...]
    logf_col = _log_sigmoid(gates[:, LANES:] + bfc_ref[...])
    b_col = _seg_scan(logf_col, L, 0, jnp.add, 0.0)
    a_col = _seg_scan(i_col - b_col, L, 0, jnp.maximum, -jnp.inf)
    bc_ref[...] = b_col.reshape(S, R, LANES)
    ac_ref[...] = a_col.reshape(S, R, LANES)
    i_row = i_col.T[:8]
    logf_row = logf_col.T[:8]
    a_row = i_row - _seg_scan(logf_row, L, 1, jnp.add, 0.0)
    for s in range(S):
        kt_ref[s] = kt[:, s * R:(s + 1) * R]
        ar_ref[s] = a_row[:, s * R:(s + 1) * R]


def _inproj_call(x, mod, w_main, b_ic, b_fc, *, S, R, L, q_transposed):
    N, T, _ = x.shape
    nsb, nrb = N // S, T // R

    def tok(c):
        return pl.BlockSpec((S, R, c), lambda i, j: (i, j, 0))

    def tok_shape(c, dt):
        return jax.ShapeDtypeStruct((N, T, c), dt)

    def tr(c):
        return pl.BlockSpec((S, c, R), lambda i, j: (i, 0, j))

    return pl.pallas_call(
        functools.partial(_inproj_kernel, S=S, R=R, L=L, q_transposed=q_transposed),
        grid=(nsb, nrb),
        in_specs=[tok(D_MODEL),
                  pl.BlockSpec((S, 6, D_MODEL), lambda i, j: (i, 0, 0)),
                  _const_spec((D_MODEL, M_END)),
                  _const_spec((1, LANES)), _const_spec((1, LANES))],
        out_specs=[tr(Q_A) if q_transposed else tok(Q_A), tok(KV_A_W), tok(KV_A_W), tok(QK_B), tr(QK_B), tok(V_B),
                   tok(V_B), tok(D_MODEL), tok(D_MODEL), tr(8), tok(LANES), tok(LANES)],
        out_shape=[jax.ShapeDtypeStruct((N, Q_A, T), BF16) if q_transposed else tok_shape(Q_A, BF16),
                   tok_shape(KV_A_W, F32), tok_shape(KV_A_W, F32),
                   tok_shape(QK_B, BF16), jax.ShapeDtypeStruct((N, QK_B, T), BF16), tok_shape(V_B, BF16),
                   tok_shape(V_B, F32), tok_shape(D_MODEL, F32), tok_shape(D_MODEL, F32),
                   jax.ShapeDtypeStruct((N, 8, T), F32), tok_shape(LANES, F32), tok_shape(LANES, F32)],
        compiler_params=pltpu.CompilerParams(dimension_semantics=("parallel", "parallel")),
        name="inproj",
    )(x, mod, w_main, b_ic, b_fc)


def _alibi_slopes():
    return 2.0 ** (-8.0 * np.arange(1, N_HEADS_A + 1, dtype=np.float64) / N_HEADS_A)


def _attn_bias(qpos, kpos, kvalid, transposed=False):
    qpos = np.asarray(qpos)[:, None]
    kpos = np.asarray(kpos)[None, :]
    qc, kc = qpos // CHUNK, kpos // CHUNK
    visible = (kpos >= 0) & (kc <= qc) & (kc >= qc - WINDOW_CHUNKS) & np.asarray(kvalid)[None, :]
    dist = np.abs(qpos - kpos).astype(np.float64)
    slopes = _alibi_slopes()
    out = np.zeros((N_KV_A, GROUP_A // 2, qpos.shape[0], 2, kpos.shape[1]), np.float32)
    for kv in range(N_KV_A):
        for p in range(GROUP_A // 2):
            for half in range(2):
                hd = kv * GROUP_A + 2 * p + half
                out[kv, p, :, half, :] = np.where(visible, -slopes[hd] * dist, -np.inf)
    if transposed:
        return out.transpose(0, 3, 4, 1, 2).reshape(N_KV_A, 2 * kpos.shape[1], (GROUP_A // 2) * qpos.shape[0])
    return out.reshape(N_KV_A, (GROUP_A // 2) * qpos.shape[0], 2 * kpos.shape[1])


def _attn_core(q, kwin, vwin, bias_of, sinks_ref, store):
    Mq = q.shape[0]
    W = kwin.shape[0]
    npair = GROUP_A // 2
    lane = lax.broadcasted_iota(jnp.int32, (npair * Mq, LANES), 1)
    for kv in range(N_KV_A):
        kk = kwin[:, kv * HEAD_DIM_A:(kv + 1) * HEAD_DIM_A].astype(BF16)
        vv = vwin[:, kv * HEAD_DIM_A:(kv + 1) * HEAD_DIM_A].astype(BF16)
        zero = jnp.zeros_like(kk)
        one = jnp.ones_like(vv)
        k2 = jnp.concatenate([jnp.concatenate([kk, zero], axis=1),
                              jnp.concatenate([zero, kk], axis=1)], axis=0)
        v_aug = jnp.concatenate([vv, one, one, vv], axis=1)
        qs = jnp.concatenate([q[:, (kv * npair + p) * LANES:(kv * npair + p + 1) * LANES]
                              for p in range(npair)], axis=0)
        s = _bdot_nt(qs, k2) + bias_of(kv)
        probs, sink_terms = [], []
        for half in range(2):
            sh = s[:, half * W:(half + 1) * W]
            sink = jnp.concatenate(
                [jnp.full((Mq, 1), sinks_ref[kv * GROUP_A + 2 * p + half], F32) for p in range(npair)], axis=0)
            mx = jnp.maximum(jnp.max(sh, axis=1, keepdims=True), sink)
            probs.append(jnp.exp(sh - mx).astype(BF16))
            sink_terms.append(jnp.exp(sink - mx))
        o = jnp.dot(jnp.concatenate(probs, axis=0), v_aug, preferred_element_type=F32)
        oe, oo = o[:npair * Mq], o[npair * Mq:]
        ye = oe[:, :LANES] / (oe[:, LANES:] + sink_terms[0])
        yo = oo[:, LANES:] / (oo[:, :LANES] + sink_terms[1])
        y = jnp.where(lane < HEAD_DIM_A, ye, yo).astype(BF16)
        for p in range(npair):
            store((kv * npair + p) * LANES, y[p * Mq:(p + 1) * Mq])


def _attn_prompt_kernel(sinks_ref, qt_ref, kp_ref, kc_ref, vp_ref, vc_ref, bias0_ref, bias1_ref, o_ref):
    Mq = ATT_QB
    npair = GROUP_A // 2
    keys = jnp.concatenate([kp_ref[0], kc_ref[0]], axis=0)
    vals_t = jnp.concatenate([vp_ref[0], vc_ref[0]], axis=0).T.astype(BF16)
    ones = jnp.ones((HEAD_DIM_A, ATT_W), BF16)
    pair_of_lane = lax.broadcasted_iota(jnp.int32, (1, npair * Mq), 1) // Mq
    units = [(sub, kv) for sub in range(ATT_SUB) for kv in range(N_KV_A)]
    sts = {}
    for sub, kv in units:
        kk = keys[sub * Mq:sub * Mq + ATT_W, kv * HEAD_DIM_A:(kv + 1) * HEAD_DIM_A].astype(BF16)
        zero = jnp.zeros_like(kk)
        k2 = jnp.concatenate([jnp.concatenate([kk, zero], axis=1),
                              jnp.concatenate([zero, kk], axis=1)], axis=0)
        qt = jnp.concatenate([qt_ref[0, (kv * npair + p) * LANES:(kv * npair + p + 1) * LANES, sub * Mq:(sub + 1) * Mq]
                              for p in range(npair)], axis=1)
        sts[sub, kv] = jnp.dot(k2, qt, preferred_element_type=F32)
    probs, sink_terms = {}, {}
    for sub, kv in units:
        st = sts[sub, kv] + (bias0_ref[0, kv] if sub == 0 else bias1_ref[0, kv])
        for half in range(2):
            sh = st[half * ATT_W:(half + 1) * ATT_W]
            sink = jnp.zeros((1, npair * Mq), F32)
            for p in range(npair):
                sink = jnp.where(pair_of_lane == p, sinks_ref[kv * GROUP_A + 2 * p + half], sink)
            mx = jnp.maximum(jnp.max(sh, axis=0, keepdims=True), sink)
            probs[sub, kv, half] = jnp.exp(sh - mx).astype(BF16)
            sink_terms[sub, kv, half] = jnp.exp(sink - mx)
    ots = {}
    for sub, kv in units:
        vt = vals_t[kv * HEAD_DIM_A:(kv + 1) * HEAD_DIM_A, sub * Mq:sub * Mq + ATT_W]
        for half in range(2):
            lhs = jnp.concatenate([vt, ones] if half == 0 else [ones, vt], axis=0)
            ots[sub, kv, half] = jnp.dot(lhs, probs[sub, kv, half], preferred_element_type=F32)
    for sub, kv in units:
        oe, oo = ots[sub, kv, 0], ots[sub, kv, 1]
        ye = oe[:HEAD_DIM_A] / (oe[HEAD_DIM_A:] + sink_terms[sub, kv, 0])
        yo = oo[HEAD_DIM_A:] / (oo[:HEAD_DIM_A] + sink_terms[sub, kv, 1])
        yt = jnp.concatenate([ye, yo], axis=0)
        for p in range(npair):
            col = (kv * npair + p) * LANES
            o_ref[0, sub * Mq:(sub + 1) * Mq, col:col + LANES] = yt[:, p * Mq:(p + 1) * Mq].T.astype(BF16)


def _attn_prompt_call(qa_t, ka, va, sinks):
    N, _, T = qa_t.shape
    rows = ATT_QB * ATT_SUB
    rel_q = ATT_QB + np.arange(ATT_QB)
    rel_k = np.arange(ATT_W)
    bias = np.stack([_attn_bias(rel_q, rel_k, rel_k >= ATT_QB, transposed=True),
                     _attn_bias(rel_q, rel_k, rel_k >= 0, transposed=True)])
    kv_prev = pl.BlockSpec((1, ATT_QB, KV_A_W), lambda n, j: (n, jnp.maximum(ATT_SUB * j - 1, 0), 0))
    kv_cur = pl.BlockSpec((1, rows, KV_A_W), lambda n, j: (n, j, 0))
    return pl.pallas_call(
        _attn_prompt_kernel,
        grid=(N, T // rows),
        in_specs=[pl.BlockSpec(memory_space=pltpu.SMEM),
                  pl.BlockSpec((1, Q_A, rows), lambda n, j: (n, 0, j)),
                  kv_prev, kv_cur, kv_prev, kv_cur,
                  pl.BlockSpec((1,) + bias.shape[1:], lambda n, j: (jnp.minimum(j, 1), 0, 0, 0)),
                  pl.BlockSpec((1,) + bias.shape[1:], lambda n, j: (1, 0, 0, 0), pipeline_mode=pl.Buffered(1))],
        out_specs=pl.BlockSpec((1, rows, Q_A), lambda n, j: (n, j, 0)),
        out_shape=jax.ShapeDtypeStruct((N, T, Q_A), BF16),
        compiler_params=pltpu.CompilerParams(dimension_semantics=("parallel", "parallel")),
        name="attn_prompt",
    )(sinks, qa_t, ka, ka, va, va, jnp.asarray(bias), jnp.asarray(bias))


def _attn_sample_kernel(sinks_ref, q_ref, kn_ref, vn_ref, kc_ref, vc_ref, bias_ref,
                        o_ref, ko_ref, vo_ref, *, T, NS):
    pad = jnp.zeros((ATT_W - KV_WIN - T, KV_A_W), F32)
    for s in range(NS):
        kwin = jnp.concatenate([kc_ref[s], kn_ref[s], pad], axis=0)
        vwin = jnp.concatenate([vc_ref[s], vn_ref[s], pad], axis=0)
        ko_ref[s] = kwin[T:T + KV_WIN]
        vo_ref[s] = vwin[T:T + KV_WIN]

        def store(col, val, s=s):
            o_ref[s, :, col:col + LANES] = val

        _attn_core(q_ref[s], kwin, vwin, lambda kv: bias_ref[kv], sinks_ref, store)


def _attn_sample_call(qa, ka, va, cache_k, cache_v, sinks, *, NS):
    N, T, _ = qa.shape
    qpos = PAST_LEN + np.arange(T)
    kpos = PAST_LEN - KV_WIN + np.arange(ATT_W)
    bias = _attn_bias(qpos, kpos, np.arange(ATT_W) < KV_WIN + T)
    new = pl.BlockSpec((NS, T, KV_A_W), lambda n: (n, 0, 0))
    cache = pl.BlockSpec((NS, KV_WIN, KV_A_W), lambda n: (n, 0, 0))
    cache_shape = jax.ShapeDtypeStruct((N, KV_WIN, KV_A_W), F32)
    return pl.pallas_call(
        functools.partial(_attn_sample_kernel, T=T, NS=NS),
        grid=(N // NS,),
        in_specs=[pl.BlockSpec(memory_space=pltpu.SMEM),
                  pl.BlockSpec((NS, T, Q_A), lambda n: (n, 0, 0)),
                  new, new, cache, cache,
                  _const_spec(bias.shape)],
        out_specs=[pl.BlockSpec((NS, T, Q_A), lambda n: (n, 0, 0)), cache, cache],
        out_shape=[jax.ShapeDtypeStruct((N, T, Q_A), BF16), cache_shape, cache_shape],
        compiler_params=pltpu.CompilerParams(dimension_semantics=("parallel",)),
        name="attn_sample",
    )(sinks, qa, ka, va, cache_k, cache_v, jnp.asarray(bias))


def _mlstm_kernel(q_ref, kt_ref, v_ref, og_ref, ar_ref, bc_ref, ac_ref, nw_ref, c0_ref, n0_ref, m0_ref,
                  y_ref, c_ref, n_ref, m_ref, ct_ref, *, SB, L):
    j = pl.program_id(1)
    heads = [(sb, hd) for sb in range(SB) for hd in range(N_HEADS_B)]

    @pl.when(j == 0)
    def _():
        for sb, hd in heads:
            ct_ref[sb, hd, :, :DV_B] = c0_ref[sb, hd].T
            ct_ref[sb, hd, :, DV_B:] = jnp.broadcast_to(n0_ref[sb, hd], (DQK_B, LANES))
        m_ref[...] = m0_ref[...]

    causal = (lax.broadcasted_iota(jnp.int32, (L, L), 0) >= lax.broadcasted_iota(jnp.int32, (L, L), 1))
    ones = jnp.ones((L, LANES), BF16)
    gates = []
    for sb in range(SB):
        a4 = ac_ref[sb]
        b4 = bc_ref[sb]
        m_row = m_ref[sb]
        a_last = a4[L - 1:L]
        b_last = b4[L - 1:L]
        mx = jnp.maximum(a_last, m_row)
        m_new = b_last + mx
        gates.append(dict(a4=a4, b4=b4, a_last=a_last, m_row=m_row, decay4=jnp.exp(b_last + m_row - m_new),
                          rho4=jnp.exp(a_last - mx)))
        m_ref[sb] = m_new

    def lanes(x):
        return jnp.broadcast_to(x, (x.shape[0], LANES))

    def col(x, hd):
        return x[:, hd:hd + 1]

    qs = {h: q_ref[h[0], :, h[1] * DQK_B:(h[1] + 1) * DQK_B] for h in heads}
    kts = {h: kt_ref[h[0], h[1] * DQK_B:(h[1] + 1) * DQK_B, :] for h in heads}
    cts = {h: ct_ref[h[0], h[1]] for h in heads}
    qk = {h: jnp.dot(qs[h], kts[h], preferred_element_type=F32) for h in heads}
    qc = {h: jnp.dot(qs[h], cts[h].astype(BF16), preferred_element_type=F32) for h in heads}
    lhs, a_bs = {}, {}
    for h in heads:
        sb, hd = h
        gt = gates[sb]
        a_row = ar_ref[sb, hd:hd + 1, :]
        a_bs[h] = lanes(col(gt["a4"], hd))
        a_full = jnp.concatenate([a_bs[h]] * (L // LANES), axis=1) if L >= LANES else a_bs[h][:, :L]
        w = jnp.exp(jnp.where(causal, a_row - a_full, -jnp.inf))
        s = (qk[h] * w).astype(BF16)
        kw = (kts[h].astype(F32) * jnp.exp(a_row - col(gt["a_last"], hd))).astype(BF16)
        lhs[h] = jnp.concatenate([s, kw], axis=0)
    pu = {}
    for h in heads:
        sb, hd = h
        v_aug = jnp.concatenate([v_ref[sb, :, hd * DV_B:(hd + 1) * DV_B], ones], axis=1)
        pu[h] = jnp.dot(lhs[h], v_aug, preferred_element_type=F32)
    ones_sum = jnp.ones((DV_B, LANES), BF16)
    for h in heads:
        sb, hd = h
        gt = gates[sb]
        a_b = a_bs[h]
        m_b = lanes(col(gt["m_row"], hd))
        big = jnp.maximum(a_b, m_b)
        r_b = jnp.exp(a_b - big)
        g_b = jnp.exp(m_b - big)
        e_b = jnp.exp(-(lanes(col(gt["b4"], hd)) + big))
        p, c = pu[h], qc[h]
        den = r_b * p[:L, DV_B:] + g_b * c[:, DV_B:]
        inv = 1.0 / jnp.maximum(jnp.abs(den), e_b)
        hh = jnp.concatenate([(r_b * p[:L, k * LANES:(k + 1) * LANES] + g_b * c[:, k * LANES:(k + 1) * LANES]) * inv
                              for k in range(DV_B // LANES)], axis=1)
        mu = jnp.dot(hh.astype(BF16), ones_sum, preferred_element_type=F32) * (1.0 / DV_B)
        hc = hh - jnp.concatenate([mu] * (DV_B // LANES), axis=1)
        var = jnp.dot(jnp.square(hc).astype(BF16), ones_sum, preferred_element_type=F32) * (1.0 / DV_B)
        rstd = lax.rsqrt(var + LN_EPS)
        yn = hc * jnp.concatenate([rstd] * (DV_B // LANES), axis=1) * nw_ref[:, hd * DV_B:(hd + 1) * DV_B]
        og = og_ref[sb, :, hd * DV_B:(hd + 1) * DV_B]
        y_ref[sb, :, hd * DV_B:(hd + 1) * DV_B] = (jax.nn.sigmoid(og) * yn).astype(BF16)
        ct_ref[sb, hd] = col(gt["decay4"], hd) * cts[h] + col(gt["rho4"], hd) * pu[h][L:]

    @pl.when(j == pl.num_programs(1) - 1)
    def _():
        for sb, hd in heads:
            c_ref[sb, hd] = ct_ref[sb, hd, :, :DV_B].T
            n_ref[sb, hd] = ct_ref[sb, hd, :, DV_B:DV_B + 1]


def _mlstm_call(qm, kt, vm, og, a_row, b_col, a_col, norm_w, c0, n0, m0, *, SB, L):
    N, T, _ = qm.shape

    def tok(c):
        return pl.BlockSpec((SB, L, c), lambda i, j: (i, j, 0))

    def tr(c):
        return pl.BlockSpec((SB, c, L), lambda i, j: (i, 0, j))

    st_c = pl.BlockSpec((SB, N_HEADS_B, DV_B, DQK_B), lambda i, j: (i, 0, 0, 0))
    st_n = pl.BlockSpec((SB, N_HEADS_B, DQK_B, 1), lambda i, j: (i, 0, 0, 0))
    st_m = pl.BlockSpec((SB, 1, LANES), lambda i, j: (i, 0, 0))
    return pl.pallas_call(
        functools.partial(_mlstm_kernel, SB=SB, L=L),
        grid=(N // SB, T // L),
        in_specs=[tok(QK_B), tr(QK_B), tok(V_B), tok(V_B), tr(8), tok(LANES), tok(LANES),
                  _const_spec((1, V_B)), st_c, st_n, st_m],
        out_specs=[tok(V_B), st_c, st_n, st_m],
        out_shape=[jax.ShapeDtypeStruct((N, T, V_B), BF16),
                   jax.ShapeDtypeStruct((N, N_HEADS_B, DV_B, DQK_B), F32),
                   jax.ShapeDtypeStruct((N, N_HEADS_B, DQK_B, 1), F32),
                   jax.ShapeDtypeStruct((N, 1, LANES), F32)],
        scratch_shapes=[pltpu.VMEM((SB, N_HEADS_B, DQK_B, DV_B + LANES), F32)],
        compiler_params=pltpu.CompilerParams(dimension_semantics=("parallel", "arbitrary")),
        name="mlstm",
    )(qm, kt, vm, og, a_row, b_col, a_col, norm_w, c0, n0, m0)


def _post_kernel(x_ref, mod_ref, ya_ref, yb_ref, ga_ref, gb_ref, cin_ref,
                 wpa_ref, wpb_ref, wo_ref, l1w_ref, l1b_ref, wup_ref, bup_ref, cw_ref, cb_ref,
                 wdn_ref, l2w_ref, l2b_ref,
                 y_ref, cout_ref, ctx_ref, act_ref, *, S, R):
    SR = S * R
    FC = D_FF // FFN_CHUNKS

    @pl.when(pl.program_id(1) == 0)
    def _():
        cout_ref[...] = cin_ref[...]

    g1 = mod_ref[:, 2:3, :]
    sh2 = mod_ref[:, 3:4, :]
    sc2 = mod_ref[:, 4:5, :]
    g2 = mod_ref[:, 5:6, :]
    x = x_ref[...]

    pa = jnp.dot(ya_ref[...].reshape(SR, Q_A), wpa_ref[...], preferred_element_type=F32)
    pb = jnp.dot(yb_ref[...].reshape(SR, V_B), wpb_ref[...], preferred_element_type=F32)
    merged = (jax.nn.sigmoid(ga_ref[...].reshape(SR, D_MODEL)) * pa
              + jax.nn.sigmoid(gb_ref[...].reshape(SR, D_MODEL)) * pb)
    mo = _bdot(merged, wo_ref[...]).reshape(S, R, D_MODEL)
    x1 = _layer_norm(ALPHA * x + g1 * mo, l1w_ref[...], l1b_ref[...])
    h2 = (x1 * (1.0 + sc2) + sh2).reshape(SR, D_MODEL).astype(BF16)

    def conv(lo):
        u = (jnp.dot(h2, wup_ref[:, lo:lo + FC], preferred_element_type=F32)
             + bup_ref[:, lo:lo + FC]).reshape(S, R, FC)
        ctx_ref[:, 6:8, :] = cout_ref[:, :, lo:lo + FC]
        ctx_ref[:, 8:, :] = u
        cout_ref[:, :, lo:lo + FC] = ctx_ref[:, R + 6:R + 8, :]
        return (cb_ref[:, lo:lo + FC]
                + cw_ref[0:1, lo:lo + FC] * ctx_ref[:, 6:R + 6, :]
                + cw_ref[1:2, lo:lo + FC] * ctx_ref[:, 7:R + 7, :]
                + cw_ref[2:3, lo:lo + FC] * u)

    for ci in range(FFN_CHUNKS):
        lo = ci * FC
        a = conv(lo)
        g = conv(D_FF + lo)
        act_ref[:, lo:lo + FC] = (a * jax.nn.gelu(g, approximate=True)).reshape(SR, FC).astype(BF16)

    f = jnp.dot(act_ref[...], wdn_ref[...], preferred_element_type=F32).reshape(S, R, D_MODEL)
    y_ref[...] = _layer_norm(ALPHA * x1 + g2 * f, l2w_ref[...], l2b_ref[...])


def _post_call(x, mod, ya, yb, ga, gb, conv_in, wpa, wpb, wo, l1w, l1b, wup, bup, cw, cb, wdn, l2w, l2b,
               *, S, R):
    N, T, _ = x.shape
    nsb, nrb = N // S, T // R
    FC = D_FF // FFN_CHUNKS

    def tok(c):
        return pl.BlockSpec((S, R, c), lambda i, j: (i, j, 0))

    conv_spec = pl.BlockSpec((S, CONV_W - 1, 2 * D_FF), lambda i, j: (i, 0, 0))
    return pl.pallas_call(
        functools.partial(_post_kernel, S=S, R=R),
        grid=(nsb, nrb),
        in_specs=[tok(D_MODEL), pl.BlockSpec((S, 6, D_MODEL), lambda i, j: (i, 0, 0)),
                  tok(Q_A), tok(V_B), tok(D_MODEL), tok(D_MODEL), conv_spec,
                  _const_spec((Q_A, D_MODEL)), _const_spec((V_B, D_MODEL)), _const_spec((D_MODEL, D_MODEL)),
                  _const_spec((1, D_MODEL)), _const_spec((1, D_MODEL)),
                  _const_spec((D_MODEL, 2 * D_FF)), _const_spec((1, 2 * D_FF)),
                  _const_spec((CONV_W, 2 * D_FF)), _const_spec((1, 2 * D_FF)),
                  _const_spec((D_FF, D_MODEL)), _const_spec((1, D_MODEL)), _const_spec((1, D_MODEL))],
        out_specs=[tok(D_MODEL), conv_spec],
        out_shape=[jax.ShapeDtypeStruct((N, T, D_MODEL), F32),
                   jax.ShapeDtypeStruct((N, CONV_W - 1, 2 * D_FF), F32)],
        scratch_shapes=[pltpu.VMEM((S, R + 8, FC), F32), pltpu.VMEM((S * R, D_FF), BF16)],
        compiler_params=pltpu.CompilerParams(dimension_semantics=("parallel", "arbitrary")),
        name="post",
    )(x, mod, ya, yb, ga, gb, conv_in, wpa, wpb, wo, l1w, l1b, wup, bup, cw, cb, wdn, l2w, l2b)


def kernel(x_prompt, x_sample, cache_attn_k, cache_attn_v, state_mlstm_C, state_mlstm_n, state_mlstm_m, state_ffn_conv, c_prompt, c_sample, w_ada, b_ada, w_in, b_igate, b_fgate, attn_sinks, mlstm_norm_w, w_proj_a, w_proj_b, w_out, ln1_w, ln1_b, w_up, b_up, conv_w, conv_b, w_down, ln2_w, ln2_b):
    Bp, Tp, D = x_prompt.shape
    Bs, Ts, _ = x_sample.shape
    l = 0

    mod = _mod_call(jnp.concatenate([c_prompt, c_sample], axis=0), w_ada[l], b_ada[l][None])
    mod_p = mod[:Bp].reshape(Bp, 6, D)
    mod_s = mod[Bp:].reshape(Bs, 6, D)

    w = w_in[l]
    o_i, o_f, o_o = Z_OFF[6], Z_OFF[7], Z_OFF[8]
    lane_pad = ((0, 0), (0, LANES - N_HEADS_B))
    w_main = jnp.concatenate([w[:, :o_i], w[:, o_o:],
                              jnp.pad(w[:, o_i:o_f], lane_pad), jnp.pad(w[:, o_f:o_o], lane_pad)],
                             axis=1).astype(BF16)
    b_ic = jnp.pad(b_igate[l][None], lane_pad)
    b_fc = jnp.pad(b_fgate[l][None], lane_pad)
    inproj_w = (w_main, b_ic, b_fc)
    sinks = attn_sinks[l]
    norm_w = mlstm_norm_w[l][None]
    post_w = (w_proj_a[l].astype(BF16), w_proj_b[l].astype(BF16), w_out[l].astype(BF16),
              ln1_w[l][None], ln1_b[l][None], w_up[l].astype(BF16), b_up[l][None], conv_w[l], conv_b[l][None],
              w_down[l].astype(BF16), ln2_w[l][None], ln2_b[l][None])

    qa, ka, va, qm, kt, vm, og, ga, gb, a_row, b_col, a_col = _inproj_call(
        x_prompt, mod_p, *inproj_w, S=1, R=INPROJ_ROWS, L=MLSTM_CHUNK, q_transposed=True)
    ya = _attn_prompt_call(qa, ka, va, sinks)
    yb, p_c, p_n, p_m = _mlstm_call(
        qm, kt, vm, og, a_row, b_col, a_col, norm_w,
        jnp.zeros((Bp, N_HEADS_B, DV_B, DQK_B), F32), jnp.zeros((Bp, N_HEADS_B, DQK_B, 1), F32),
        jnp.zeros((Bp, 1, LANES), F32), SB=4, L=MLSTM_CHUNK)
    p_m = p_m[:, 0, :N_HEADS_B]
    y_prompt, p_conv = _post_call(x_prompt, mod_p, ya, yb, ga, gb,
                                  jnp.zeros((Bp, CONV_W - 1, 2 * D_FF), F32), *post_w, S=1, R=256)
    p_k = ka[:, Tp - KV_WIN:].reshape(Bp, KV_WIN, N_KV_A, HEAD_DIM_A)
    p_v = va[:, Tp - KV_WIN:].reshape(Bp, KV_WIN, N_KV_A, HEAD_DIM_A)

    qa, ka, va, qm, kt, vm, og, ga, gb, a_row, b_col, a_col = _inproj_call(
        x_sample, mod_s, *inproj_w, S=16, R=Ts, L=Ts, q_transposed=False)
    ya, s_k, s_v = _attn_sample_call(qa, ka, va, cache_attn_k[l].reshape(Bs, KV_WIN, KV_A_W),
                                     cache_attn_v[l].reshape(Bs, KV_WIN, KV_A_W), sinks, NS=8)
    yb, s_c, s_n, s_m = _mlstm_call(
        qm, kt, vm, og, a_row, b_col, a_col, norm_w,
        state_mlstm_C[l], state_mlstm_n[l][:, :, :, None], jnp.pad(state_mlstm_m[l][:, None, :], ((0, 0),) + lane_pad),
        SB=4, L=Ts)
    s_m = s_m[:, 0, :N_HEADS_B]
    y_sample, s_conv = _post_call(x_sample, mod_s, ya, yb, ga, gb, state_ffn_conv[l], *post_w, S=8, R=Ts)

    def st(a, shape):
        return a.reshape((1,) + shape)

    return (y_prompt, y_sample,
            st(p_k, (Bp, KV_WIN, N_KV_A, HEAD_DIM_A)), st(p_v, (Bp, KV_WIN, N_KV_A, HEAD_DIM_A)),
            st(p_c, (Bp, N_HEADS_B, DV_B, DQK_B)), st(p_n, (Bp, N_HEADS_B, DQK_B)), st(p_m, (Bp, N_HEADS_B)),
            st(p_conv, (Bp, CONV_W - 1, 2 * D_FF)),
            st(s_k, (Bs, KV_WIN, N_KV_A, HEAD_DIM_A)), st(s_v, (Bs, KV_WIN, N_KV_A, HEAD_DIM_A)),
            st(s_c, (Bs, N_HEADS_B, DV_B, DQK_B)), st(s_n, (Bs, N_HEADS_B, DQK_B)), st(s_m, (Bs, N_HEADS_B)),
            st(s_conv, (Bs, CONV_W - 1, 2 * D_FF)))
```

```python
import functools

import numpy as np
import jax
import jax.numpy as jnp
from jax import lax
from jax.experimental import pallas as pl
from jax.experimental.pallas import tpu as pltpu

F32 = jnp.float32
BF16 = jnp.bfloat16

D_MODEL = 1024
DEPTH = 1
PAST_LEN = 2048
CHUNK = 64
N_HEADS_A = 16
N_KV_A = 2
HEAD_DIM_A = 64
GROUP_A = N_HEADS_A // N_KV_A
WINDOW = 128
WINDOW_CHUNKS = WINDOW // CHUNK
KV_WIN = min(WINDOW, PAST_LEN)
N_HEADS_B = 4
DQK_B = 128
DV_B = 256
D_FF = 2816
CONV_W = 3
LN_EPS = 1e-5
ALPHA = (2 * DEPTH) ** 0.25
Q_A = N_HEADS_A * HEAD_DIM_A
KV_A_W = N_KV_A * HEAD_DIM_A
QK_B = N_HEADS_B * DQK_B
V_B = N_HEADS_B * DV_B
Z_PARTS = (Q_A, KV_A_W, KV_A_W, QK_B, QK_B, V_B, N_HEADS_B, N_HEADS_B, V_B, D_MODEL, D_MODEL)
Z_OFF = tuple(int(v) for v in np.cumsum((0,) + Z_PARTS))

LANES = 128
LOG2E = float(np.log2(np.e))

M_QA = 0
M_KA = M_QA + Q_A
M_VA = M_KA + KV_A_W
M_QM = M_VA + KV_A_W
M_KM = M_QM + QK_B
M_VM = M_KM + QK_B
M_OG = M_VM + V_B
M_GA = M_OG + V_B
M_GB = M_GA + D_MODEL
M_GI = M_GB + D_MODEL
M_GF = M_GI + LANES
M_END = M_GF + LANES
ATT_QB = 128
ATT_SUB = 8
ATT_W = 256
FFN_CHUNKS = 11
MLSTM_CHUNK = 256
INPROJ_ROWS = 512


def _bdot(a, b):
    return jnp.dot(a.astype(BF16), b.astype(BF16), preferred_element_type=F32)


def _bdot_nt(a, b):
    return lax.dot_general(a.astype(BF16), b.astype(BF16), (((1,), (1,)), ((), ())),
                           preferred_element_type=F32)


def _const_spec(shape):
    nd = len(shape)
    return pl.BlockSpec(shape, lambda *_: (0,) * nd, pipeline_mode=pl.Buffered(1))


def _layer_norm(x, w, b):
    mu = jnp.mean(x, -1, keepdims=True)
    xc = x - mu
    var = jnp.mean(jnp.square(xc), -1, keepdims=True)
    return xc * lax.rsqrt(var + LN_EPS) * w + b


def _mod_kernel(c_ref, w_ref, b_ref, o_ref):
    c = c_ref[...]
    s = c * jax.nn.sigmoid(c)
    o_ref[...] = _bdot(s, w_ref[...]) + b_ref[...]


def _mod_call(c_all, w_ada, b_ada):
    n = c_all.shape[0]
    tn = 1536
    return pl.pallas_call(
        _mod_kernel,
        grid=(6 * D_MODEL // tn,),
        in_specs=[pl.BlockSpec((n, D_MODEL), lambda j: (0, 0)),
                  pl.BlockSpec((D_MODEL, tn), lambda j: (0, j)),
                  pl.BlockSpec((1, tn), lambda j: (0, j))],
        out_specs=pl.BlockSpec((n, tn), lambda j: (0, j)),
        out_shape=jax.ShapeDtypeStruct((n, 6 * D_MODEL), F32),
        name="mod",
    )(c_all, w_ada, b_ada)


def _log_sigmoid(x):
    return jnp.minimum(x, 0.0) - jnp.log1p(jnp.exp(-jnp.abs(x)))


def _seg_scan(x, L, axis, op, ident):
    pos = lax.broadcasted_iota(jnp.int32, x.shape, axis) % L
    step = 1
    while step < L:
        x = op(x, jnp.where(pos >= step, pltpu.roll(x, step, axis=axis), ident))
        step *= 2
    return x


def _inproj_kernel(x_ref, mod_ref, w_ref, bic_ref, bfc_ref,
                   qa_ref, ka_ref, va_ref, qm_ref, kt_ref, vm_ref, og_ref, ga_ref, gb_ref,
                   ar_ref, bc_ref, ac_ref, *, S, R, L, q_transposed):
    SR = S * R
    sh = mod_ref[:, 0:1, :]
    sc = mod_ref[:, 1:2, :]
    h = (x_ref[...] * (1.0 + sc) + sh).reshape(SR, D_MODEL).astype(BF16)

    def proj(lo, hi):
        return jnp.dot(h, w_ref[:, lo:hi], preferred_element_type=F32)

    q = proj(M_QA, M_KA) * (HEAD_DIM_A ** -0.5)
    if q_transposed:
        qt = (q * LOG2E).T.astype(BF16)
        for s in range(S):
            qa_ref[s] = qt[:, s * R:(s + 1) * R]
    else:
        qa_ref[...] = q.astype(BF16).reshape(S, R, Q_A)
    kv = proj(M_KA, M_QM)
    ka_ref[...] = kv[:, :KV_A_W].reshape(S, R, KV_A_W)
    va_ref[...] = kv[:, KV_A_W:].reshape(S, R, KV_A_W)
    qm_ref[...] = proj(M_QM, M_KM).astype(BF16).reshape(S, R, QK_B)
    kt = (proj(M_KM, M_VM) * (DQK_B ** -0.5)).T.astype(BF16)
    vm_ref[...] = proj(M_VM, M_OG).astype(BF16).reshape(S, R, V_B)
    og_ref[...] = proj(M_OG, M_GA).reshape(S, R, V_B)
    ga_ref[...] = proj(M_GA, M_GB).reshape(S, R, D_MODEL)
    gb_ref[...] = proj(M_GB, M_GI).reshape(S, R, D_MODEL)

    gates = proj(M_GI, M_END)
    i_col = gates[:, :LANES] + bic_ref[...]
    logf_col = _log_sigmoid(gates[:, LANES:] + bfc_ref[...])
    b_col = _seg_scan(logf_col, L, 0, jnp.add, 0.0)
    a_col = _seg_scan(i_col - b_col, L, 0, jnp.maximum, -jnp.inf)
    bc_ref[...] = b_col.reshape(S, R, LANES)
    ac_ref[...] = a_col.reshape(S, R, LANES)
    i_row = i_col.T[:8]
    logf_row = logf_col.T[:8]
    a_row = i_row - _seg_scan(logf_row, L, 1, jnp.add, 0.0)
    for s in range(S):
        kt_ref[s] = kt[:, s * R:(s + 1) * R]
        ar_ref[s] = a_row[:, s * R:(s + 1) * R]


def _inproj_call(x, mod, w_main, b_ic, b_fc, *, S, R, L, q_transposed):
    N, T, _ = x.shape
    nsb, nrb = N // S, T // R

    def tok(c):
        return pl.BlockSpec((S, R, c), lambda i, j: (i, j, 0))

    def tok_shape(c, dt):
        return jax.ShapeDtypeStruct((N, T, c), dt)

    def tr(c):
        return pl.BlockSpec((S, c, R), lambda i, j: (i, 0, j))

    return pl.pallas_call(
        functools.partial(_inproj_kernel, S=S, R=R, L=L, q_transposed=q_transposed),
        grid=(nsb, nrb),
        in_specs=[tok(D_MODEL),
                  pl.BlockSpec((S, 6, D_MODEL), lambda i, j: (i, 0, 0)),
                  _const_spec((D_MODEL, M_END)),
                  _const_spec((1, LANES)), _const_spec((1, LANES))],
        out_specs=[tr(Q_A) if q_transposed else tok(Q_A), tok(KV_A_W), tok(KV_A_W), tok(QK_B), tr(QK_B), tok(V_B),
                   tok(V_B), tok(D_MODEL), tok(D_MODEL), tr(8), tok(LANES), tok(LANES)],
        out_shape=[jax.ShapeDtypeStruct((N, Q_A, T), BF16) if q_transposed else tok_shape(Q_A, BF16),
                   tok_shape(KV_A_W, F32), tok_shape(KV_A_W, F32),
                   tok_shape(QK_B, BF16), jax.ShapeDtypeStruct((N, QK_B, T), BF16), tok_shape(V_B, BF16),
                   tok_shape(V_B, F32), tok_shape(D_MODEL, F32), tok_shape(D_MODEL, F32),
                   jax.ShapeDtypeStruct((N, 8, T), F32), tok_shape(LANES, F32), tok_shape(LANES, F32)],
        compiler_params=pltpu.CompilerParams(dimension_semantics=("parallel", "parallel")),
        name="inproj",
    )(x, mod, w_main, b_ic, b_fc)


def _alibi_slopes():
    return 2.0 ** (-8.0 * np.arange(1, N_HEADS_A + 1, dtype=np.float64) / N_HEADS_A)


def _attn_bias(qpos, kpos, kvalid, transposed=False):
    qpos = np.asarray(qpos)[:, None]
    kpos = np.asarray(kpos)[None, :]
    qc, kc = qpos // CHUNK, kpos // CHUNK
    visible = (kpos >= 0) & (kc <= qc) & (kc >= qc - WINDOW_CHUNKS) & np.asarray(kvalid)[None, :]
    dist = np.abs(qpos - kpos).astype(np.float64)
    slopes = _alibi_slopes()
    out = np.zeros((N_KV_A, GROUP_A // 2, qpos.shape[0], 2, kpos.shape[1]), np.float32)
    for kv in range(N_KV_A):
        for p in range(GROUP_A // 2):
            for half in range(2):
                hd = kv * GROUP_A + 2 * p + half
                out[kv, p, :, half, :] = np.where(visible, -slopes[hd] * dist, -np.inf)
    if transposed:
        return out.transpose(0, 3, 4, 1, 2).reshape(N_KV_A, 2 * kpos.shape[1], (GROUP_A // 2) * qpos.shape[0])
    return out.reshape(N_KV_A, (GROUP_A // 2) * qpos.shape[0], 2 * kpos.shape[1])


def _attn_core(q, kwin, vwin, bias_of, sinks_ref, store):
    Mq = q.shape[0]
    W = kwin.shape[0]
    npair = GROUP_A // 2
    lane = lax.broadcasted_iota(jnp.int32, (npair * Mq, LANES), 1)
    for kv in range(N_KV_A):
        kk = kwin[:, kv * HEAD_DIM_A:(kv + 1) * HEAD_DIM_A].astype(BF16)
        vv = vwin[:, kv * HEAD_DIM_A:(kv + 1) * HEAD_DIM_A].astype(BF16)
        zero = jnp.zeros_like(kk)
        one = jnp.ones_like(vv)
        k2 = jnp.concatenate([jnp.concatenate([kk, zero], axis=1),
                              jnp.concatenate([zero, kk], axis=1)], axis=0)
        v_aug = jnp.concatenate([vv, one, one, vv], axis=1)
        qs = jnp.concatenate([q[:, (kv * npair + p) * LANES:(kv * npair + p + 1) * LANES]
                              for p in range(npair)], axis=0)
        s = _bdot_nt(qs, k2) + bias_of(kv)
        probs, sink_terms = [], []
        for half in range(2):
            sh = s[:, half * W:(half + 1) * W]
            sink = jnp.concatenate(
                [jnp.full((Mq, 1), sinks_ref[kv * GROUP_A + 2 * p + half], F32) for p in range(npair)], axis=0)
            mx = jnp.maximum(jnp.max(sh, axis=1, keepdims=True), sink)
            probs.append(jnp.exp(sh - mx).astype(BF16))
            sink_terms.append(jnp.exp(sink - mx))
        o = jnp.dot(jnp.concatenate(probs, axis=0), v_aug, preferred_element_type=F32)
        oe, oo = o[:npair * Mq], o[npair * Mq:]
        ye = oe[:, :LANES] / (oe[:, LANES:] + sink_terms[0])
        yo = oo[:, LANES:] / (oo[:, :LANES] + sink_terms[1])
        y = jnp.where(lane < HEAD_DIM_A, ye, yo).astype(BF16)
        for p in range(npair):
            store((kv * npair + p) * LANES, y[p * Mq:(p + 1) * Mq])


def _attn_prompt_kernel(sinks_ref, qt_ref, kp_ref, kc_ref, vp_ref, vc_ref, bias0_ref, bias1_ref, o_ref):
    Mq = ATT_QB
    npair = GROUP_A // 2
    keys = jnp.concatenate([kp_ref[0], kc_ref[0]], axis=0)
    vals_t = jnp.concatenate([vp_ref[0], vc_ref[0]], axis=0).T.astype(BF16)
    ones = jnp.ones((HEAD_DIM_A, ATT_W), BF16)
    pair_of_lane = lax.broadcasted_iota(jnp.int32, (1, npair * Mq), 1) // Mq
    units = [(sub, kv) for sub in range(ATT_SUB) for kv in range(N_KV_A)]
    sts = {}
    for sub, kv in units:
        kk = keys[sub * Mq:sub * Mq + ATT_W, kv * HEAD_DIM_A:(kv + 1) * HEAD_DIM_A].astype(BF16)
        zero = jnp.zeros_like(kk)
        k2 = jnp.concatenate([jnp.concatenate([kk, zero], axis=1),
                              jnp.concatenate([zero, kk], axis=1)], axis=0)
        qt = jnp.concatenate([qt_ref[0, (kv * npair + p) * LANES:(kv * npair + p + 1) * LANES, sub * Mq:(sub + 1) * Mq]
                              for p in range(npair)], axis=1)
        sts[sub, kv] = jnp.dot(k2, qt, preferred_element_type=F32)
    probs, sink_terms = {}, {}
    for sub, kv in units:
        st = sts[sub, kv] + (bias0_ref[0, kv] if sub == 0 else bias1_ref[0, kv])
        for half in range(2):
            sh = st[half * ATT_W:(half + 1) * ATT_W]
            sink = jnp.zeros((1, npair * Mq), F32)
            for p in range(npair):
                sink = jnp.where(pair_of_lane == p, sinks_ref[kv * GROUP_A + 2 * p + half] * LOG2E, sink)
            mx = jnp.maximum(jnp.max(sh, axis=0, keepdims=True), sink)
            probs[sub, kv, half] = jnp.exp2(sh - mx).astype(BF16)
            sink_terms[sub, kv, half] = jnp.exp2(sink - mx)
    ots = {}
    for sub, kv in units:
        vt = vals_t[kv * HEAD_DIM_A:(kv + 1) * HEAD_DIM_A, sub * Mq:sub * Mq + ATT_W]
        for half in range(2):
            lhs = jnp.concatenate([vt, ones] if half == 0 else [ones, vt], axis=0)
            ots[sub, kv, half] = jnp.dot(lhs, probs[sub, kv, half], preferred_element_type=F32)
    for sub, kv in units:
        oe, oo = ots[sub, kv, 0], ots[sub, kv, 1]
        ye = oe[:HEAD_DIM_A] / (oe[HEAD_DIM_A:] + sink_terms[sub, kv, 0])
        yo = oo[HEAD_DIM_A:] / (oo[:HEAD_DIM_A] + sink_terms[sub, kv, 1])
        yt = jnp.concatenate([ye, yo], axis=0)
        for p in range(npair):
            col = (kv * npair + p) * LANES
            o_ref[0, sub * Mq:(sub + 1) * Mq, col:col + LANES] = yt[:, p * Mq:(p + 1) * Mq].T.astype(BF16)


def _attn_prompt_call(qa_t, ka, va, sinks):
    N, _, T = qa_t.shape
    rows = ATT_QB * ATT_SUB
    rel_q = ATT_QB + np.arange(ATT_QB)
    rel_k = np.arange(ATT_W)
    bias = np.stack([_attn_bias(rel_q, rel_k, rel_k >= ATT_QB, transposed=True),
                     _attn_bias(rel_q, rel_k, rel_k >= 0, transposed=True)]) * np.float32(LOG2E)
    kv_prev = pl.BlockSpec((1, ATT_QB, KV_A_W), lambda n, j: (n, jnp.maximum(ATT_SUB * j - 1, 0), 0))
    kv_cur = pl.BlockSpec((1, rows, KV_A_W), lambda n, j: (n, j, 0))
    return pl.pallas_call(
        _attn_prompt_kernel,
        grid=(N, T // rows),
        in_specs=[pl.BlockSpec(memory_space=pltpu.SMEM),
                  pl.BlockSpec((1, Q_A, rows), lambda n, j: (n, 0, j)),
                  kv_prev, kv_cur, kv_prev, kv_cur,
                  pl.BlockSpec((1,) + bias.shape[1:], lambda n, j: (jnp.minimum(j, 1), 0, 0, 0)),
                  pl.BlockSpec((1,) + bias.shape[1:], lambda n, j: (1, 0, 0, 0), pipeline_mode=pl.Buffered(1))],
        out_specs=pl.BlockSpec((1, rows, Q_A), lambda n, j: (n, j, 0)),
        out_shape=jax.ShapeDtypeStruct((N, T, Q_A), BF16),
        compiler_params=pltpu.CompilerParams(dimension_semantics=("parallel", "parallel")),
        name="attn_prompt",
    )(sinks, qa_t, ka, ka, va, va, jnp.asarray(bias), jnp.asarray(bias))


def _attn_sample_kernel(sinks_ref, q_ref, kn_ref, vn_ref, kc_ref, vc_ref, bias_ref,
                        o_ref, ko_ref, vo_ref, *, T, NS):
    pad = jnp.zeros((ATT_W - KV_WIN - T, KV_A_W), F32)
    for s in range(NS):
        kwin = jnp.concatenate([kc_ref[s], kn_ref[s], pad], axis=0)
        vwin = jnp.concatenate([vc_ref[s], vn_ref[s], pad], axis=0)
        ko_ref[s] = kwin[T:T + KV_WIN]
        vo_ref[s] = vwin[T:T + KV_WIN]

        def store(col, val, s=s):
            o_ref[s, :, col:col + LANES] = val

        _attn_core(q_ref[s], kwin, vwin, lambda kv: bias_ref[kv], sinks_ref, store)


def _attn_sample_call(qa, ka, va, cache_k, cache_v, sinks, *, NS):
    N, T, _ = qa.shape
    qpos = PAST_LEN + np.arange(T)
    kpos = PAST_LEN - KV_WIN + np.arange(ATT_W)
    bias = _attn_bias(qpos, kpos, np.arange(ATT_W) < KV_WIN + T)
    new = pl.BlockSpec((NS, T, KV_A_W), lambda n: (n, 0, 0))
    cache = pl.BlockSpec((NS, KV_WIN, KV_A_W), lambda n: (n, 0, 0))
    cache_shape = jax.ShapeDtypeStruct((N, KV_WIN, KV_A_W), F32)
    return pl.pallas_call(
        functools.partial(_attn_sample_kernel, T=T, NS=NS),
        grid=(N // NS,),
        in_specs=[pl.BlockSpec(memory_space=pltpu.SMEM),
                  pl.BlockSpec((NS, T, Q_A), lambda n: (n, 0, 0)),
                  new, new, cache, cache,
                  _const_spec(bias.shape)],
        out_specs=[pl.BlockSpec((NS, T, Q_A), lambda n: (n, 0, 0)), cache, cache],
        out_shape=[jax.ShapeDtypeStruct((N, T, Q_A), BF16), cache_shape, cache_shape],
        compiler_params=pltpu.CompilerParams(dimension_semantics=("parallel",)),
        name="attn_sample",
    )(sinks, qa, ka, va, cache_k, cache_v, jnp.asarray(bias))


def _mlstm_kernel(q_ref, kt_ref, v_ref, og_ref, ar_ref, bc_ref, ac_ref, nw_ref, c0_ref, n0_ref, m0_ref,
                  y_ref, c_ref, n_ref, m_ref, ct_ref, *, SB, L):
    j = pl.program_id(1)
    heads = [(sb, hd) for sb in range(SB) for hd in range(N_HEADS_B)]

    @pl.when(j == 0)
    def _():
        for sb, hd in heads:
            ct_ref[sb, hd, :, :DV_B] = c0_ref[sb, hd].T
            ct_ref[sb, hd, :, DV_B:] = jnp.broadcast_to(n0_ref[sb, hd], (DQK_B, LANES))
        m_ref[...] = m0_ref[...]

    causal = (lax.broadcasted_iota(jnp.int32, (L, L), 0) >= lax.broadcasted_iota(jnp.int32, (L, L), 1))
    ones = jnp.ones((L, LANES), BF16)
    gates = []
    for sb in range(SB):
        a4 = ac_ref[sb]
        b4 = bc_ref[sb]
        m_row = m_ref[sb]
        a_last = a4[L - 1:L]
        b_last = b4[L - 1:L]
        mx = jnp.maximum(a_last, m_row)
        m_new = b_last + mx
        gates.append(dict(a4=a4, b4=b4, a_last=a_last, m_row=m_row, decay4=jnp.exp(b_last + m_row - m_new),
                          rho4=jnp.exp(a_last - mx)))
        m_ref[sb] = m_new

    def lanes(x):
        return jnp.broadcast_to(x, (x.shape[0], LANES))

    def col(x, hd):
        return x[:, hd:hd + 1]

    qs = {h: q_ref[h[0], :, h[1] * DQK_B:(h[1] + 1) * DQK_B] for h in heads}
    kts = {h: kt_ref[h[0], h[1] * DQK_B:(h[1] + 1) * DQK_B, :] for h in heads}
    cts = {h: ct_ref[h[0], h[1]] for h in heads}
    qk = {h: jnp.dot(qs[h], kts[h], preferred_element_type=F32) for h in heads}
    qc = {h: jnp.dot(qs[h], cts[h].astype(BF16), preferred_element_type=F32) for h in heads}
    lhs, a_bs = {}, {}
    for h in heads:
        sb, hd = h
        gt = gates[sb]
        a_row = ar_ref[sb, hd:hd + 1, :]
        a_bs[h] = lanes(col(gt["a4"], hd))
        a_full = jnp.concatenate([a_bs[h]] * (L // LANES), axis=1) if L >= LANES else a_bs[h][:, :L]
        w = jnp.exp(jnp.where(causal, a_row - a_full, -jnp.inf))
        s = (qk[h] * w).astype(BF16)
        kw = (kts[h].astype(F32) * jnp.exp(a_row - col(gt["a_last"], hd))).astype(BF16)
        lhs[h] = jnp.concatenate([s, kw], axis=0)
    pu = {}
    for h in heads:
        sb, hd = h
        v_aug = jnp.concatenate([v_ref[sb, :, hd * DV_B:(hd + 1) * DV_B], ones], axis=1)
        pu[h] = jnp.dot(lhs[h], v_aug, preferred_element_type=F32)
    ones_sum = jnp.ones((DV_B, LANES), BF16)
    for h in heads:
        sb, hd = h
        gt = gates[sb]
        a_b = a_bs[h]
        m_b = lanes(col(gt["m_row"], hd))
        big = jnp.maximum(a_b, m_b)
        r_b = jnp.exp(a_b - big)
        g_b = jnp.exp(m_b - big)
        e_b = jnp.exp(-(lanes(col(gt["b4"], hd)) + big))
        p, c = pu[h], qc[h]
        den = r_b * p[:L, DV_B:] + g_b * c[:, DV_B:]
        inv = 1.0 / jnp.maximum(jnp.abs(den), e_b)
        hh = jnp.concatenate([(r_b * p[:L, k * LANES:(k + 1) * LANES] + g_b * c[:, k * LANES:(k + 1) * LANES]) * inv
                              for k in range(DV_B // LANES)], axis=1)
        mu = jnp.dot(hh.astype(BF16), ones_sum, preferred_element_type=F32) * (1.0 / DV_B)
        hc = hh - jnp.concatenate([mu] * (DV_B // LANES), axis=1)
        var = jnp.dot(jnp.square(hc).astype(BF16), ones_sum, preferred_element_type=F32) * (1.0 / DV_B)
        rstd = lax.rsqrt(var + LN_EPS)
        yn = hc * jnp.concatenate([rstd] * (DV_B // LANES), axis=1) * nw_ref[:, hd * DV_B:(hd + 1) * DV_B]
        og = og_ref[sb, :, hd * DV_B:(hd + 1) * DV_B]
        y_ref[sb, :, hd * DV_B:(hd + 1) * DV_B] = (jax.nn.sigmoid(og) * yn).astype(BF16)
        ct_ref[sb, hd] = col(gt["decay4"], hd) * cts[h] + col(gt["rho4"], hd) * pu[h][L:]

    @pl.when(j == pl.num_programs(1) - 1)
    def _():
        for sb, hd in heads:
            c_ref[sb, hd] = ct_ref[sb, hd, :, :DV_B].T
            n_ref[sb, hd] = ct_ref[sb, hd, :, DV_B:DV_B + 1]


def _mlstm_call(qm, kt, vm, og, a_row, b_col, a_col, norm_w, c0, n0, m0, *, SB, L):
    N, T, _ = qm.shape

    def tok(c):
        return pl.BlockSpec((SB, L, c), lambda i, j: (i, j, 0))

    def tr(c):
        return pl.BlockSpec((SB, c, L), lambda i, j: (i, 0, j))

    st_c = pl.BlockSpec((SB, N_HEADS_B, DV_B, DQK_B), lambda i, j: (i, 0, 0, 0))
    st_n = pl.BlockSpec((SB, N_HEADS_B, DQK_B, 1), lambda i, j: (i, 0, 0, 0))
    st_m = pl.BlockSpec((SB, 1, LANES), lambda i, j: (i, 0, 0))
    return pl.pallas_call(
        functools.partial(_mlstm_kernel, SB=SB, L=L),
        grid=(N // SB, T // L),
        in_specs=[tok(QK_B), tr(QK_B), tok(V_B), tok(V_B), tr(8), tok(LANES), tok(LANES),
                  _const_spec((1, V_B)), st_c, st_n, st_m],
        out_specs=[tok(V_B), st_c, st_n, st_m],
        out_shape=[jax.ShapeDtypeStruct((N, T, V_B), BF16),
                   jax.ShapeDtypeStruct((N, N_HEADS_B, DV_B, DQK_B), F32),
                   jax.ShapeDtypeStruct((N, N_HEADS_B, DQK_B, 1), F32),
                   jax.ShapeDtypeStruct((N, 1, LANES), F32)],
        scratch_shapes=[pltpu.VMEM((SB, N_HEADS_B, DQK_B, DV_B + LANES), F32)],
        compiler_params=pltpu.CompilerParams(dimension_semantics=("parallel", "arbitrary")),
        name="mlstm",
    )(qm, kt, vm, og, a_row, b_col, a_col, norm_w, c0, n0, m0)


def _post_kernel(x_ref, mod_ref, ya_ref, yb_ref, ga_ref, gb_ref, cin_ref,
                 wpa_ref, wpb_ref, wo_ref, l1w_ref, l1b_ref, wup_ref, bup_ref, cw_ref, cb_ref,
                 wdn_ref, l2w_ref, l2b_ref,
                 y_ref, cout_ref, ctx_ref, act_ref, *, S, R):
    SR = S * R
    FC = D_FF // FFN_CHUNKS

    @pl.when(pl.program_id(1) == 0)
    def _():
        cout_ref[...] = cin_ref[...]

    g1 = mod_ref[:, 2:3, :]
    sh2 = mod_ref[:, 3:4, :]
    sc2 = mod_ref[:, 4:5, :]
    g2 = mod_ref[:, 5:6, :]
    x = x_ref[...]

    pa = jnp.dot(ya_ref[...].reshape(SR, Q_A), wpa_ref[...], preferred_element_type=F32)
    pb = jnp.dot(yb_ref[...].reshape(SR, V_B), wpb_ref[...], preferred_element_type=F32)
    merged = (jax.nn.sigmoid(ga_ref[...].reshape(SR, D_MODEL)) * pa
              + jax.nn.sigmoid(gb_ref[...].reshape(SR, D_MODEL)) * pb)
    mo = _bdot(merged, wo_ref[...]).reshape(S, R, D_MODEL)
    x1 = _layer_norm(ALPHA * x + g1 * mo, l1w_ref[...], l1b_ref[...])
    h2 = (x1 * (1.0 + sc2) + sh2).reshape(SR, D_MODEL).astype(BF16)

    def conv(lo):
        u = (jnp.dot(h2, wup_ref[:, lo:lo + FC], preferred_element_type=F32)
             + bup_ref[:, lo:lo + FC]).reshape(S, R, FC)
        ctx_ref[:, 6:8, :] = cout_ref[:, :, lo:lo + FC]
        ctx_ref[:, 8:, :] = u
        cout_ref[:, :, lo:lo + FC] = ctx_ref[:, R + 6:R + 8, :]
        return (cb_ref[:, lo:lo + FC]
                + cw_ref[0:1, lo:lo + FC] * ctx_ref[:, 6:R + 6, :]
                + cw_ref[1:2, lo:lo + FC] * ctx_ref[:, 7:R + 7, :]
                + cw_ref[2:3, lo:lo + FC] * u)

    for ci in range(FFN_CHUNKS):
        lo = ci * FC
        a = conv(lo)
        g = conv(D_FF + lo)
        act_ref[:, lo:lo + FC] = (a * jax.nn.gelu(g, approximate=True)).reshape(SR, FC).astype(BF16)

    f = jnp.dot(act_ref[...], wdn_ref[...], preferred_element_type=F32).reshape(S, R, D_MODEL)
    y_ref[...] = _layer_norm(ALPHA * x1 + g2 * f, l2w_ref[...], l2b_ref[...])


def _post_call(x, mod, ya, yb, ga, gb, conv_in, wpa, wpb, wo, l1w, l1b, wup, bup, cw, cb, wdn, l2w, l2b,
               *, S, R):
    N, T, _ = x.shape
    nsb, nrb = N // S, T // R
    FC = D_FF // FFN_CHUNKS

    def tok(c):
        return pl.BlockSpec((S, R, c), lambda i, j: (i, j, 0))

    conv_spec = pl.BlockSpec((S, CONV_W - 1, 2 * D_FF), lambda i, j: (i, 0, 0))
    return pl.pallas_call(
        functools.partial(_post_kernel, S=S, R=R),
        grid=(nsb, nrb),
        in_specs=[tok(D_MODEL), pl.BlockSpec((S, 6, D_MODEL), lambda i, j: (i, 0, 0)),
                  tok(Q_A), tok(V_B), tok(D_MODEL), tok(D_MODEL), conv_spec,
                  _const_spec((Q_A, D_MODEL)), _const_spec((V_B, D_MODEL)), _const_spec((D_MODEL, D_MODEL)),
                  _const_spec((1, D_MODEL)), _const_spec((1, D_MODEL)),
                  _const_spec((D_MODEL, 2 * D_FF)), _const_spec((1, 2 * D_FF)),
                  _const_spec((CONV_W, 2 * D_FF)), _const_spec((1, 2 * D_FF)),
                  _const_spec((D_FF, D_MODEL)), _const_spec((1, D_MODEL)), _const_spec((1, D_MODEL))],
        out_specs=[tok(D_MODEL), conv_spec],
        out_shape=[jax.ShapeDtypeStruct((N, T, D_MODEL), F32),
                   jax.ShapeDtypeStruct((N, CONV_W - 1, 2 * D_FF), F32)],
        scratch_shapes=[pltpu.VMEM((S, R + 8, FC), F32), pltpu.VMEM((S * R, D_FF), BF16)],
        compiler_params=pltpu.CompilerParams(dimension_semantics=("parallel", "arbitrary")),
        name="post",
    )(x, mod, ya, yb, ga, gb, conv_in, wpa, wpb, wo, l1w, l1b, wup, bup, cw, cb, wdn, l2w, l2b)


def kernel(x_prompt, x_sample, cache_attn_k, cache_attn_v, state_mlstm_C, state_mlstm_n, state_mlstm_m, state_ffn_conv, c_prompt, c_sample, w_ada, b_ada, w_in, b_igate, b_fgate, attn_sinks, mlstm_norm_w, w_proj_a, w_proj_b, w_out, ln1_w, ln1_b, w_up, b_up, conv_w, conv_b, w_down, ln2_w, ln2_b):
    Bp, Tp, D = x_prompt.shape
    Bs, Ts, _ = x_sample.shape
    l = 0

    mod = _mod_call(jnp.concatenate([c_prompt, c_sample], axis=0), w_ada[l], b_ada[l][None])
    mod_p = mod[:Bp].reshape(Bp, 6, D)
    mod_s = mod[Bp:].reshape(Bs, 6, D)

    w = w_in[l]
    o_i, o_f, o_o = Z_OFF[6], Z_OFF[7], Z_OFF[8]
    lane_pad = ((0, 0), (0, LANES - N_HEADS_B))
    w_main = jnp.concatenate([w[:, :o_i], w[:, o_o:],
                              jnp.pad(w[:, o_i:o_f], lane_pad), jnp.pad(w[:, o_f:o_o], lane_pad)],
                             axis=1).astype(BF16)
    b_ic = jnp.pad(b_igate[l][None], lane_pad)
    b_fc = jnp.pad(b_fgate[l][None], lane_pad)
    inproj_w = (w_main, b_ic, b_fc)
    sinks = attn_sinks[l]
    norm_w = mlstm_norm_w[l][None]
    post_w = (w_proj_a[l].astype(BF16), w_proj_b[l].astype(BF16), w_out[l].astype(BF16),
              ln1_w[l][None], ln1_b[l][None], w_up[l].astype(BF16), b_up[l][None], conv_w[l], conv_b[l][None],
              w_down[l].astype(BF16), ln2_w[l][None], ln2_b[l][None])

    qa, ka, va, qm, kt, vm, og, ga, gb, a_row, b_col, a_col = _inproj_call(
        x_prompt, mod_p, *inproj_w, S=1, R=INPROJ_ROWS, L=MLSTM_CHUNK, q_transposed=True)
    ya = _attn_prompt_call(qa, ka, va, sinks)
    yb, p_c, p_n, p_m = _mlstm_call(
        qm, kt, vm, og, a_row, b_col, a_col, norm_w,
        jnp.zeros((Bp, N_HEADS_B, DV_B, DQK_B), F32), jnp.zeros((Bp, N_HEADS_B, DQK_B, 1), F32),
        jnp.zeros((Bp, 1, LANES), F32), SB=4, L=MLSTM_CHUNK)
    p_m = p_m[:, 0, :N_HEADS_B]
    y_prompt, p_conv = _post_call(x_prompt, mod_p, ya, yb, ga, gb,
                                  jnp.zeros((Bp, CONV_W - 1, 2 * D_FF), F32), *post_w, S=1, R=256)
    p_k = ka[:, Tp - KV_WIN:].reshape(Bp, KV_WIN, N_KV_A, HEAD_DIM_A)
    p_v = va[:, Tp - KV_WIN:].reshape(Bp, KV_WIN, N_KV_A, HEAD_DIM_A)

    qa, ka, va, qm, kt, vm, og, ga, gb, a_row, b_col, a_col = _inproj_call(
        x_sample, mod_s, *inproj_w, S=16, R=Ts, L=Ts, q_transposed=False)
    ya, s_k, s_v = _attn_sample_call(qa, ka, va, cache_attn_k[l].reshape(Bs, KV_WIN, KV_A_W),
                                     cache_attn_v[l].reshape(Bs, KV_WIN, KV_A_W), sinks, NS=8)
    yb, s_c, s_n, s_m = _mlstm_call(
        qm, kt, vm, og, a_row, b_col, a_col, norm_w,
        state_mlstm_C[l], state_mlstm_n[l][:, :, :, None], jnp.pad(state_mlstm_m[l][:, None, :], ((0, 0),) + lane_pad),
        SB=4, L=Ts)
    s_m = s_m[:, 0, :N_HEADS_B]
    y_sample, s_conv = _post_call(x_sample, mod_s, ya, yb, ga, gb, state_ffn_conv[l], *post_w, S=8, R=Ts)

    def st(a, shape):
        return a.reshape((1,) + shape)

    return (y_prompt, y_sample,
            st(p_k, (Bp, KV_WIN, N_KV_A, HEAD_DIM_A)), st(p_v, (Bp, KV_WIN, N_KV_A, HEAD_DIM_A)),
            st(p_c, (Bp, N_HEADS_B, DV_B, DQK_B)), st(p_n, (Bp, N_HEADS_B, DQK_B)), st(p_m, (Bp, N_HEADS_B)),
            st(p_conv, (Bp, CONV_W - 1, 2 * D_FF)),
            st(s_k, (Bs, KV_WIN, N_KV_A, HEAD_DIM_A)), st(s_v, (Bs, KV_WIN, N_KV_A, HEAD_DIM_A)),
            st(s_c, (Bs, N_HEADS_B, DV_B, DQK_B)), st(s_n, (Bs, N_HEADS_B, DQK_B)), st(s_m, (Bs, N_HEADS_B)),
            st(s_conv, (Bs, CONV_W - 1, 2 * D_FF)))
```

```python
import functools

import numpy as np
import jax
import jax.numpy as jnp
from jax import lax
from jax.experimental import pallas as pl
from jax.experimental.pallas import tpu as pltpu

F32 = jnp.float32
BF16 = jnp.bfloat16

D_MODEL = 1024
DEPTH = 1
PAST_LEN = 2048
CHUNK = 64
N_HEADS_A = 16
N_KV_A = 2
HEAD_DIM_A = 64
GROUP_A = N_HEADS_A // N_KV_A
WINDOW = 128
WINDOW_CHUNKS = WINDOW // CHUNK
KV_WIN = min(WINDOW, PAST_LEN)
N_HEADS_B = 4
DQK_B = 128
DV_B = 256
D_FF = 2816
CONV_W = 3
LN_EPS = 1e-5
ALPHA = (2 * DEPTH) ** 0.25
Q_A = N_HEADS_A * HEAD_DIM_A
KV_A_W = N_KV_A * HEAD_DIM_A
QK_B = N_HEADS_B * DQK_B
V_B = N_HEADS_B * DV_B
Z_PARTS = (Q_A, KV_A_W, KV_A_W, QK_B, QK_B, V_B, N_HEADS_B, N_HEADS_B, V_B, D_MODEL, D_MODEL)
Z_OFF = tuple(int(v) for v in np.cumsum((0,) + Z_PARTS))

LANES = 128
LOG2E = float(np.log2(np.e))

M_QA = 0
M_KA = M_QA + Q_A
M_VA = M_KA + KV_A_W
M_QM = M_VA + KV_A_W
M_KM = M_QM + QK_B
M_VM = M_KM + QK_B
M_OG = M_VM + V_B
M_GA = M_OG + V_B
M_GB = M_GA + D_MODEL
M_GI = M_GB + D_MODEL
M_GF = M_GI + LANES
M_END = M_GF + LANES
ATT_QB = 128
ATT_SUB = 8
ATT_W = 256
FFN_CHUNKS = 11
MLSTM_CHUNK = 256
INPROJ_ROWS = 512
POST_ROWS = 512


def _bdot(a, b):
    return jnp.dot(a.astype(BF16), b.astype(BF16), preferred_element_type=F32)


def _bdot_nt(a, b):
    return lax.dot_general(a.astype(BF16), b.astype(BF16), (((1,), (1,)), ((), ())),
                           preferred_element_type=F32)


def _const_spec(shape):
    nd = len(shape)
    return pl.BlockSpec(shape, lambda *_: (0,) * nd, pipeline_mode=pl.Buffered(1))


def _layer_norm(x, w, b):
    mu = jnp.mean(x, -1, keepdims=True)
    xc = x - mu
    var = jnp.mean(jnp.square(xc), -1, keepdims=True)
    return xc * lax.rsqrt(var + LN_EPS) * w + b


def _mod_kernel(c_ref, w_ref, b_ref, o_ref):
    c = c_ref[...]
    s = c * jax.nn.sigmoid(c)
    o_ref[...] = _bdot(s, w_ref[...]) + b_ref[...]


def _mod_call(c_all, w_ada, b_ada):
    n = c_all.shape[0]
    tn = 1536
    return pl.pallas_call(
        _mod_kernel,
        grid=(6 * D_MODEL // tn,),
        in_specs=[pl.BlockSpec((n, D_MODEL), lambda j: (0, 0)),
                  pl.BlockSpec((D_MODEL, tn), lambda j: (0, j)),
                  pl.BlockSpec((1, tn), lambda j: (0, j))],
        out_specs=pl.BlockSpec((n, tn), lambda j: (0, j)),
        out_shape=jax.ShapeDtypeStruct((n, 6 * D_MODEL), F32),
        name="mod",
    )(c_all, w_ada, b_ada)


def _log_sigmoid(x):
    return jnp.minimum(x, 0.0) - jnp.log1p(jnp.exp(-jnp.abs(x)))


def _seg_scan(x, L, axis, op, ident):
    pos = lax.broadcasted_iota(jnp.int32, x.shape, axis) % L
    step = 1
    while step < L:
        x = op(x, jnp.where(pos >= step, pltpu.roll(x, step, axis=axis), ident))
        step *= 2
    return x


def _inproj_kernel(x_ref, mod_ref, w_ref, bic_ref, bfc_ref,
                   qa_ref, ka_ref, va_ref, qm_ref, kt_ref, vm_ref, og_ref, ga_ref, gb_ref,
                   ar_ref, bc_ref, ac_ref, *, S, R, L, q_transposed):
    SR = S * R
    sh = mod_ref[:, 0:1, :]
    sc = mod_ref[:, 1:2, :]
    h = (x_ref[...] * (1.0 + sc) + sh).reshape(SR, D_MODEL).astype(BF16)

    def proj(lo, hi):
        return jnp.dot(h, w_ref[:, lo:hi], preferred_element_type=F32)

    q = proj(M_QA, M_KA) * (HEAD_DIM_A ** -0.5)
    if q_transposed:
        qt = (q * LOG2E).T.astype(BF16)
        for s in range(S):
            qa_ref[s] = qt[:, s * R:(s + 1) * R]
    else:
        qa_ref[...] = q.astype(BF16).reshape(S, R, Q_A)
    kv = proj(M_KA, M_QM)
    ka_ref[...] = kv[:, :KV_A_W].reshape(S, R, KV_A_W)
    va_ref[...] = kv[:, KV_A_W:].reshape(S, R, KV_A_W)
    qm_ref[...] = proj(M_QM, M_KM).astype(BF16).reshape(S, R, QK_B)
    kt = (proj(M_KM, M_VM) * (DQK_B ** -0.5)).T.astype(BF16)
    vm_ref[...] = proj(M_VM, M_OG).astype(BF16).reshape(S, R, V_B)
    og_ref[...] = proj(M_OG, M_GA).reshape(S, R, V_B)
    ga_ref[...] = proj(M_GA, M_GB).reshape(S, R, D_MODEL)
    gb_ref[...] = proj(M_GB, M_GI).reshape(S, R, D_MODEL)

    gates = proj(M_GI, M_END)
    i_col = gates[:, :LANES] + bic_ref[...]
    logf_col = _log_sigmoid(gates[:, LANES:] + bfc_ref[...])
    b_col = _seg_scan(logf_col, L, 0, jnp.add, 0.0)
    a_col = _seg_scan(i_col - b_col, L, 0, jnp.maximum, -jnp.inf)
    bc_ref[...] = b_col.reshape(S, R, LANES)
    ac_ref[...] = a_col.reshape(S, R, LANES)
    i_row = i_col.T[:8]
    logf_row = logf_col.T[:8]
    a_row = i_row - _seg_scan(logf_row, L, 1, jnp.add, 0.0)
    for s in range(S):
        kt_ref[s] = kt[:, s * R:(s + 1) * R]
        ar_ref[s] = a_row[:, s * R:(s + 1) * R]


def _inproj_call(x, mod, w_main, b_ic, b_fc, *, S, R, L, q_transposed):
    N, T, _ = x.shape
    nsb, nrb = N // S, T // R

    def tok(c):
        return pl.BlockSpec((S, R, c), lambda i, j: (i, j, 0))

    def tok_shape(c, dt):
        return jax.ShapeDtypeStruct((N, T, c), dt)

    def tr(c):
        return pl.BlockSpec((S, c, R), lambda i, j: (i, 0, j))

    return pl.pallas_call(
        functools.partial(_inproj_kernel, S=S, R=R, L=L, q_transposed=q_transposed),
        grid=(nsb, nrb),
        in_specs=[tok(D_MODEL),
                  pl.BlockSpec((S, 6, D_MODEL), lambda i, j: (i, 0, 0)),
                  _const_spec((D_MODEL, M_END)),
                  _const_spec((1, LANES)), _const_spec((1, LANES))],
        out_specs=[tr(Q_A) if q_transposed else tok(Q_A), tok(KV_A_W), tok(KV_A_W), tok(QK_B), tr(QK_B), tok(V_B),
                   tok(V_B), tok(D_MODEL), tok(D_MODEL), tr(8), tok(LANES), tok(LANES)],
        out_shape=[jax.ShapeDtypeStruct((N, Q_A, T), BF16) if q_transposed else tok_shape(Q_A, BF16),
                   tok_shape(KV_A_W, F32), tok_shape(KV_A_W, F32),
                   tok_shape(QK_B, BF16), jax.ShapeDtypeStruct((N, QK_B, T), BF16), tok_shape(V_B, BF16),
                   tok_shape(V_B, F32), tok_shape(D_MODEL, F32), tok_shape(D_MODEL, F32),
                   jax.ShapeDtypeStruct((N, 8, T), F32), tok_shape(LANES, F32), tok_shape(LANES, F32)],
        compiler_params=pltpu.CompilerParams(dimension_semantics=("parallel", "parallel")),
        name="inproj",
    )(x, mod, w_main, b_ic, b_fc)


def _alibi_slopes():
    return 2.0 ** (-8.0 * np.arange(1, N_HEADS_A + 1, dtype=np.float64) / N_HEADS_A)


def _attn_bias(qpos, kpos, kvalid, transposed=False):
    qpos = np.asarray(qpos)[:, None]
    kpos = np.asarray(kpos)[None, :]
    qc, kc = qpos // CHUNK, kpos // CHUNK
    visible = (kpos >= 0) & (kc <= qc) & (kc >= qc - WINDOW_CHUNKS) & np.asarray(kvalid)[None, :]
    dist = np.abs(qpos - kpos).astype(np.float64)
    slopes = _alibi_slopes()
    out = np.zeros((N_KV_A, GROUP_A // 2, qpos.shape[0], 2, kpos.shape[1]), np.float32)
    for kv in range(N_KV_A):
        for p in range(GROUP_A // 2):
            for half in range(2):
                hd = kv * GROUP_A + 2 * p + half
                out[kv, p, :, half, :] = np.where(visible, -slopes[hd] * dist, -np.inf)
    if transposed:
        return out.transpose(0, 3, 4, 1, 2).reshape(N_KV_A, 2 * kpos.shape[1], (GROUP_A // 2) * qpos.shape[0])
    return out.reshape(N_KV_A, (GROUP_A // 2) * qpos.shape[0], 2 * kpos.shape[1])


def _attn_core(q, kwin, vwin, bias_of, sinks_ref, store):
    Mq = q.shape[0]
    W = kwin.shape[0]
    npair = GROUP_A // 2
    lane = lax.broadcasted_iota(jnp.int32, (npair * Mq, LANES), 1)
    for kv in range(N_KV_A):
        kk = kwin[:, kv * HEAD_DIM_A:(kv + 1) * HEAD_DIM_A].astype(BF16)
        vv = vwin[:, kv * HEAD_DIM_A:(kv + 1) * HEAD_DIM_A].astype(BF16)
        zero = jnp.zeros_like(kk)
        one = jnp.ones_like(vv)
        k2 = jnp.concatenate([jnp.concatenate([kk, zero], axis=1),
                              jnp.concatenate([zero, kk], axis=1)], axis=0)
        v_aug = jnp.concatenate([vv, one, one, vv], axis=1)
        qs = jnp.concatenate([q[:, (kv * npair + p) * LANES:(kv * npair + p + 1) * LANES]
                              for p in range(npair)], axis=0)
        s = _bdot_nt(qs, k2) + bias_of(kv)
        probs, sink_terms = [], []
        for half in range(2):
            sh = s[:, half * W:(half + 1) * W]
            sink = jnp.concatenate(
                [jnp.full((Mq, 1), sinks_ref[kv * GROUP_A + 2 * p + half], F32) for p in range(npair)], axis=0)
            mx = jnp.maximum(jnp.max(sh, axis=1, keepdims=True), sink)
            probs.append(jnp.exp(sh - mx).astype(BF16))
            sink_terms.append(jnp.exp(sink - mx))
        o = jnp.dot(jnp.concatenate(probs, axis=0), v_aug, preferred_element_type=F32)
        oe, oo = o[:npair * Mq], o[npair * Mq:]
        ye = oe[:, :LANES] / (oe[:, LANES:] + sink_terms[0])
        yo = oo[:, LANES:] / (oo[:, :LANES] + sink_terms[1])
        y = jnp.where(lane < HEAD_DIM_A, ye, yo).astype(BF16)
        for p in range(npair):
            store((kv * npair + p) * LANES, y[p * Mq:(p + 1) * Mq])


def _attn_prompt_kernel(sinks_ref, qt_ref, kp_ref, kc_ref, vp_ref, vc_ref, bias0_ref, bias1_ref, o_ref):
    Mq = ATT_QB
    npair = GROUP_A // 2
    keys = jnp.concatenate([kp_ref[0], kc_ref[0]], axis=0)
    vals_t = jnp.concatenate([vp_ref[0], vc_ref[0]], axis=0).T.astype(BF16)
    ones = jnp.ones((HEAD_DIM_A, ATT_W), BF16)
    pair_of_lane = lax.broadcasted_iota(jnp.int32, (1, npair * Mq), 1) // Mq
    units = [(sub, kv) for sub in range(ATT_SUB) for kv in range(N_KV_A)]
    sts = {}
    for sub, kv in units:
        kk = keys[sub * Mq:sub * Mq + ATT_W, kv * HEAD_DIM_A:(kv + 1) * HEAD_DIM_A].astype(BF16)
        zero = jnp.zeros_like(kk)
        k2 = jnp.concatenate([jnp.concatenate([kk, zero], axis=1),
                              jnp.concatenate([zero, kk], axis=1)], axis=0)
        qt = jnp.concatenate([qt_ref[0, (kv * npair + p) * LANES:(kv * npair + p + 1) * LANES, sub * Mq:(sub + 1) * Mq]
                              for p in range(npair)], axis=1)
        sts[sub, kv] = jnp.dot(k2, qt, preferred_element_type=F32)
    probs, sink_terms = {}, {}
    for sub, kv in units:
        st = sts[sub, kv] + (bias0_ref[0, kv] if sub == 0 else bias1_ref[0, kv])
        for half in range(2):
            sh = st[half * ATT_W:(half + 1) * ATT_W]
            sink = jnp.zeros((1, npair * Mq), F32)
            for p in range(npair):
                sink = jnp.where(pair_of_lane == p, sinks_ref[kv * GROUP_A + 2 * p + half] * LOG2E, sink)
            mx = jnp.maximum(jnp.max(sh, axis=0, keepdims=True), sink)
            probs[sub, kv, half] = jnp.exp2(sh - mx).astype(BF16)
            sink_terms[sub, kv, half] = jnp.exp2(sink - mx)
    ots = {}
    for sub, kv in units:
        vt = vals_t[kv * HEAD_DIM_A:(kv + 1) * HEAD_DIM_A, sub * Mq:sub * Mq + ATT_W]
        for half in range(2):
            lhs = jnp.concatenate([vt, ones] if half == 0 else [ones, vt], axis=0)
            ots[sub, kv, half] = jnp.dot(lhs, probs[sub, kv, half], preferred_element_type=F32)
    for sub, kv in units:
        oe, oo = ots[sub, kv, 0], ots[sub, kv, 1]
        ye = oe[:HEAD_DIM_A] / (oe[HEAD_DIM_A:] + sink_terms[sub, kv, 0])
        yo = oo[HEAD_DIM_A:] / (oo[:HEAD_DIM_A] + sink_terms[sub, kv, 1])
        yt = jnp.concatenate([ye, yo], axis=0)
        for p in range(npair):
            col = (kv * npair + p) * LANES
            o_ref[0, sub * Mq:(sub + 1) * Mq, col:col + LANES] = yt[:, p * Mq:(p + 1) * Mq].T.astype(BF16)


def _attn_prompt_call(qa_t, ka, va, sinks):
    N, _, T = qa_t.shape
    rows = ATT_QB * ATT_SUB
    rel_q = ATT_QB + np.arange(ATT_QB)
    rel_k = np.arange(ATT_W)
    bias = np.stack([_attn_bias(rel_q, rel_k, rel_k >= ATT_QB, transposed=True),
                     _attn_bias(rel_q, rel_k, rel_k >= 0, transposed=True)]) * np.float32(LOG2E)
    kv_prev = pl.BlockSpec((1, ATT_QB, KV_A_W), lambda n, j: (n, jnp.maximum(ATT_SUB * j - 1, 0), 0))
    kv_cur = pl.BlockSpec((1, rows, KV_A_W), lambda n, j: (n, j, 0))
    return pl.pallas_call(
        _attn_prompt_kernel,
        grid=(N, T // rows),
        in_specs=[pl.BlockSpec(memory_space=pltpu.SMEM),
                  pl.BlockSpec((1, Q_A, rows), lambda n, j: (n, 0, j)),
                  kv_prev, kv_cur, kv_prev, kv_cur,
                  pl.BlockSpec((1,) + bias.shape[1:], lambda n, j: (jnp.minimum(j, 1), 0, 0, 0)),
                  pl.BlockSpec((1,) + bias.shape[1:], lambda n, j: (1, 0, 0, 0), pipeline_mode=pl.Buffered(1))],
        out_specs=pl.BlockSpec((1, rows, Q_A), lambda n, j: (n, j, 0)),
        out_shape=jax.ShapeDtypeStruct((N, T, Q_A), BF16),
        compiler_params=pltpu.CompilerParams(dimension_semantics=("parallel", "parallel")),
        name="attn_prompt",
    )(sinks, qa_t, ka, ka, va, va, jnp.asarray(bias), jnp.asarray(bias))


def _attn_sample_kernel(sinks_ref, q_ref, kn_ref, vn_ref, kc_ref, vc_ref, bias_ref,
                        o_ref, ko_ref, vo_ref, *, T, NS):
    pad = jnp.zeros((ATT_W - KV_WIN - T, KV_A_W), F32)
    for s in range(NS):
        kwin = jnp.concatenate([kc_ref[s], kn_ref[s], pad], axis=0)
        vwin = jnp.concatenate([vc_ref[s], vn_ref[s], pad], axis=0)
        ko_ref[s] = kwin[T:T + KV_WIN]
        vo_ref[s] = vwin[T:T + KV_WIN]

        def store(col, val, s=s):
            o_ref[s, :, col:col + LANES] = val

        _attn_core(q_ref[s], kwin, vwin, lambda kv: bias_ref[kv], sinks_ref, store)


def _attn_sample_call(qa, ka, va, cache_k, cache_v, sinks, *, NS):
    N, T, _ = qa.shape
    qpos = PAST_LEN + np.arange(T)
    kpos = PAST_LEN - KV_WIN + np.arange(ATT_W)
    bias = _attn_bias(qpos, kpos, np.arange(ATT_W) < KV_WIN + T)
    new = pl.BlockSpec((NS, T, KV_A_W), lambda n: (n, 0, 0))
    cache = pl.BlockSpec((NS, KV_WIN, KV_A_W), lambda n: (n, 0, 0))
    cache_shape = jax.ShapeDtypeStruct((N, KV_WIN, KV_A_W), F32)
    return pl.pallas_call(
        functools.partial(_attn_sample_kernel, T=T, NS=NS),
        grid=(N // NS,),
        in_specs=[pl.BlockSpec(memory_space=pltpu.SMEM),
                  pl.BlockSpec((NS, T, Q_A), lambda n: (n, 0, 0)),
                  new, new, cache, cache,
                  _const_spec(bias.shape)],
        out_specs=[pl.BlockSpec((NS, T, Q_A), lambda n: (n, 0, 0)), cache, cache],
        out_shape=[jax.ShapeDtypeStruct((N, T, Q_A), BF16), cache_shape, cache_shape],
        compiler_params=pltpu.CompilerParams(dimension_semantics=("parallel",)),
        name="attn_sample",
    )(sinks, qa, ka, va, cache_k, cache_v, jnp.asarray(bias))


def _mlstm_kernel(q_ref, kt_ref, v_ref, og_ref, ar_ref, bc_ref, ac_ref, nw_ref, c0_ref, n0_ref, m0_ref,
                  y_ref, c_ref, n_ref, m_ref, ct_ref, *, SB, L):
    j = pl.program_id(1)
    heads = [(sb, hd) for sb in range(SB) for hd in range(N_HEADS_B)]

    @pl.when(j == 0)
    def _():
        for sb, hd in heads:
            ct_ref[sb, hd, :, :DV_B] = c0_ref[sb, hd].T
            ct_ref[sb, hd, :, DV_B:] = jnp.broadcast_to(n0_ref[sb, hd], (DQK_B, LANES))
        m_ref[...] = m0_ref[...]

    causal = (lax.broadcasted_iota(jnp.int32, (L, L), 0) >= lax.broadcasted_iota(jnp.int32, (L, L), 1))
    ones = jnp.ones((L, LANES), BF16)
    gates = []
    for sb in range(SB):
        a4 = ac_ref[sb]
        b4 = bc_ref[sb]
        m_row = m_ref[sb]
        a_last = a4[L - 1:L]
        b_last = b4[L - 1:L]
        mx = jnp.maximum(a_last, m_row)
        m_new = b_last + mx
        gates.append(dict(a4=a4, b4=b4, a_last=a_last, m_row=m_row, decay4=jnp.exp(b_last + m_row - m_new),
                          rho4=jnp.exp(a_last - mx)))
        m_ref[sb] = m_new

    def lanes(x):
        return jnp.broadcast_to(x, (x.shape[0], LANES))

    def col(x, hd):
        return x[:, hd:hd + 1]

    qs = {h: q_ref[h[0], :, h[1] * DQK_B:(h[1] + 1) * DQK_B] for h in heads}
    kts = {h: kt_ref[h[0], h[1] * DQK_B:(h[1] + 1) * DQK_B, :] for h in heads}
    cts = {h: ct_ref[h[0], h[1]] for h in heads}
    qk = {h: jnp.dot(qs[h], kts[h], preferred_element_type=F32) for h in heads}
    qc = {h: jnp.dot(qs[h], cts[h].astype(BF16), preferred_element_type=F32) for h in heads}
    lhs, a_bs = {}, {}
    for h in heads:
        sb, hd = h
        gt = gates[sb]
        a_row = ar_ref[sb, hd:hd + 1, :]
        a_bs[h] = lanes(col(gt["a4"], hd))
        a_full = jnp.concatenate([a_bs[h]] * (L // LANES), axis=1) if L >= LANES else a_bs[h][:, :L]
        w = jnp.exp(jnp.where(causal, a_row - a_full, -jnp.inf))
        s = (qk[h] * w).astype(BF16)
        kw = (kts[h].astype(F32) * jnp.exp(a_row - col(gt["a_last"], hd))).astype(BF16)
        lhs[h] = jnp.concatenate([s, kw], axis=0)
    pu = {}
    for h in heads:
        sb, hd = h
        v_aug = jnp.concatenate([v_ref[sb, :, hd * DV_B:(hd + 1) * DV_B], ones], axis=1)
        pu[h] = jnp.dot(lhs[h], v_aug, preferred_element_type=F32)
    ones_sum = jnp.ones((DV_B, LANES), BF16)
    for h in heads:
        sb, hd = h
        gt = gates[sb]
        a_b = a_bs[h]
        m_b = lanes(col(gt["m_row"], hd))
        big = jnp.maximum(a_b, m_b)
        r_b = jnp.exp(a_b - big)
        g_b = jnp.exp(m_b - big)
        e_b = jnp.exp(-(lanes(col(gt["b4"], hd)) + big))
        p, c = pu[h], qc[h]
        den = r_b * p[:L, DV_B:] + g_b * c[:, DV_B:]
        inv = 1.0 / jnp.maximum(jnp.abs(den), e_b)
        hh = jnp.concatenate([(r_b * p[:L, k * LANES:(k + 1) * LANES] + g_b * c[:, k * LANES:(k + 1) * LANES]) * inv
                              for k in range(DV_B // LANES)], axis=1)
        mu = jnp.dot(hh.astype(BF16), ones_sum, preferred_element_type=F32) * (1.0 / DV_B)
        hc = hh - jnp.concatenate([mu] * (DV_B // LANES), axis=1)
        var = jnp.dot(jnp.square(hc).astype(BF16), ones_sum, preferred_element_type=F32) * (1.0 / DV_B)
        rstd = lax.rsqrt(var + LN_EPS)
        yn = hc * jnp.concatenate([rstd] * (DV_B // LANES), axis=1) * nw_ref[:, hd * DV_B:(hd + 1) * DV_B]
        og = og_ref[sb, :, hd * DV_B:(hd + 1) * DV_B]
        y_ref[sb, :, hd * DV_B:(hd + 1) * DV_B] = (jax.nn.sigmoid(og) * yn).astype(BF16)
        ct_ref[sb, hd] = col(gt["decay4"], hd) * cts[h] + col(gt["rho4"], hd) * pu[h][L:]

    @pl.when(j == pl.num_programs(1) - 1)
    def _():
        for sb, hd in heads:
            c_ref[sb, hd] = ct_ref[sb, hd, :, :DV_B].T
            n_ref[sb, hd] = ct_ref[sb, hd, :, DV_B:DV_B + 1]


def _mlstm_call(qm, kt, vm, og, a_row, b_col, a_col, norm_w, c0, n0, m0, *, SB, L):
    N, T, _ = qm.shape

    def tok(c):
        return pl.BlockSpec((SB, L, c), lambda i, j: (i, j, 0))

    def tr(c):
        return pl.BlockSpec((SB, c, L), lambda i, j: (i, 0, j))

    st_c = pl.BlockSpec((SB, N_HEADS_B, DV_B, DQK_B), lambda i, j: (i, 0, 0, 0))
    st_n = pl.BlockSpec((SB, N_HEADS_B, DQK_B, 1), lambda i, j: (i, 0, 0, 0))
    st_m = pl.BlockSpec((SB, 1, LANES), lambda i, j: (i, 0, 0))
    return pl.pallas_call(
        functools.partial(_mlstm_kernel, SB=SB, L=L),
        grid=(N // SB, T // L),
        in_specs=[tok(QK_B), tr(QK_B), tok(V_B), tok(V_B), tr(8), tok(LANES), tok(LANES),
                  _const_spec((1, V_B)), st_c, st_n, st_m],
        out_specs=[tok(V_B), st_c, st_n, st_m],
        out_shape=[jax.ShapeDtypeStruct((N, T, V_B), BF16),
                   jax.ShapeDtypeStruct((N, N_HEADS_B, DV_B, DQK_B), F32),
                   jax.ShapeDtypeStruct((N, N_HEADS_B, DQK_B, 1), F32),
                   jax.ShapeDtypeStruct((N, 1, LANES), F32)],
        scratch_shapes=[pltpu.VMEM((SB, N_HEADS_B, DQK_B, DV_B + LANES), F32)],
        compiler_params=pltpu.CompilerParams(dimension_semantics=("parallel", "arbitrary")),
        name="mlstm",
    )(qm, kt, vm, og, a_row, b_col, a_col, norm_w, c0, n0, m0)


def _merge_kernel(x_ref, mod_ref, ya_ref, yb_ref, ga_ref, gb_ref,
                  wpa_ref, wpb_ref, wo_ref, l1w_ref, l1b_ref, x1_ref, *, S, R):
    SR = S * R
    g1 = mod_ref[:, 2:3, :]
    pa = jnp.dot(ya_ref[...].reshape(SR, Q_A), wpa_ref[...], preferred_element_type=F32)
    pb = jnp.dot(yb_ref[...].reshape(SR, V_B), wpb_ref[...], preferred_element_type=F32)
    merged = (jax.nn.sigmoid(ga_ref[...].reshape(SR, D_MODEL)) * pa
              + jax.nn.sigmoid(gb_ref[...].reshape(SR, D_MODEL)) * pb)
    mo = _bdot(merged, wo_ref[...]).reshape(S, R, D_MODEL)
    x1_ref[...] = _layer_norm(ALPHA * x_ref[...] + g1 * mo, l1w_ref[...], l1b_ref[...])


def _merge_call(x, mod, ya, yb, ga, gb, wpa, wpb, wo, l1w, l1b, *, S, R):
    N, T, _ = x.shape

    def tok(c):
        return pl.BlockSpec((S, R, c), lambda i, j: (i, j, 0))

    return pl.pallas_call(
        functools.partial(_merge_kernel, S=S, R=R),
        grid=(N // S, T // R),
        in_specs=[tok(D_MODEL), pl.BlockSpec((S, 6, D_MODEL), lambda i, j: (i, 0, 0)),
                  tok(Q_A), tok(V_B), tok(D_MODEL), tok(D_MODEL),
                  _const_spec((Q_A, D_MODEL)), _const_spec((V_B, D_MODEL)), _const_spec((D_MODEL, D_MODEL)),
                  _const_spec((1, D_MODEL)), _const_spec((1, D_MODEL))],
        out_specs=tok(D_MODEL),
        out_shape=jax.ShapeDtypeStruct((N, T, D_MODEL), F32),
        compiler_params=pltpu.CompilerParams(dimension_semantics=("parallel", "parallel")),
        name="merge",
    )(x, mod, ya, yb, ga, gb, wpa, wpb, wo, l1w, l1b)


def _ffn_kernel(x1_ref, mod_ref, cin_ref, wup_ref, bup_ref, cw_ref, cb_ref, wdn_ref, l2w_ref, l2b_ref,
                y_ref, cout_ref, ctx_ref, act_ref, *, S, R):
    SR = S * R
    FC = D_FF // FFN_CHUNKS

    @pl.when(pl.program_id(1) == 0)
    def _():
        cout_ref[...] = cin_ref[...]

    sh2 = mod_ref[:, 3:4, :]
    sc2 = mod_ref[:, 4:5, :]
    g2 = mod_ref[:, 5:6, :]
    x1 = x1_ref[...]
    h2 = (x1 * (1.0 + sc2) + sh2).reshape(SR, D_MODEL).astype(BF16)

    def conv(lo):
        u = (jnp.dot(h2, wup_ref[:, lo:lo + FC], preferred_element_type=F32)
             + bup_ref[:, lo:lo + FC]).reshape(S, R, FC)
        ctx_ref[:, 6:8, :] = cout_ref[:, :, lo:lo + FC]
        ctx_ref[:, 8:, :] = u
        cout_ref[:, :, lo:lo + FC] = ctx_ref[:, R + 6:R + 8, :]
        return (cb_ref[:, lo:lo + FC]
                + cw_ref[0:1, lo:lo + FC] * ctx_ref[:, 6:R + 6, :]
                + cw_ref[1:2, lo:lo + FC] * ctx_ref[:, 7:R + 7, :]
                + cw_ref[2:3, lo:lo + FC] * u)

    for ci in range(FFN_CHUNKS):
        lo = ci * FC
        a = conv(lo)
        g = conv(D_FF + lo)
        act_ref[:, lo:lo + FC] = (a * jax.nn.gelu(g, approximate=True)).reshape(SR, FC).astype(BF16)

    f = jnp.dot(act_ref[...], wdn_ref[...], preferred_element_type=F32).reshape(S, R, D_MODEL)
    y_ref[...] = _layer_norm(ALPHA * x1 + g2 * f, l2w_ref[...], l2b_ref[...])


def _ffn_call(x1, mod, conv_in, wup, bup, cw, cb, wdn, l2w, l2b, *, S, R):
    N, T, _ = x1.shape
    FC = D_FF // FFN_CHUNKS

    def tok(c):
        return pl.BlockSpec((S, R, c), lambda i, j: (i, j, 0))

    conv_spec = pl.BlockSpec((S, CONV_W - 1, 2 * D_FF), lambda i, j: (i, 0, 0))
    return pl.pallas_call(
        functools.partial(_ffn_kernel, S=S, R=R),
        grid=(N // S, T // R),
        in_specs=[tok(D_MODEL), pl.BlockSpec((S, 6, D_MODEL), lambda i, j: (i, 0, 0)), conv_spec,
                  _const_spec((D_MODEL, 2 * D_FF)), _const_spec((1, 2 * D_FF)),
                  _const_spec((CONV_W, 2 * D_FF)), _const_spec((1, 2 * D_FF)),
                  _const_spec((D_FF, D_MODEL)), _const_spec((1, D_MODEL)), _const_spec((1, D_MODEL))],
        out_specs=[tok(D_MODEL), conv_spec],
        out_shape=[jax.ShapeDtypeStruct((N, T, D_MODEL), F32),
                   jax.ShapeDtypeStruct((N, CONV_W - 1, 2 * D_FF), F32)],
        scratch_shapes=[pltpu.VMEM((S, R + 8, FC), F32), pltpu.VMEM((S * R, D_FF), BF16)],
        compiler_params=pltpu.CompilerParams(dimension_semantics=("parallel", "arbitrary")),
        name="ffn",
    )(x1, mod, conv_in, wup, bup, cw, cb, wdn, l2w, l2b)


def kernel(x_prompt, x_sample, cache_attn_k, cache_attn_v, state_mlstm_C, state_mlstm_n, state_mlstm_m, state_ffn_conv, c_prompt, c_sample, w_ada, b_ada, w_in, b_igate, b_fgate, attn_sinks, mlstm_norm_w, w_proj_a, w_proj_b, w_out, ln1_w, ln1_b, w_up, b_up, conv_w, conv_b, w_down, ln2_w, ln2_b):
    Bp, Tp, D = x_prompt.shape
    Bs, Ts, _ = x_sample.shape
    l = 0

    mod = _mod_call(jnp.concatenate([c_prompt, c_sample], axis=0), w_ada[l], b_ada[l][None])
    mod_p = mod[:Bp].reshape(Bp, 6, D)
    mod_s = mod[Bp:].reshape(Bs, 6, D)

    w = w_in[l]
    o_i, o_f, o_o = Z_OFF[6], Z_OFF[7], Z_OFF[8]
    lane_pad = ((0, 0), (0, LANES - N_HEADS_B))
    w_main = jnp.concatenate([w[:, :o_i], w[:, o_o:],
                              jnp.pad(w[:, o_i:o_f], lane_pad), jnp.pad(w[:, o_f:o_o], lane_pad)],
                             axis=1).astype(BF16)
    b_ic = jnp.pad(b_igate[l][None], lane_pad)
    b_fc = jnp.pad(b_fgate[l][None], lane_pad)
    inproj_w = (w_main, b_ic, b_fc)
    sinks = attn_sinks[l]
    norm_w = mlstm_norm_w[l][None]
    merge_w = (w_proj_a[l].astype(BF16), w_proj_b[l].astype(BF16), w_out[l].astype(BF16),
               ln1_w[l][None], ln1_b[l][None])
    ffn_w = (w_up[l].astype(BF16), b_up[l][None], conv_w[l], conv_b[l][None],
             w_down[l].astype(BF16), ln2_w[l][None], ln2_b[l][None])

    qa, ka, va, qm, kt, vm, og, ga, gb, a_row, b_col, a_col = _inproj_call(
        x_prompt, mod_p, *inproj_w, S=1, R=INPROJ_ROWS, L=MLSTM_CHUNK, q_transposed=True)
    ya = _attn_prompt_call(qa, ka, va, sinks)
    yb, p_c, p_n, p_m = _mlstm_call(
        qm, kt, vm, og, a_row, b_col, a_col, norm_w,
        jnp.zeros((Bp, N_HEADS_B, DV_B, DQK_B), F32), jnp.zeros((Bp, N_HEADS_B, DQK_B, 1), F32),
        jnp.zeros((Bp, 1, LANES), F32), SB=4, L=MLSTM_CHUNK)
    p_m = p_m[:, 0, :N_HEADS_B]
    x1 = _merge_call(x_prompt, mod_p, ya, yb, ga, gb, *merge_w, S=1, R=POST_ROWS)
    y_prompt, p_conv = _ffn_call(x1, mod_p, jnp.zeros((Bp, CONV_W - 1, 2 * D_FF), F32), *ffn_w, S=1, R=POST_ROWS)
    p_k = ka[:, Tp - KV_WIN:].reshape(Bp, KV_WIN, N_KV_A, HEAD_DIM_A)
    p_v = va[:, Tp - KV_WIN:].reshape(Bp, KV_WIN, N_KV_A, HEAD_DIM_A)

    qa, ka, va, qm, kt, vm, og, ga, gb, a_row, b_col, a_col = _inproj_call(
        x_sample, mod_s, *inproj_w, S=16, R=Ts, L=Ts, q_transposed=False)
    ya, s_k, s_v = _attn_sample_call(qa, ka, va, cache_attn_k[l].reshape(Bs, KV_WIN, KV_A_W),
                                     cache_attn_v[l].reshape(Bs, KV_WIN, KV_A_W), sinks, NS=8)
    yb, s_c, s_n, s_m = _mlstm_call(
        qm, kt, vm, og, a_row, b_col, a_col, norm_w,
        state_mlstm_C[l], state_mlstm_n[l][:, :, :, None], jnp.pad(state_mlstm_m[l][:, None, :], ((0, 0),) + lane_pad),
        SB=4, L=Ts)
    s_m = s_m[:, 0, :N_HEADS_B]
    x1 = _merge_call(x_sample, mod_s, ya, yb, ga, gb, *merge_w, S=16, R=Ts)
    y_sample, s_conv = _ffn_call(x1, mod_s, state_ffn_conv[l], *ffn_w, S=8, R=Ts)

    def st(a, shape):
        return a.reshape((1,) + shape)

    return (y_prompt, y_sample,
            st(p_k, (Bp, KV_WIN, N_KV_A, HEAD_DIM_A)), st(p_v, (Bp, KV_WIN, N_KV_A, HEAD_DIM_A)),
            st(p_c, (Bp, N_HEADS_B, DV_B, DQK_B)), st(p_n, (Bp, N_HEADS_B, DQK_B)), st(p_m, (Bp, N_HEADS_B)),
            st(p_conv, (Bp, CONV_W - 1, 2 * D_FF)),
            st(s_k, (Bs, KV_WIN, N_KV_A, HEAD_DIM_A)), st(s_v, (Bs, KV_WIN, N_KV_A, HEAD_DIM_A)),
            st(s_c, (Bs, N_HEADS_B, DV_B, DQK_B)), st(s_n, (Bs, N_HEADS_B, DQK_B)), st(s_m, (Bs, N_HEADS_B)),
            st(s_conv, (Bs, CONV_W - 1, 2 * D_FF)))
```

```python
import functools

import numpy as np
import jax
import jax.numpy as jnp
from jax import lax
from jax.experimental import pallas as pl
from jax.experimental.pallas import tpu as pltpu

F32 = jnp.float32
BF16 = jnp.bfloat16

D_MODEL = 1024
DEPTH = 1
PAST_LEN = 2048
CHUNK = 64
N_HEADS_A = 16
N_KV_A = 2
HEAD_DIM_A = 64
GROUP_A = N_HEADS_A // N_KV_A
WINDOW = 128
WINDOW_CHUNKS = WINDOW // CHUNK
KV_WIN = min(WINDOW, PAST_LEN)
N_HEADS_B = 4
DQK_B = 128
DV_B = 256
D_FF = 2816
CONV_W = 3
LN_EPS = 1e-5
ALPHA = (2 * DEPTH) ** 0.25
Q_A = N_HEADS_A * HEAD_DIM_A
KV_A_W = N_KV_A * HEAD_DIM_A
QK_B = N_HEADS_B * DQK_B
V_B = N_HEADS_B * DV_B
Z_PARTS = (Q_A, KV_A_W, KV_A_W, QK_B, QK_B, V_B, N_HEADS_B, N_HEADS_B, V_B, D_MODEL, D_MODEL)
Z_OFF = tuple(int(v) for v in np.cumsum((0,) + Z_PARTS))

LANES = 128
SUBLANES = 8
LOG2E = float(np.log2(np.e))

M_QA = 0
M_KA = M_QA + Q_A
M_VA = M_KA + KV_A_W
M_QM = M_VA + KV_A_W
M_KM = M_QM + QK_B
M_VM = M_KM + QK_B
M_OG = M_VM + V_B
M_GA = M_OG + V_B
M_GB = M_GA + D_MODEL
M_GI = M_GB + D_MODEL
M_GF = M_GI + LANES
M_END = M_GF + LANES
ATT_QB = 128
ATT_SUB = 8
ATT_W = 256
FFN_CHUNKS = 11
MLSTM_CHUNK = 256
INPROJ_ROWS = 512
POST_ROWS = 512


def _bdot(a, b):
    return jnp.dot(a.astype(BF16), b.astype(BF16), preferred_element_type=F32)


def _bdot_nt(a, b):
    return lax.dot_general(a.astype(BF16), b.astype(BF16), (((1,), (1,)), ((), ())),
                           preferred_element_type=F32)


def _const_spec(shape):
    nd = len(shape)
    return pl.BlockSpec(shape, lambda *_: (0,) * nd, pipeline_mode=pl.Buffered(1))


def _layer_norm(x, w, b):
    mu = jnp.mean(x, -1, keepdims=True)
    xc = x - mu
    var = jnp.mean(jnp.square(xc), -1, keepdims=True)
    return xc * lax.rsqrt(var + LN_EPS) * w + b


def _mod_kernel(c_ref, w_ref, b_ref, o_ref):
    c = c_ref[...]
    s = c * jax.nn.sigmoid(c)
    o_ref[...] = _bdot(s, w_ref[...]) + b_ref[...]


def _mod_call(c_all, w_ada, b_ada):
    n = c_all.shape[0]
    tn = 1536
    return pl.pallas_call(
        _mod_kernel,
        grid=(6 * D_MODEL // tn,),
        in_specs=[pl.BlockSpec((n, D_MODEL), lambda j: (0, 0)),
                  pl.BlockSpec((D_MODEL, tn), lambda j: (0, j)),
                  pl.BlockSpec((1, tn), lambda j: (0, j))],
        out_specs=pl.BlockSpec((n, tn), lambda j: (0, j)),
        out_shape=jax.ShapeDtypeStruct((n, 6 * D_MODEL), F32),
        name="mod",
    )(c_all, w_ada, b_ada)


def _log_sigmoid(x):
    return jnp.minimum(x, 0.0) - jnp.log1p(jnp.exp(-jnp.abs(x)))


def _seg_scan(x, L, axis, op, ident):
    pos = lax.broadcasted_iota(jnp.int32, x.shape, axis) % L
    step = 1
    while step < L:
        x = op(x, jnp.where(pos >= step, pltpu.roll(x, step, axis=axis), ident))
        step *= 2
    return x


def _inproj_kernel(x_ref, mod_ref, w_ref, bic_ref, bfc_ref,
                   qa_ref, ka_ref, va_ref, qm_ref, kt_ref, vm_ref, og_ref, ga_ref, gb_ref,
                   ar_ref, bc_ref, ac_ref, *, S, R, L, q_transposed):
    SR = S * R
    sh = mod_ref[:, 0:1, :]
    sc = mod_ref[:, 1:2, :]
    h = (x_ref[...] * (1.0 + sc) + sh).reshape(SR, D_MODEL).astype(BF16)

    def proj(lo, hi):
        return jnp.dot(h, w_ref[:, lo:hi], preferred_element_type=F32)

    q = proj(M_QA, M_KA) * (HEAD_DIM_A ** -0.5)
    if q_transposed:
        qt = (q * LOG2E).T.astype(BF16)
        for s in range(S):
            qa_ref[s] = qt[:, s * R:(s + 1) * R]
    else:
        qa_ref[...] = q.astype(BF16).reshape(S, R, Q_A)
    kv = proj(M_KA, M_QM)
    ka_ref[...] = kv[:, :KV_A_W].reshape(S, R, KV_A_W)
    va_ref[...] = kv[:, KV_A_W:].reshape(S, R, KV_A_W)
    qm_ref[...] = proj(M_QM, M_KM).astype(BF16).reshape(S, R, QK_B)
    kt = (proj(M_KM, M_VM) * (DQK_B ** -0.5)).T.astype(BF16)
    vm_ref[...] = proj(M_VM, M_OG).astype(BF16).reshape(S, R, V_B)
    og_ref[...] = proj(M_OG, M_GA).reshape(S, R, V_B)
    ga_ref[...] = proj(M_GA, M_GB).reshape(S, R, D_MODEL)
    gb_ref[...] = proj(M_GB, M_GI).reshape(S, R, D_MODEL)

    gates = proj(M_GI, M_END)
    i_col = gates[:, :LANES] + bic_ref[...]
    logf_col = _log_sigmoid(gates[:, LANES:] + bfc_ref[...])
    b_col = _seg_scan(logf_col, L, 0, jnp.add, 0.0)
    a_col = _seg_scan(i_col - b_col, L, 0, jnp.maximum, -jnp.inf)
    bc_ref[...] = b_col.reshape(S, R, LANES)
    ac_ref[...] = a_col.reshape(S, R, LANES)
    i_row = i_col.T[:8]
    logf_row = logf_col.T[:8]
    a_row = i_row - _seg_scan(logf_row, L, 1, jnp.add, 0.0)
    for s in range(S):
        kt_ref[s] = kt[:, s * R:(s + 1) * R]
        ar_ref[s] = a_row[:, s * R:(s + 1) * R]


def _inproj_call(x, mod, w_main, b_ic, b_fc, *, S, R, L, q_transposed):
    N, T, _ = x.shape
    nsb, nrb = N // S, T // R

    def tok(c):
        return pl.BlockSpec((S, R, c), lambda i, j: (i, j, 0))

    def tok_shape(c, dt):
        return jax.ShapeDtypeStruct((N, T, c), dt)

    def tr(c):
        return pl.BlockSpec((S, c, R), lambda i, j: (i, 0, j))

    return pl.pallas_call(
        functools.partial(_inproj_kernel, S=S, R=R, L=L, q_transposed=q_transposed),
        grid=(nsb, nrb),
        in_specs=[tok(D_MODEL),
                  pl.BlockSpec((S, 6, D_MODEL), lambda i, j: (i, 0, 0)),
                  _const_spec((D_MODEL, M_END)),
                  _const_spec((1, LANES)), _const_spec((1, LANES))],
        out_specs=[tr(Q_A) if q_transposed else tok(Q_A), tok(KV_A_W), tok(KV_A_W), tok(QK_B), tr(QK_B), tok(V_B),
                   tok(V_B), tok(D_MODEL), tok(D_MODEL), tr(8), tok(LANES), tok(LANES)],
        out_shape=[jax.ShapeDtypeStruct((N, Q_A, T), BF16) if q_transposed else tok_shape(Q_A, BF16),
                   tok_shape(KV_A_W, F32), tok_shape(KV_A_W, F32),
                   tok_shape(QK_B, BF16), jax.ShapeDtypeStruct((N, QK_B, T), BF16), tok_shape(V_B, BF16),
                   tok_shape(V_B, F32), tok_shape(D_MODEL, F32), tok_shape(D_MODEL, F32),
                   jax.ShapeDtypeStruct((N, 8, T), F32), tok_shape(LANES, F32), tok_shape(LANES, F32)],
        compiler_params=pltpu.CompilerParams(dimension_semantics=("parallel", "parallel")),
        name="inproj",
    )(x, mod, w_main, b_ic, b_fc)


def _alibi_slopes():
    return 2.0 ** (-8.0 * np.arange(1, N_HEADS_A + 1, dtype=np.float64) / N_HEADS_A)


def _attn_bias(qpos, kpos, kvalid, transposed=False):
    qpos = np.asarray(qpos)[:, None]
    kpos = np.asarray(kpos)[None, :]
    qc, kc = qpos // CHUNK, kpos // CHUNK
    visible = (kpos >= 0) & (kc <= qc) & (kc >= qc - WINDOW_CHUNKS) & np.asarray(kvalid)[None, :]
    dist = np.abs(qpos - kpos).astype(np.float64)
    slopes = _alibi_slopes()
    out = np.zeros((N_KV_A, GROUP_A // 2, qpos.shape[0], 2, kpos.shape[1]), np.float32)
    for kv in range(N_KV_A):
        for p in range(GROUP_A // 2):
            for half in range(2):
                hd = kv * GROUP_A + 2 * p + half
                out[kv, p, :, half, :] = np.where(visible, -slopes[hd] * dist, -np.inf)
    if transposed:
        return out.transpose(0, 3, 4, 1, 2).reshape(N_KV_A, 2 * kpos.shape[1], (GROUP_A // 2) * qpos.shape[0])
    return out.reshape(N_KV_A, (GROUP_A // 2) * qpos.shape[0], 2 * kpos.shape[1])


def _attn_core(q, kwin, vwin, bias_of, sinks_ref, store):
    Mq = q.shape[0]
    W = kwin.shape[0]
    npair = GROUP_A // 2
    lane = lax.broadcasted_iota(jnp.int32, (npair * Mq, LANES), 1)
    for kv in range(N_KV_A):
        kk = kwin[:, kv * HEAD_DIM_A:(kv + 1) * HEAD_DIM_A].astype(BF16)
        vv = vwin[:, kv * HEAD_DIM_A:(kv + 1) * HEAD_DIM_A].astype(BF16)
        zero = jnp.zeros_like(kk)
        one = jnp.ones_like(vv)
        k2 = jnp.concatenate([jnp.concatenate([kk, zero], axis=1),
                              jnp.concatenate([zero, kk], axis=1)], axis=0)
        v_aug = jnp.concatenate([vv, one, one, vv], axis=1)
        qs = jnp.concatenate([q[:, (kv * npair + p) * LANES:(kv * npair + p + 1) * LANES]
                              for p in range(npair)], axis=0)
        s = _bdot_nt(qs, k2) + bias_of(kv)
        probs, sink_terms = [], []
        for half in range(2):
            sh = s[:, half * W:(half + 1) * W]
            sink = jnp.concatenate(
                [jnp.full((Mq, 1), sinks_ref[kv * GROUP_A + 2 * p + half], F32) for p in range(npair)], axis=0)
            mx = jnp.maximum(jnp.max(sh, axis=1, keepdims=True), sink)
            probs.append(jnp.exp(sh - mx).astype(BF16))
            sink_terms.append(jnp.exp(sink - mx))
        o = jnp.dot(jnp.concatenate(probs, axis=0), v_aug, preferred_element_type=F32)
        oe, oo = o[:npair * Mq], o[npair * Mq:]
        ye = oe[:, :LANES] / (oe[:, LANES:] + sink_terms[0])
        yo = oo[:, LANES:] / (oo[:, :LANES] + sink_terms[1])
        y = jnp.where(lane < HEAD_DIM_A, ye, yo).astype(BF16)
        for p in range(npair):
            store((kv * npair + p) * LANES, y[p * Mq:(p + 1) * Mq])


def _attn_prompt_kernel(sinks_ref, qt_ref, kp_ref, kc_ref, vp_ref, vc_ref, bias0_ref, bias1_ref, o_ref):
    Mq = ATT_QB
    npair = GROUP_A // 2
    keys = jnp.concatenate([kp_ref[0], kc_ref[0]], axis=0)
    vals_t = jnp.concatenate([vp_ref[0], vc_ref[0]], axis=0).T.astype(BF16)
    ones = jnp.ones((HEAD_DIM_A, ATT_W), BF16)
    pair_of_lane = lax.broadcasted_iota(jnp.int32, (1, npair * Mq), 1) // Mq
    units = [(sub, kv) for sub in range(ATT_SUB) for kv in range(N_KV_A)]
    sts = {}
    for sub, kv in units:
        kk = keys[sub * Mq:sub * Mq + ATT_W, kv * HEAD_DIM_A:(kv + 1) * HEAD_DIM_A].astype(BF16)
        zero = jnp.zeros_like(kk)
        k2 = jnp.concatenate([jnp.concatenate([kk, zero], axis=1),
                              jnp.concatenate([zero, kk], axis=1)], axis=0)
        qt = jnp.concatenate([qt_ref[0, (kv * npair + p) * LANES:(kv * npair + p + 1) * LANES, sub * Mq:(sub + 1) * Mq]
                              for p in range(npair)], axis=1)
        sts[sub, kv] = jnp.dot(k2, qt, preferred_element_type=F32)
    probs, sink_terms = {}, {}
    for sub, kv in units:
        st = sts[sub, kv] + (bias0_ref[0, kv] if sub == 0 else bias1_ref[0, kv])
        for half in range(2):
            sh = st[half * ATT_W:(half + 1) * ATT_W]
            sink = jnp.zeros((1, npair * Mq), F32)
            for p in range(npair):
                sink = jnp.where(pair_of_lane == p, sinks_ref[kv * GROUP_A + 2 * p + half] * LOG2E, sink)
            mx = jnp.maximum(jnp.max(sh, axis=0, keepdims=True), sink)
            probs[sub, kv, half] = jnp.exp2(sh - mx).astype(BF16)
            sink_terms[sub, kv, half] = jnp.exp2(sink - mx)
    ots = {}
    for sub, kv in units:
        vt = vals_t[kv * HEAD_DIM_A:(kv + 1) * HEAD_DIM_A, sub * Mq:sub * Mq + ATT_W]
        for half in range(2):
            lhs = jnp.concatenate([vt, ones] if half == 0 else [ones, vt], axis=0)
            ots[sub, kv, half] = jnp.dot(lhs, probs[sub, kv, half], preferred_element_type=F32)
    for sub, kv in units:
        oe, oo = ots[sub, kv, 0], ots[sub, kv, 1]
        ye = oe[:HEAD_DIM_A] / (oe[HEAD_DIM_A:] + sink_terms[sub, kv, 0])
        yo = oo[HEAD_DIM_A:] / (oo[:HEAD_DIM_A] + sink_terms[sub, kv, 1])
        yt = jnp.concatenate([ye, yo], axis=0)
        for p in range(npair):
            col = (kv * npair + p) * LANES
            o_ref[0, sub * Mq:(sub + 1) * Mq, col:col + LANES] = yt[:, p * Mq:(p + 1) * Mq].T.astype(BF16)


def _attn_prompt_call(qa_t, ka, va, sinks):
    N, _, T = qa_t.shape
    rows = ATT_QB * ATT_SUB
    rel_q = ATT_QB + np.arange(ATT_QB)
    rel_k = np.arange(ATT_W)
    bias = np.stack([_attn_bias(rel_q, rel_k, rel_k >= ATT_QB, transposed=True),
                     _attn_bias(rel_q, rel_k, rel_k >= 0, transposed=True)]) * np.float32(LOG2E)
    kv_prev = pl.BlockSpec((1, ATT_QB, KV_A_W), lambda n, j: (n, jnp.maximum(ATT_SUB * j - 1, 0), 0))
    kv_cur = pl.BlockSpec((1, rows, KV_A_W), lambda n, j: (n, j, 0))
    return pl.pallas_call(
        _attn_prompt_kernel,
        grid=(N, T // rows),
        in_specs=[pl.BlockSpec(memory_space=pltpu.SMEM),
                  pl.BlockSpec((1, Q_A, rows), lambda n, j: (n, 0, j)),
                  kv_prev, kv_cur, kv_prev, kv_cur,
                  pl.BlockSpec((1,) + bias.shape[1:], lambda n, j: (jnp.minimum(j, 1), 0, 0, 0)),
                  pl.BlockSpec((1,) + bias.shape[1:], lambda n, j: (1, 0, 0, 0), pipeline_mode=pl.Buffered(1))],
        out_specs=pl.BlockSpec((1, rows, Q_A), lambda n, j: (n, j, 0)),
        out_shape=jax.ShapeDtypeStruct((N, T, Q_A), BF16),
        compiler_params=pltpu.CompilerParams(dimension_semantics=("parallel", "parallel")),
        name="attn_prompt",
    )(sinks, qa_t, ka, ka, va, va, jnp.asarray(bias), jnp.asarray(bias))


def _attn_sample_kernel(sinks_ref, q_ref, kn_ref, vn_ref, kc_ref, vc_ref, bias_ref,
                        o_ref, ko_ref, vo_ref, *, T, NS):
    pad = jnp.zeros((ATT_W - KV_WIN - T, KV_A_W), F32)
    for s in range(NS):
        kwin = jnp.concatenate([kc_ref[s], kn_ref[s], pad], axis=0)
        vwin = jnp.concatenate([vc_ref[s], vn_ref[s], pad], axis=0)
        ko_ref[s] = kwin[T:T + KV_WIN]
        vo_ref[s] = vwin[T:T + KV_WIN]

        def store(col, val, s=s):
            o_ref[s, :, col:col + LANES] = val

        _attn_core(q_ref[s], kwin, vwin, lambda kv: bias_ref[kv], sinks_ref, store)


def _attn_sample_call(qa, ka, va, cache_k, cache_v, sinks, *, NS):
    N, T, _ = qa.shape
    qpos = PAST_LEN + np.arange(T)
    kpos = PAST_LEN - KV_WIN + np.arange(ATT_W)
    bias = _attn_bias(qpos, kpos, np.arange(ATT_W) < KV_WIN + T)
    new = pl.BlockSpec((NS, T, KV_A_W), lambda n: (n, 0, 0))
    cache = pl.BlockSpec((NS, KV_WIN, KV_A_W), lambda n: (n, 0, 0))
    cache_shape = jax.ShapeDtypeStruct((N, KV_WIN, KV_A_W), F32)
    return pl.pallas_call(
        functools.partial(_attn_sample_kernel, T=T, NS=NS),
        grid=(N // NS,),
        in_specs=[pl.BlockSpec(memory_space=pltpu.SMEM),
                  pl.BlockSpec((NS, T, Q_A), lambda n: (n, 0, 0)),
                  new, new, cache, cache,
                  _const_spec(bias.shape)],
        out_specs=[pl.BlockSpec((NS, T, Q_A), lambda n: (n, 0, 0)), cache, cache],
        out_shape=[jax.ShapeDtypeStruct((N, T, Q_A), BF16), cache_shape, cache_shape],
        compiler_params=pltpu.CompilerParams(dimension_semantics=("parallel",)),
        name="attn_sample",
    )(sinks, qa, ka, va, cache_k, cache_v, jnp.asarray(bias))


def _mlstm_kernel(q_ref, kt_ref, v_ref, og_ref, ar_ref, bc_ref, ac_ref, nw_ref, c0_ref, n0_ref, m0_ref,
                  y_ref, c_ref, n_ref, m_ref, ct_ref, *, SB, L):
    j = pl.program_id(1)
    heads = [(sb, hd) for sb in range(SB) for hd in range(N_HEADS_B)]

    @pl.when(j == 0)
    def _():
        for sb, hd in heads:
            ct_ref[sb, hd, :, :DV_B] = c0_ref[sb, hd].T
            ct_ref[sb, hd, :, DV_B:] = jnp.broadcast_to(n0_ref[sb, hd], (DQK_B, LANES))
        m_ref[...] = m0_ref[...]

    causal = (lax.broadcasted_iota(jnp.int32, (L, L), 0) >= lax.broadcasted_iota(jnp.int32, (L, L), 1))
    ones = jnp.ones((L, LANES), BF16)
    gates = []
    for sb in range(SB):
        a4 = ac_ref[sb]
        b4 = bc_ref[sb]
        m_row = m_ref[sb]
        a_last = a4[L - 1:L]
        b_last = b4[L - 1:L]
        mx = jnp.maximum(a_last, m_row)
        m_new = b_last + mx
        gates.append(dict(a4=a4, b4=b4, a_last=a_last, m_row=m_row, decay4=jnp.exp(b_last + m_row - m_new),
                          rho4=jnp.exp(a_last - mx)))
        m_ref[sb] = m_new

    def lanes(x):
        return jnp.broadcast_to(x, (x.shape[0], LANES))

    def col(x, hd):
        return x[:, hd:hd + 1]

    qs = {h: q_ref[h[0], :, h[1] * DQK_B:(h[1] + 1) * DQK_B] for h in heads}
    kts = {h: kt_ref[h[0], h[1] * DQK_B:(h[1] + 1) * DQK_B, :] for h in heads}
    cts = {h: ct_ref[h[0], h[1]] for h in heads}
    qk = {h: jnp.dot(qs[h], kts[h], preferred_element_type=F32) for h in heads}
    qc = {h: jnp.dot(qs[h], cts[h].astype(BF16), preferred_element_type=F32) for h in heads}
    lhs, a_bs = {}, {}
    for h in heads:
        sb, hd = h
        gt = gates[sb]
        a_row = ar_ref[sb, hd:hd + 1, :]
        a_bs[h] = lanes(col(gt["a4"], hd))
        a_full = jnp.concatenate([a_bs[h]] * (L // LANES), axis=1) if L >= LANES else a_bs[h][:, :L]
        w = jnp.exp(jnp.where(causal, a_row - a_full, -jnp.inf))
        s = (qk[h] * w).astype(BF16)
        kw = (kts[h].astype(F32) * jnp.exp(a_row - col(gt["a_last"], hd))).astype(BF16)
        lhs[h] = jnp.concatenate([s, kw], axis=0)
    pu = {}
    for h in heads:
        sb, hd = h
        v_aug = jnp.concatenate([v_ref[sb, :, hd * DV_B:(hd + 1) * DV_B], ones], axis=1)
        pu[h] = jnp.dot(lhs[h], v_aug, preferred_element_type=F32)
    ones_sum = jnp.ones((DV_B, LANES), BF16)
    for h in heads:
        sb, hd = h
        gt = gates[sb]
        a_b = a_bs[h]
        m_b = lanes(col(gt["m_row"], hd))
        big = jnp.maximum(a_b, m_b)
        r_b = jnp.exp(a_b - big)
        g_b = jnp.exp(m_b - big)
        e_b = jnp.exp(-(lanes(col(gt["b4"], hd)) + big))
        p, c = pu[h], qc[h]
        den = r_b * p[:L, DV_B:] + g_b * c[:, DV_B:]
        inv = 1.0 / jnp.maximum(jnp.abs(den), e_b)
        hh = jnp.concatenate([(r_b * p[:L, k * LANES:(k + 1) * LANES] + g_b * c[:, k * LANES:(k + 1) * LANES]) * inv
                              for k in range(DV_B // LANES)], axis=1)
        mu = jnp.dot(hh.astype(BF16), ones_sum, preferred_element_type=F32) * (1.0 / DV_B)
        hc = hh - jnp.concatenate([mu] * (DV_B // LANES), axis=1)
        var = jnp.dot(jnp.square(hc).astype(BF16), ones_sum, preferred_element_type=F32) * (1.0 / DV_B)
        rstd = lax.rsqrt(var + LN_EPS)
        yn = hc * jnp.concatenate([rstd] * (DV_B // LANES), axis=1) * nw_ref[:, hd * DV_B:(hd + 1) * DV_B]
        og = og_ref[sb, :, hd * DV_B:(hd + 1) * DV_B]
        y_ref[sb, :, hd * DV_B:(hd + 1) * DV_B] = (jax.nn.sigmoid(og) * yn).astype(BF16)
        ct_ref[sb, hd] = col(gt["decay4"], hd) * cts[h] + col(gt["rho4"], hd) * pu[h][L:]

    @pl.when(j == pl.num_programs(1) - 1)
    def _():
        for sb, hd in heads:
            c_ref[sb, hd] = ct_ref[sb, hd, :, :DV_B].T
            n_ref[sb, hd] = ct_ref[sb, hd, :, DV_B:DV_B + 1]


def _mlstm_call(qm, kt, vm, og, a_row, b_col, a_col, norm_w, c0, n0, m0, *, SB, L):
    N, T, _ = qm.shape

    def tok(c):
        return pl.BlockSpec((SB, L, c), lambda i, j: (i, j, 0))

    def tr(c):
        return pl.BlockSpec((SB, c, L), lambda i, j: (i, 0, j))

    st_c = pl.BlockSpec((SB, N_HEADS_B, DV_B, DQK_B), lambda i, j: (i, 0, 0, 0))
    st_n = pl.BlockSpec((SB, N_HEADS_B, DQK_B, 1), lambda i, j: (i, 0, 0, 0))
    st_m = pl.BlockSpec((SB, 1, LANES), lambda i, j: (i, 0, 0))
    return pl.pallas_call(
        functools.partial(_mlstm_kernel, SB=SB, L=L),
        grid=(N // SB, T // L),
        in_specs=[tok(QK_B), tr(QK_B), tok(V_B), tok(V_B), tr(8), tok(LANES), tok(LANES),
                  _const_spec((1, V_B)), st_c, st_n, st_m],
        out_specs=[tok(V_B), st_c, st_n, st_m],
        out_shape=[jax.ShapeDtypeStruct((N, T, V_B), BF16),
                   jax.ShapeDtypeStruct((N, N_HEADS_B, DV_B, DQK_B), F32),
                   jax.ShapeDtypeStruct((N, N_HEADS_B, DQK_B, 1), F32),
                   jax.ShapeDtypeStruct((N, 1, LANES), F32)],
        scratch_shapes=[pltpu.VMEM((SB, N_HEADS_B, DQK_B, DV_B + LANES), F32)],
        compiler_params=pltpu.CompilerParams(dimension_semantics=("parallel", "arbitrary")),
        name="mlstm",
    )(qm, kt, vm, og, a_row, b_col, a_col, norm_w, c0, n0, m0)


def _merge_kernel(x_ref, mod_ref, ya_ref, yb_ref, ga_ref, gb_ref,
                  wpa_ref, wpb_ref, wo_ref, l1w_ref, l1b_ref, x1_ref, *, S, R):
    SR = S * R
    g1 = mod_ref[:, 2:3, :]
    pa = jnp.dot(ya_ref[...].reshape(SR, Q_A), wpa_ref[...], preferred_element_type=F32)
    pb = jnp.dot(yb_ref[...].reshape(SR, V_B), wpb_ref[...], preferred_element_type=F32)
    merged = (jax.nn.sigmoid(ga_ref[...].reshape(SR, D_MODEL)) * pa
              + jax.nn.sigmoid(gb_ref[...].reshape(SR, D_MODEL)) * pb)
    mo = _bdot(merged, wo_ref[...]).reshape(S, R, D_MODEL)
    x1_ref[...] = _layer_norm(ALPHA * x_ref[...] + g1 * mo, l1w_ref[...], l1b_ref[...])


def _merge_call(x, mod, ya, yb, ga, gb, wpa, wpb, wo, l1w, l1b, *, S, R):
    N, T, _ = x.shape

    def tok(c):
        return pl.BlockSpec((S, R, c), lambda i, j: (i, j, 0))

    return pl.pallas_call(
        functools.partial(_merge_kernel, S=S, R=R),
        grid=(N // S, T // R),
        in_specs=[tok(D_MODEL), pl.BlockSpec((S, 6, D_MODEL), lambda i, j: (i, 0, 0)),
                  tok(Q_A), tok(V_B), tok(D_MODEL), tok(D_MODEL),
                  _const_spec((Q_A, D_MODEL)), _const_spec((V_B, D_MODEL)), _const_spec((D_MODEL, D_MODEL)),
                  _const_spec((1, D_MODEL)), _const_spec((1, D_MODEL))],
        out_specs=tok(D_MODEL),
        out_shape=jax.ShapeDtypeStruct((N, T, D_MODEL), F32),
        compiler_params=pltpu.CompilerParams(dimension_semantics=("parallel", "parallel")),
        name="merge",
    )(x, mod, ya, yb, ga, gb, wpa, wpb, wo, l1w, l1b)


def _ffn_kernel(x1_ref, mod_ref, cin_ref, wup_ref, bup_ref, cw_ref, cb_ref, wdn_ref, l2w_ref, l2b_ref,
                y_ref, cout_ref, carry_ref, act_ref, *perm_refs, S, R, permute):
    SR = S * R
    FC = D_FF // FFN_CHUNKS
    NV = R // SUBLANES
    NT = D_MODEL // LANES

    @pl.when(pl.program_id(1) == 0)
    def _():
        carry_ref[...] = cin_ref[...] - bup_ref[...]

    sh2 = mod_ref[:, 3:4, :]
    sc2 = mod_ref[:, 4:5, :]
    g2 = mod_ref[:, 5:6, :]

    def regroup(src_ref, start_of, stride):
        return jnp.concatenate(
            [jnp.concatenate([src_ref[s, c, pl.ds(start_of(v), SUBLANES, stride=stride), :] for c in range(NT)],
                             axis=1) for s in range(S) for v in range(NV)], axis=0).reshape(S, R, D_MODEL)

    if permute:
        (tiles_ref,) = perm_refs
        for c in range(NT):
            tiles_ref[:, c] = x1_ref[:, :, c * LANES:(c + 1) * LANES]
        x1 = regroup(tiles_ref, lambda v: v, NV)
    else:
        x1 = x1_ref[...]
    h2 = (x1 * (1.0 + sc2) + sh2).reshape(SR, D_MODEL).astype(BF16)
    row8 = lax.broadcasted_iota(jnp.int32, (SUBLANES, FC), 0)

    def wrap(group, first):
        return jnp.where(row8 == 0, first, pltpu.roll(group, 1, axis=0))

    def back_permuted(blk, c0, c1):
        g_last = wrap(blk[(NV - 1) * SUBLANES:], c1)
        g_prev = wrap(blk[(NV - 2) * SUBLANES:(NV - 1) * SUBLANES], c0)
        new = jnp.concatenate([blk[(NV - 1) * SUBLANES - 1:(NV - 1) * SUBLANES], blk[R - 1:R]], axis=0)
        return ([g_last, blk[:(NV - 1) * SUBLANES]], [g_prev, g_last, blk[:(NV - 2) * SUBLANES]], new)

    def back_natural(blk, c0, c1):
        b1 = pltpu.roll(blk, 1, axis=0)
        b1 = [jnp.where(row8 == 0, c1, b1[:SUBLANES]), b1[SUBLANES:]]
        b2 = pltpu.roll(blk, 2, axis=0)
        b2 = [jnp.where(row8 == 0, c0, jnp.where(row8 == 1, c1, b2[:SUBLANES])), b2[SUBLANES:]]
        return b1, b2, blk[R - 2:]

    def conv(lo):
        cs = slice(lo, lo + FC)
        u = jnp.dot(h2, wup_ref[:, cs], preferred_element_type=F32)
        w0, w1, w2 = cw_ref[0:1, cs], cw_ref[1:2, cs], cw_ref[2:3, cs]
        const = cb_ref[:, cs] + bup_ref[:, cs] * (w0 + w1 + w2)
        back1, back2 = [], []
        for s in range(S):
            b1, b2, new = (back_permuted if permute else back_natural)(
                u[s * R:(s + 1) * R], carry_ref[s, 0:1, cs], carry_ref[s, 1:2, cs])
            back1 += b1
            back2 += b2
            carry_ref[s, :, cs] = new
            cout_ref[s, :, cs] = new + bup_ref[:, cs]
        return w0 * jnp.concatenate(back2, axis=0) + w1 * jnp.concatenate(back1, axis=0) + w2 * u + const

    for ci in range(FFN_CHUNKS):
        lo = ci * FC
        a = conv(lo)
        g = conv(D_FF + lo)
        act_ref[:, lo:lo + FC] = (a * jax.nn.gelu(g, approximate=True)).astype(BF16)

    f = jnp.dot(act_ref[...], wdn_ref[...], preferred_element_type=F32).reshape(S, R, D_MODEL)
    y = _layer_norm(ALPHA * x1 + g2 * f, l2w_ref[...], l2b_ref[...])
    if permute:
        for c in range(NT):
            tiles_ref[:, c] = y[:, :, c * LANES:(c + 1) * LANES]
        y = regroup(tiles_ref, lambda v: ((SUBLANES * v) % NV) * SUBLANES + (SUBLANES * v) // NV, SUBLANES)
    y_ref[...] = y


def _ffn_call(x1, mod, conv_in, wup, bup, cw, cb, wdn, l2w, l2b, *, S, R):
    N, T, _ = x1.shape
    permute = (R // SUBLANES) % SUBLANES == 0

    def tok(c):
        return pl.BlockSpec((S, R, c), lambda i, j: (i, j, 0))

    conv_spec = pl.BlockSpec((S, CONV_W - 1, 2 * D_FF), lambda i, j: (i, 0, 0))
    return pl.pallas_call(
        functools.partial(_ffn_kernel, S=S, R=R, permute=permute),
        grid=(N // S, T // R),
        in_specs=[tok(D_MODEL), pl.BlockSpec((S, 6, D_MODEL), lambda i, j: (i, 0, 0)), conv_spec,
                  _const_spec((D_MODEL, 2 * D_FF)), _const_spec((1, 2 * D_FF)),
                  _const_spec((CONV_W, 2 * D_FF)), _const_spec((1, 2 * D_FF)),
                  _const_spec((D_FF, D_MODEL)), _const_spec((1, D_MODEL)), _const_spec((1, D_MODEL))],
        out_specs=[tok(D_MODEL), conv_spec],
        out_shape=[jax.ShapeDtypeStruct((N, T, D_MODEL), F32),
                   jax.ShapeDtypeStruct((N, CONV_W - 1, 2 * D_FF), F32)],
        scratch_shapes=[pltpu.VMEM((S, CONV_W - 1, 2 * D_FF), F32), pltpu.VMEM((S * R, D_FF), BF16)]
        + ([pltpu.VMEM((S, D_MODEL // LANES, R, LANES), F32)] if permute else []),
        compiler_params=pltpu.CompilerParams(dimension_semantics=("parallel", "arbitrary")),
        name="ffn",
    )(x1, mod, conv_in, wup, bup, cw, cb, wdn, l2w, l2b)


def kernel(x_prompt, x_sample, cache_attn_k, cache_attn_v, state_mlstm_C, state_mlstm_n, state_mlstm_m, state_ffn_conv, c_prompt, c_sample, w_ada, b_ada, w_in, b_igate, b_fgate, attn_sinks, mlstm_norm_w, w_proj_a, w_proj_b, w_out, ln1_w, ln1_b, w_up, b_up, conv_w, conv_b, w_down, ln2_w, ln2_b):
    Bp, Tp, D = x_prompt.shape
    Bs, Ts, _ = x_sample.shape
    l = 0

    mod = _mod_call(jnp.concatenate([c_prompt, c_sample], axis=0), w_ada[l], b_ada[l][None])
    mod_p = mod[:Bp].reshape(Bp, 6, D)
    mod_s = mod[Bp:].reshape(Bs, 6, D)

    w = w_in[l]
    o_i, o_f, o_o = Z_OFF[6], Z_OFF[7], Z_OFF[8]
    lane_pad = ((0, 0), (0, LANES - N_HEADS_B))
    w_main = jnp.concatenate([w[:, :o_i], w[:, o_o:],
                              jnp.pad(w[:, o_i:o_f], lane_pad), jnp.pad(w[:, o_f:o_o], lane_pad)],
                             axis=1).astype(BF16)
    b_ic = jnp.pad(b_igate[l][None], lane_pad)
    b_fc = jnp.pad(b_fgate[l][None], lane_pad)
    inproj_w = (w_main, b_ic, b_fc)
    sinks = attn_sinks[l]
    norm_w = mlstm_norm_w[l][None]
    merge_w = (w_proj_a[l].astype(BF16), w_proj_b[l].astype(BF16), w_out[l].astype(BF16),
               ln1_w[l][None], ln1_b[l][None])
    ffn_w = (w_up[l].astype(BF16), b_up[l][None], conv_w[l], conv_b[l][None],
             w_down[l].astype(BF16), ln2_w[l][None], ln2_b[l][None])

    qa, ka, va, qm, kt, vm, og, ga, gb, a_row, b_col, a_col = _inproj_call(
        x_prompt, mod_p, *inproj_w, S=1, R=INPROJ_ROWS, L=MLSTM_CHUNK, q_transposed=True)
    ya = _attn_prompt_call(qa, ka, va, sinks)
    yb, p_c, p_n, p_m = _mlstm_call(
        qm, kt, vm, og, a_row, b_col, a_col, norm_w,
        jnp.zeros((Bp, N_HEADS_B, DV_B, DQK_B), F32), jnp.zeros((Bp, N_HEADS_B, DQK_B, 1), F32),
        jnp.zeros((Bp, 1, LANES), F32), SB=4, L=MLSTM_CHUNK)
    p_m = p_m[:, 0, :N_HEADS_B]
    x1 = _merge_call(x_prompt, mod_p, ya, yb, ga, gb, *merge_w, S=1, R=POST_ROWS)
    y_prompt, p_conv = _ffn_call(x1, mod_p, jnp.zeros((Bp, CONV_W - 1, 2 * D_FF), F32), *ffn_w, S=1, R=POST_ROWS)
    p_k = ka[:, Tp - KV_WIN:].reshape(Bp, KV_WIN, N_KV_A, HEAD_DIM_A)
    p_v = va[:, Tp - KV_WIN:].reshape(Bp, KV_WIN, N_KV_A, HEAD_DIM_A)

    qa, ka, va, qm, kt, vm, og, ga, gb, a_row, b_col, a_col = _inproj_call(
        x_sample, mod_s, *inproj_w, S=16, R=Ts, L=Ts, q_transposed=False)
    ya, s_k, s_v = _attn_sample_call(qa, ka, va, cache_attn_k[l].reshape(Bs, KV_WIN, KV_A_W),
                                     cache_attn_v[l].reshape(Bs, KV_WIN, KV_A_W), sinks, NS=8)
    yb, s_c, s_n, s_m = _mlstm_call(
        qm, kt, vm, og, a_row, b_col, a_col, norm_w,
        state_mlstm_C[l], state_mlstm_n[l][:, :, :, None], jnp.pad(state_mlstm_m[l][:, None, :], ((0, 0),) + lane_pad),
        SB=4, L=Ts)
    s_m = s_m[:, 0, :N_HEADS_B]
    x1 = _merge_call(x_sample, mod_s, ya, yb, ga, gb, *merge_w, S=16, R=Ts)
    y_sample, s_conv = _ffn_call(x1, mod_s, state_ffn_conv[l], *ffn_w, S=8, R=Ts)

    def st(a, shape):
        return a.reshape((1,) + shape)

    return (y_prompt, y_sample,
            st(p_k, (Bp, KV_WIN, N_KV_A, HEAD_DIM_A)), st(p_v, (Bp, KV_WIN, N_KV_A, HEAD_DIM_A)),
            st(p_c, (Bp, N_HEADS_B, DV_B, DQK_B)), st(p_n, (Bp, N_HEADS_B, DQK_B)), st(p_m, (Bp, N_HEADS_B)),
            st(p_conv, (Bp, CONV_W - 1, 2 * D_FF)),
            st(s_k, (Bs, KV_WIN, N_KV_A, HEAD_DIM_A)), st(s_v, (Bs, KV_WIN, N_KV_A, HEAD_DIM_A)),
            st(s_c, (Bs, N_HEADS_B, DV_B, DQK_B)), st(s_n, (Bs, N_HEADS_B, DQK_B)), st(s_m, (Bs, N_HEADS_B)),
            st(s_conv, (Bs, CONV_W - 1, 2 * D_FF)))
```

```python
import functools

import numpy as np
import jax
import jax.numpy as jnp
from jax import lax
from jax.experimental import pallas as pl
from jax.experimental.pallas import tpu as pltpu

F32 = jnp.float32
BF16 = jnp.bfloat16

D_MODEL = 1024
DEPTH = 1
PAST_LEN = 2048
CHUNK = 64
N_HEADS_A = 16
N_KV_A = 2
HEAD_DIM_A = 64
GROUP_A = N_HEADS_A // N_KV_A
WINDOW = 128
WINDOW_CHUNKS = WINDOW // CHUNK
KV_WIN = min(WINDOW, PAST_LEN)
N_HEADS_B = 4
DQK_B = 128
DV_B = 256
D_FF = 2816
CONV_W = 3
LN_EPS = 1e-5
ALPHA = (2 * DEPTH) ** 0.25
Q_A = N_HEADS_A * HEAD_DIM_A
KV_A_W = N_KV_A * HEAD_DIM_A
QK_B = N_HEADS_B * DQK_B
V_B = N_HEADS_B * DV_B
Z_PARTS = (Q_A, KV_A_W, KV_A_W, QK_B, QK_B, V_B, N_HEADS_B, N_HEADS_B, V_B, D_MODEL, D_MODEL)
Z_OFF = tuple(int(v) for v in np.cumsum((0,) + Z_PARTS))

LANES = 128
SUBLANES = 8
LOG2E = float(np.log2(np.e))

M_QA = 0
M_KA = M_QA + Q_A
M_VA = M_KA + KV_A_W
M_QM = M_VA + KV_A_W
M_KM = M_QM + QK_B
M_VM = M_KM + QK_B
M_OG = M_VM + V_B
M_GA = M_OG + V_B
M_GB = M_GA + D_MODEL
M_GI = M_GB + D_MODEL
M_GF = M_GI + LANES
M_END = M_GF + LANES
ATT_QB = 128
ATT_SUB = 8
ATT_W = 256
FFN_CHUNKS = 11
MLSTM_CHUNK = 256
INPROJ_ROWS = 512
POST_ROWS = 512
EPI_SPLIT = 2


def _bdot(a, b):
    return jnp.dot(a.astype(BF16), b.astype(BF16), preferred_element_type=F32)


def _bdot_nt(a, b):
    return lax.dot_general(a.astype(BF16), b.astype(BF16), (((1,), (1,)), ((), ())),
                           preferred_element_type=F32)


def _const_spec(shape):
    nd = len(shape)
    return pl.BlockSpec(shape, lambda *_: (0,) * nd, pipeline_mode=pl.Buffered(1))


def _layer_norm(x, w, b):
    mu = jnp.mean(x, -1, keepdims=True)
    xc = x - mu
    var = jnp.mean(jnp.square(xc), -1, keepdims=True)
    return xc * lax.rsqrt(var + LN_EPS) * w + b


def _mod_kernel(c_ref, w_ref, b_ref, o_ref):
    c = c_ref[...]
    s = c * jax.nn.sigmoid(c)
    o_ref[...] = _bdot(s, w_ref[...]) + b_ref[...]


def _mod_call(c_all, w_ada, b_ada):
    n = c_all.shape[0]
    tn = 1536
    return pl.pallas_call(
        _mod_kernel,
        grid=(6 * D_MODEL // tn,),
        in_specs=[pl.BlockSpec((n, D_MODEL), lambda j: (0, 0)),
                  pl.BlockSpec((D_MODEL, tn), lambda j: (0, j)),
                  pl.BlockSpec((1, tn), lambda j: (0, j))],
        out_specs=pl.BlockSpec((n, tn), lambda j: (0, j)),
        out_shape=jax.ShapeDtypeStruct((n, 6 * D_MODEL), F32),
        name="mod",
    )(c_all, w_ada, b_ada)


def _log_sigmoid(x):
    return jnp.minimum(x, 0.0) - jnp.log1p(jnp.exp(-jnp.abs(x)))


def _seg_scan(x, L, axis, op, ident):
    pos = lax.broadcasted_iota(jnp.int32, x.shape, axis) % L
    step = 1
    while step < L:
        x = op(x, jnp.where(pos >= step, pltpu.roll(x, step, axis=axis), ident))
        step *= 2
    return x


def _inproj_kernel(x_ref, mod_ref, w_ref, bic_ref, bfc_ref,
                   qa_ref, ka_ref, va_ref, qm_ref, kt_ref, vm_ref, og_ref, ga_ref, gb_ref,
                   ar_ref, bc_ref, ac_ref, *, S, R, L, q_transposed):
    SR = S * R
    sh = mod_ref[:, 0:1, :]
    sc = mod_ref[:, 1:2, :]
    h = (x_ref[...] * (1.0 + sc) + sh).reshape(SR, D_MODEL).astype(BF16)

    def proj(lo, hi):
        return jnp.dot(h, w_ref[:, lo:hi], preferred_element_type=F32)

    q = proj(M_QA, M_KA) * (HEAD_DIM_A ** -0.5)
    if q_transposed:
        qt = (q * LOG2E).T.astype(BF16)
        for s in range(S):
            qa_ref[s] = qt[:, s * R:(s + 1) * R]
    else:
        qa_ref[...] = q.astype(BF16).reshape(S, R, Q_A)
    kv = proj(M_KA, M_QM)
    ka_ref[...] = kv[:, :KV_A_W].reshape(S, R, KV_A_W)
    va_ref[...] = kv[:, KV_A_W:].reshape(S, R, KV_A_W)
    qm_ref[...] = proj(M_QM, M_KM).astype(BF16).reshape(S, R, QK_B)
    kt = (proj(M_KM, M_VM) * (DQK_B ** -0.5)).T.astype(BF16)
    vm_ref[...] = proj(M_VM, M_OG).astype(BF16).reshape(S, R, V_B)
    og_ref[...] = proj(M_OG, M_GA).reshape(S, R, V_B)
    ga_ref[...] = proj(M_GA, M_GB).reshape(S, R, D_MODEL)
    gb_ref[...] = proj(M_GB, M_GI).reshape(S, R, D_MODEL)

    gates = proj(M_GI, M_END)
    i_col = gates[:, :LANES] + bic_ref[...]
    logf_col = _log_sigmoid(gates[:, LANES:] + bfc_ref[...])
    b_col = _seg_scan(logf_col, L, 0, jnp.add, 0.0)
    a_col = _seg_scan(i_col - b_col, L, 0, jnp.maximum, -jnp.inf)
    bc_ref[...] = b_col.reshape(S, R, LANES)
    ac_ref[...] = a_col.reshape(S, R, LANES)
    i_row = i_col.T[:8]
    logf_row = logf_col.T[:8]
    a_row = i_row - _seg_scan(logf_row, L, 1, jnp.add, 0.0)
    for s in range(S):
        kt_ref[s] = kt[:, s * R:(s + 1) * R]
        ar_ref[s] = a_row[:, s * R:(s + 1) * R]


def _inproj_call(x, mod, w_main, b_ic, b_fc, *, S, R, L, q_transposed):
    N, T, _ = x.shape
    nsb, nrb = N // S, T // R

    def tok(c):
        return pl.BlockSpec((S, R, c), lambda i, j: (i, j, 0))

    def tok_shape(c, dt):
        return jax.ShapeDtypeStruct((N, T, c), dt)

    def tr(c):
        return pl.BlockSpec((S, c, R), lambda i, j: (i, 0, j))

    return pl.pallas_call(
        functools.partial(_inproj_kernel, S=S, R=R, L=L, q_transposed=q_transposed),
        grid=(nsb, nrb),
        in_specs=[tok(D_MODEL),
                  pl.BlockSpec((S, 6, D_MODEL), lambda i, j: (i, 0, 0)),
                  _const_spec((D_MODEL, M_END)),
                  _const_spec((1, LANES)), _const_spec((1, LANES))],
        out_specs=[tr(Q_A) if q_transposed else tok(Q_A), tok(KV_A_W), tok(KV_A_W), tok(QK_B), tr(QK_B), tok(V_B),
                   tok(V_B), tok(D_MODEL), tok(D_MODEL), tr(8), tok(LANES), tok(LANES)],
        out_shape=[jax.ShapeDtypeStruct((N, Q_A, T), BF16) if q_transposed else tok_shape(Q_A, BF16),
                   tok_shape(KV_A_W, F32), tok_shape(KV_A_W, F32),
                   tok_shape(QK_B, BF16), jax.ShapeDtypeStruct((N, QK_B, T), BF16), tok_shape(V_B, BF16),
                   tok_shape(V_B, F32), tok_shape(D_MODEL, F32), tok_shape(D_MODEL, F32),
                   jax.ShapeDtypeStruct((N, 8, T), F32), tok_shape(LANES, F32), tok_shape(LANES, F32)],
        compiler_params=pltpu.CompilerParams(dimension_semantics=("parallel", "parallel")),
        name="inproj",
    )(x, mod, w_main, b_ic, b_fc)


def _alibi_slopes():
    return 2.0 ** (-8.0 * np.arange(1, N_HEADS_A + 1, dtype=np.float64) / N_HEADS_A)


def _attn_bias(qpos, kpos, kvalid, transposed=False):
    qpos = np.asarray(qpos)[:, None]
    kpos = np.asarray(kpos)[None, :]
    qc, kc = qpos // CHUNK, kpos // CHUNK
    visible = (kpos >= 0) & (kc <= qc) & (kc >= qc - WINDOW_CHUNKS) & np.asarray(kvalid)[None, :]
    dist = np.abs(qpos - kpos).astype(np.float64)
    slopes = _alibi_slopes()
    out = np.zeros((N_KV_A, GROUP_A // 2, qpos.shape[0], 2, kpos.shape[1]), np.float32)
    for kv in range(N_KV_A):
        for p in range(GROUP_A // 2):
            for half in range(2):
                hd = kv * GROUP_A + 2 * p + half
                out[kv, p, :, half, :] = np.where(visible, -slopes[hd] * dist, -np.inf)
    if transposed:
        return out.transpose(0, 3, 4, 1, 2).reshape(N_KV_A, 2 * kpos.shape[1], (GROUP_A // 2) * qpos.shape[0])
    return out.reshape(N_KV_A, (GROUP_A // 2) * qpos.shape[0], 2 * kpos.shape[1])


def _attn_core(q, kwin, vwin, bias_of, sinks_ref, store):
    Mq = q.shape[0]
    W = kwin.shape[0]
    npair = GROUP_A // 2
    lane = lax.broadcasted_iota(jnp.int32, (npair * Mq, LANES), 1)
    for kv in range(N_KV_A):
        kk = kwin[:, kv * HEAD_DIM_A:(kv + 1) * HEAD_DIM_A].astype(BF16)
        vv = vwin[:, kv * HEAD_DIM_A:(kv + 1) * HEAD_DIM_A].astype(BF16)
        zero = jnp.zeros_like(kk)
        one = jnp.ones_like(vv)
        k2 = jnp.concatenate([jnp.concatenate([kk, zero], axis=1),
                              jnp.concatenate([zero, kk], axis=1)], axis=0)
        v_aug = jnp.concatenate([vv, one, one, vv], axis=1)
        qs = jnp.concatenate([q[:, (kv * npair + p) * LANES:(kv * npair + p + 1) * LANES]
                              for p in range(npair)], axis=0)
        s = _bdot_nt(qs, k2) + bias_of(kv)
        probs, sink_terms = [], []
        for half in range(2):
            sh = s[:, half * W:(half + 1) * W]
            sink = jnp.concatenate(
                [jnp.full((Mq, 1), sinks_ref[kv * GROUP_A + 2 * p + half], F32) for p in range(npair)], axis=0)
            mx = jnp.maximum(jnp.max(sh, axis=1, keepdims=True), sink)
            probs.append(jnp.exp(sh - mx).astype(BF16))
            sink_terms.append(jnp.exp(sink - mx))
        o = jnp.dot(jnp.concatenate(probs, axis=0), v_aug, preferred_element_type=F32)
        oe, oo = o[:npair * Mq], o[npair * Mq:]
        ye = oe[:, :LANES] / (oe[:, LANES:] + sink_terms[0])
        yo = oo[:, LANES:] / (oo[:, :LANES] + sink_terms[1])
        y = jnp.where(lane < HEAD_DIM_A, ye, yo).astype(BF16)
        for p in range(npair):
            store((kv * npair + p) * LANES, y[p * Mq:(p + 1) * Mq])


def _attn_prompt_kernel(sinks_ref, qt_ref, kp_ref, kc_ref, vp_ref, vc_ref, bias0_ref, bias1_ref, o_ref):
    Mq = ATT_QB
    npair = GROUP_A // 2
    keys = jnp.concatenate([kp_ref[0], kc_ref[0]], axis=0)
    vals_t = jnp.concatenate([vp_ref[0], vc_ref[0]], axis=0).T.astype(BF16)
    ones = jnp.ones((HEAD_DIM_A, ATT_W), BF16)
    pair_of_lane = lax.broadcasted_iota(jnp.int32, (1, npair * Mq), 1) // Mq
    units = [(sub, kv) for sub in range(ATT_SUB) for kv in range(N_KV_A)]
    sts = {}
    for sub, kv in units:
        kk = keys[sub * Mq:sub * Mq + ATT_W, kv * HEAD_DIM_A:(kv + 1) * HEAD_DIM_A].astype(BF16)
        zero = jnp.zeros_like(kk)
        k2 = jnp.concatenate([jnp.concatenate([kk, zero], axis=1),
                              jnp.concatenate([zero, kk], axis=1)], axis=0)
        qt = jnp.concatenate([qt_ref[0, (kv * npair + p) * LANES:(kv * npair + p + 1) * LANES, sub * Mq:(sub + 1) * Mq]
                              for p in range(npair)], axis=1)
        sts[sub, kv] = jnp.dot(k2, qt, preferred_element_type=F32)
    probs, sink_terms = {}, {}
    for sub, kv in units:
        st = sts[sub, kv] + (bias0_ref[0, kv] if sub == 0 else bias1_ref[0, kv])
        for half in range(2):
            sh = st[half * ATT_W:(half + 1) * ATT_W]
            sink = jnp.zeros((1, npair * Mq), F32)
            for p in range(npair):
                sink = jnp.where(pair_of_lane == p, sinks_ref[kv * GROUP_A + 2 * p + half] * LOG2E, sink)
            mx = jnp.maximum(jnp.max(sh, axis=0, keepdims=True), sink)
            probs[sub, kv, half] = jnp.exp2(sh - mx).astype(BF16)
            sink_terms[sub, kv, half] = jnp.exp2(sink - mx)
    ots = {}
    for sub, kv in units:
        vt = vals_t[kv * HEAD_DIM_A:(kv + 1) * HEAD_DIM_A, sub * Mq:sub * Mq + ATT_W]
        for half in range(2):
            lhs = jnp.concatenate([vt, ones] if half == 0 else [ones, vt], axis=0)
            ots[sub, kv, half] = jnp.dot(lhs, probs[sub, kv, half], preferred_element_type=F32)
    for sub, kv in units:
        oe, oo = ots[sub, kv, 0], ots[sub, kv, 1]
        ye = oe[:HEAD_DIM_A] / (oe[HEAD_DIM_A:] + sink_terms[sub, kv, 0])
        yo = oo[HEAD_DIM_A:] / (oo[:HEAD_DIM_A] + sink_terms[sub, kv, 1])
        yt = jnp.concatenate([ye, yo], axis=0)
        for p in range(npair):
            col = (kv * npair + p) * LANES
            o_ref[0, sub * Mq:(sub + 1) * Mq, col:col + LANES] = yt[:, p * Mq:(p + 1) * Mq].T.astype(BF16)


def _attn_prompt_call(qa_t, ka, va, sinks):
    N, _, T = qa_t.shape
    rows = ATT_QB * ATT_SUB
    rel_q = ATT_QB + np.arange(ATT_QB)
    rel_k = np.arange(ATT_W)
    bias = np.stack([_attn_bias(rel_q, rel_k, rel_k >= ATT_QB, transposed=True),
                     _attn_bias(rel_q, rel_k, rel_k >= 0, transposed=True)]) * np.float32(LOG2E)
    kv_prev = pl.BlockSpec((1, ATT_QB, KV_A_W), lambda n, j: (n, jnp.maximum(ATT_SUB * j - 1, 0), 0))
    kv_cur = pl.BlockSpec((1, rows, KV_A_W), lambda n, j: (n, j, 0))
    return pl.pallas_call(
        _attn_prompt_kernel,
        grid=(N, T // rows),
        in_specs=[pl.BlockSpec(memory_space=pltpu.SMEM),
                  pl.BlockSpec((1, Q_A, rows), lambda n, j: (n, 0, j)),
                  kv_prev, kv_cur, kv_prev, kv_cur,
                  pl.BlockSpec((1,) + bias.shape[1:], lambda n, j: (jnp.minimum(j, 1), 0, 0, 0)),
                  pl.BlockSpec((1,) + bias.shape[1:], lambda n, j: (1, 0, 0, 0), pipeline_mode=pl.Buffered(1))],
        out_specs=pl.BlockSpec((1, rows, Q_A), lambda n, j: (n, j, 0)),
        out_shape=jax.ShapeDtypeStruct((N, T, Q_A), BF16),
        compiler_params=pltpu.CompilerParams(dimension_semantics=("parallel", "parallel")),
        name="attn_prompt",
    )(sinks, qa_t, ka, ka, va, va, jnp.asarray(bias), jnp.asarray(bias))


def _attn_sample_kernel(sinks_ref, q_ref, kn_ref, vn_ref, kc_ref, vc_ref, bias_ref,
                        o_ref, ko_ref, vo_ref, *, T, NS):
    pad = jnp.zeros((ATT_W - KV_WIN - T, KV_A_W), F32)
    for s in range(NS):
        kwin = jnp.concatenate([kc_ref[s], kn_ref[s], pad], axis=0)
        vwin = jnp.concatenate([vc_ref[s], vn_ref[s], pad], axis=0)
        ko_ref[s] = kwin[T:T + KV_WIN]
        vo_ref[s] = vwin[T:T + KV_WIN]

        def store(col, val, s=s):
            o_ref[s, :, col:col + LANES] = val

        _attn_core(q_ref[s], kwin, vwin, lambda kv: bias_ref[kv], sinks_ref, store)


def _attn_sample_call(qa, ka, va, cache_k, cache_v, sinks, *, NS):
    N, T, _ = qa.shape
    qpos = PAST_LEN + np.arange(T)
    kpos = PAST_LEN - KV_WIN + np.arange(ATT_W)
    bias = _attn_bias(qpos, kpos, np.arange(ATT_W) < KV_WIN + T)
    new = pl.BlockSpec((NS, T, KV_A_W), lambda n: (n, 0, 0))
    cache = pl.BlockSpec((NS, KV_WIN, KV_A_W), lambda n: (n, 0, 0))
    cache_shape = jax.ShapeDtypeStruct((N, KV_WIN, KV_A_W), F32)
    return pl.pallas_call(
        functools.partial(_attn_sample_kernel, T=T, NS=NS),
        grid=(N // NS,),
        in_specs=[pl.BlockSpec(memory_space=pltpu.SMEM),
                  pl.BlockSpec((NS, T, Q_A), lambda n: (n, 0, 0)),
                  new, new, cache, cache,
                  _const_spec(bias.shape)],
        out_specs=[pl.BlockSpec((NS, T, Q_A), lambda n: (n, 0, 0)), cache, cache],
        out_shape=[jax.ShapeDtypeStruct((N, T, Q_A), BF16), cache_shape, cache_shape],
        compiler_params=pltpu.CompilerParams(dimension_semantics=("parallel",)),
        name="attn_sample",
    )(sinks, qa, ka, va, cache_k, cache_v, jnp.asarray(bias))


def _mlstm_kernel(q_ref, kt_ref, v_ref, og_ref, ar_ref, bc_ref, ac_ref, nw_ref, c0_ref, n0_ref, m0_ref,
                  y_ref, c_ref, n_ref, m_ref, ct_ref, *, SB, L):
    j = pl.program_id(1)
    heads = [(sb, hd) for sb in range(SB) for hd in range(N_HEADS_B)]

    @pl.when(j == 0)
    def _():
        for sb, hd in heads:
            ct_ref[sb, hd, :, :DV_B] = c0_ref[sb, hd].T
            ct_ref[sb, hd, :, DV_B:] = jnp.broadcast_to(n0_ref[sb, hd], (DQK_B, LANES))
        m_ref[...] = m0_ref[...]

    causal = (lax.broadcasted_iota(jnp.int32, (L, L), 0) >= lax.broadcasted_iota(jnp.int32, (L, L), 1))
    ones = jnp.ones((L, LANES), BF16)
    gates = []
    for sb in range(SB):
        a4 = ac_ref[sb]
        b4 = bc_ref[sb]
        m_row = m_ref[sb]
        a_last = a4[L - 1:L]
        b_last = b4[L - 1:L]
        mx = jnp.maximum(a_last, m_row)
        m_new = b_last + mx
        gates.append(dict(a4=a4, b4=b4, a_last=a_last, m_row=m_row, decay4=jnp.exp(b_last + m_row - m_new),
                          rho4=jnp.exp(a_last - mx)))
        m_ref[sb] = m_new

    def lanes(x):
        return jnp.broadcast_to(x, (x.shape[0], LANES))

    def col(x, hd):
        return x[:, hd:hd + 1]

    qs = {h: q_ref[h[0], :, h[1] * DQK_B:(h[1] + 1) * DQK_B] for h in heads}
    kts = {h: kt_ref[h[0], h[1] * DQK_B:(h[1] + 1) * DQK_B, :] for h in heads}
    cts = {h: ct_ref[h[0], h[1]] for h in heads}
    qk = {h: jnp.dot(qs[h], kts[h], preferred_element_type=F32) for h in heads}
    qc = {h: jnp.dot(qs[h], cts[h].astype(BF16), preferred_element_type=F32) for h in heads}
    lhs, a_bs = {}, {}
    for h in heads:
        sb, hd = h
        gt = gates[sb]
        a_row = ar_ref[sb, hd:hd + 1, :]
        a_bs[h] = lanes(col(gt["a4"], hd))
        a_full = jnp.concatenate([a_bs[h]] * (L // LANES), axis=1) if L >= LANES else a_bs[h][:, :L]
        w = jnp.exp(jnp.where(causal, a_row - a_full, -jnp.inf))
        s = (qk[h] * w).astype(BF16)
        kw = (kts[h].astype(F32) * jnp.exp(a_row - col(gt["a_last"], hd))).astype(BF16)
        lhs[h] = jnp.concatenate([s, kw], axis=0)
    pu = {}
    for h in heads:
        sb, hd = h
        v_aug = jnp.concatenate([v_ref[sb, :, hd * DV_B:(hd + 1) * DV_B], ones], axis=1)
        pu[h] = jnp.dot(lhs[h], v_aug, preferred_element_type=F32)
    ones_sum = jnp.ones((DV_B, LANES), BF16)
    for h in heads:
        sb, hd = h
        gt = gates[sb]
        a_b = a_bs[h]
        m_b = lanes(col(gt["m_row"], hd))
        big = jnp.maximum(a_b, m_b)
        r_b = jnp.exp(a_b - big)
        g_b = jnp.exp(m_b - big)
        e_b = jnp.exp(-(lanes(col(gt["b4"], hd)) + big))
        p, c = pu[h], qc[h]
        den = r_b * p[:L, DV_B:] + g_b * c[:, DV_B:]
        inv = 1.0 / jnp.maximum(jnp.abs(den), e_b)
        hh = jnp.concatenate([(r_b * p[:L, k * LANES:(k + 1) * LANES] + g_b * c[:, k * LANES:(k + 1) * LANES]) * inv
                              for k in range(DV_B // LANES)], axis=1)
        mu = jnp.dot(hh.astype(BF16), ones_sum, preferred_element_type=F32) * (1.0 / DV_B)
        hc = hh - jnp.concatenate([mu] * (DV_B // LANES), axis=1)
        var = jnp.dot(jnp.square(hc).astype(BF16), ones_sum, preferred_element_type=F32) * (1.0 / DV_B)
        rstd = lax.rsqrt(var + LN_EPS)
        yn = hc * jnp.concatenate([rstd] * (DV_B // LANES), axis=1) * nw_ref[:, hd * DV_B:(hd + 1) * DV_B]
        og = og_ref[sb, :, hd * DV_B:(hd + 1) * DV_B]
        y_ref[sb, :, hd * DV_B:(hd + 1) * DV_B] = (jax.nn.sigmoid(og) * yn).astype(BF16)
        ct_ref[sb, hd] = col(gt["decay4"], hd) * cts[h] + col(gt["rho4"], hd) * pu[h][L:]

    @pl.when(j == pl.num_programs(1) - 1)
    def _():
        for sb, hd in heads:
            c_ref[sb, hd] = ct_ref[sb, hd, :, :DV_B].T
            n_ref[sb, hd] = ct_ref[sb, hd, :, DV_B:DV_B + 1]


def _mlstm_call(qm, kt, vm, og, a_row, b_col, a_col, norm_w, c0, n0, m0, *, SB, L):
    N, T, _ = qm.shape

    def tok(c):
        return pl.BlockSpec((SB, L, c), lambda i, j: (i, j, 0))

    def tr(c):
        return pl.BlockSpec((SB, c, L), lambda i, j: (i, 0, j))

    st_c = pl.BlockSpec((SB, N_HEADS_B, DV_B, DQK_B), lambda i, j: (i, 0, 0, 0))
    st_n = pl.BlockSpec((SB, N_HEADS_B, DQK_B, 1), lambda i, j: (i, 0, 0, 0))
    st_m = pl.BlockSpec((SB, 1, LANES), lambda i, j: (i, 0, 0))
    return pl.pallas_call(
        functools.partial(_mlstm_kernel, SB=SB, L=L),
        grid=(N // SB, T // L),
        in_specs=[tok(QK_B), tr(QK_B), tok(V_B), tok(V_B), tr(8), tok(LANES), tok(LANES),
                  _const_spec((1, V_B)), st_c, st_n, st_m],
        out_specs=[tok(V_B), st_c, st_n, st_m],
        out_shape=[jax.ShapeDtypeStruct((N, T, V_B), BF16),
                   jax.ShapeDtypeStruct((N, N_HEADS_B, DV_B, DQK_B), F32),
                   jax.ShapeDtypeStruct((N, N_HEADS_B, DQK_B, 1), F32),
                   jax.ShapeDtypeStruct((N, 1, LANES), F32)],
        scratch_shapes=[pltpu.VMEM((SB, N_HEADS_B, DQK_B, DV_B + LANES), F32)],
        compiler_params=pltpu.CompilerParams(dimension_semantics=("parallel", "arbitrary")),
        name="mlstm",
    )(qm, kt, vm, og, a_row, b_col, a_col, norm_w, c0, n0, m0)


def _merge_kernel(x_ref, mod_ref, ya_ref, yb_ref, ga_ref, gb_ref,
                  wpa_ref, wpb_ref, wo_ref, l1w_ref, l1b_ref, x1_ref, *, S, R):
    SR = S * R
    g1 = mod_ref[:, 2:3, :]
    pa = jnp.dot(ya_ref[...].reshape(SR, Q_A), wpa_ref[...], preferred_element_type=F32)
    pb = jnp.dot(yb_ref[...].reshape(SR, V_B), wpb_ref[...], preferred_element_type=F32)
    merged = (jax.nn.sigmoid(ga_ref[...].reshape(SR, D_MODEL)) * pa
              + jax.nn.sigmoid(gb_ref[...].reshape(SR, D_MODEL)) * pb)
    merged = merged.astype(BF16).reshape(S, R, D_MODEL)
    for lo in range(0, R, R // EPI_SPLIT):
        rows = slice(lo, lo + R // EPI_SPLIT)
        mo = jnp.dot(merged[:, rows].reshape(S * (R // EPI_SPLIT), D_MODEL), wo_ref[...],
                     preferred_element_type=F32).reshape(S, R // EPI_SPLIT, D_MODEL)
        x1_ref[:, rows, :] = _layer_norm(ALPHA * x_ref[:, rows, :] + g1 * mo, l1w_ref[...], l1b_ref[...])


def _merge_call(x, mod, ya, yb, ga, gb, wpa, wpb, wo, l1w, l1b, *, S, R):
    N, T, _ = x.shape

    def tok(c):
        return pl.BlockSpec((S, R, c), lambda i, j: (i, j, 0))

    return pl.pallas_call(
        functools.partial(_merge_kernel, S=S, R=R),
        grid=(N // S, T // R),
        in_specs=[tok(D_MODEL), pl.BlockSpec((S, 6, D_MODEL), lambda i, j: (i, 0, 0)),
                  tok(Q_A), tok(V_B), tok(D_MODEL), tok(D_MODEL),
                  _const_spec((Q_A, D_MODEL)), _const_spec((V_B, D_MODEL)), _const_spec((D_MODEL, D_MODEL)),
                  _const_spec((1, D_MODEL)), _const_spec((1, D_MODEL))],
        out_specs=tok(D_MODEL),
        out_shape=jax.ShapeDtypeStruct((N, T, D_MODEL), F32),
        compiler_params=pltpu.CompilerParams(dimension_semantics=("parallel", "parallel")),
        name="merge",
    )(x, mod, ya, yb, ga, gb, wpa, wpb, wo, l1w, l1b)


def _ffn_kernel(x1_ref, mod_ref, cin_ref, wup_ref, bup_ref, cw_ref, cb_ref, wdn_ref, l2w_ref, l2b_ref,
                y_ref, cout_ref, carry_ref, act_ref, *perm_refs, S, R, permute):
    SR = S * R
    FC = D_FF // FFN_CHUNKS
    NV = R // SUBLANES
    NT = D_MODEL // LANES

    @pl.when(pl.program_id(1) == 0)
    def _():
        carry_ref[...] = cin_ref[...] - bup_ref[...]

    sh2 = mod_ref[:, 3:4, :]
    sc2 = mod_ref[:, 4:5, :]
    g2 = mod_ref[:, 5:6, :]

    def regroup(src_ref, start_of, stride):
        return jnp.concatenate(
            [jnp.concatenate([src_ref[s, c, pl.ds(start_of(v), SUBLANES, stride=stride), :] for c in range(NT)],
                             axis=1) for s in range(S) for v in range(NV)], axis=0).reshape(S, R, D_MODEL)

    if permute:
        (tiles_ref,) = perm_refs
        for c in range(NT):
            tiles_ref[:, c] = x1_ref[:, :, c * LANES:(c + 1) * LANES]
        x1 = regroup(tiles_ref, lambda v: v, NV)
    else:
        x1 = x1_ref[...]
    h2 = (x1 * (1.0 + sc2) + sh2).reshape(SR, D_MODEL).astype(BF16)
    row8 = lax.broadcasted_iota(jnp.int32, (SUBLANES, FC), 0)

    def wrap(group, first):
        return jnp.where(row8 == 0, first, pltpu.roll(group, 1, axis=0))

    def back_permuted(blk, c0, c1):
        g_last = wrap(blk[(NV - 1) * SUBLANES:], c1)
        g_prev = wrap(blk[(NV - 2) * SUBLANES:(NV - 1) * SUBLANES], c0)
        new = jnp.concatenate([blk[(NV - 1) * SUBLANES - 1:(NV - 1) * SUBLANES], blk[R - 1:R]], axis=0)
        return ([g_last, blk[:(NV - 1) * SUBLANES]], [g_prev, g_last, blk[:(NV - 2) * SUBLANES]], new)

    def back_natural(blk, c0, c1):
        b1 = pltpu.roll(blk, 1, axis=0)
        b1 = [jnp.where(row8 == 0, c1, b1[:SUBLANES]), b1[SUBLANES:]]
        b2 = pltpu.roll(blk, 2, axis=0)
        b2 = [jnp.where(row8 == 0, c0, jnp.where(row8 == 1, c1, b2[:SUBLANES])), b2[SUBLANES:]]
        return b1, b2, blk[R - 2:]

    def conv(lo):
        cs = slice(lo, lo + FC)
        u = jnp.dot(h2, wup_ref[:, cs], preferred_element_type=F32)
        w0, w1, w2 = cw_ref[0:1, cs], cw_ref[1:2, cs], cw_ref[2:3, cs]
        const = cb_ref[:, cs] + bup_ref[:, cs] * (w0 + w1 + w2)
        back1, back2 = [], []
        for s in range(S):
            b1, b2, new = (back_permuted if permute else back_natural)(
                u[s * R:(s + 1) * R], carry_ref[s, 0:1, cs], carry_ref[s, 1:2, cs])
            back1 += b1
            back2 += b2
            carry_ref[s, :, cs] = new
            cout_ref[s, :, cs] = new + bup_ref[:, cs]
        return w0 * jnp.concatenate(back2, axis=0) + w1 * jnp.concatenate(back1, axis=0) + w2 * u + const

    for ci in range(FFN_CHUNKS):
        lo = ci * FC
        a = conv(lo)
        g = conv(D_FF + lo)
        act_ref[:, lo:lo + FC] = (a * jax.nn.gelu(g, approximate=True)).astype(BF16)

    parts = []
    for k in range(EPI_SPLIT):
        if S == 1:
            sl_s, sl_r, n = slice(0, 1), slice(k * R // EPI_SPLIT, (k + 1) * R // EPI_SPLIT), R // EPI_SPLIT
            rows = sl_r
        else:
            sl_s, sl_r, n = slice(k * S // EPI_SPLIT, (k + 1) * S // EPI_SPLIT), slice(0, R), SR // EPI_SPLIT
            rows = slice(k * SR // EPI_SPLIT, (k + 1) * SR // EPI_SPLIT)
        f = jnp.dot(act_ref[rows, :], wdn_ref[...], preferred_element_type=F32)
        f = f.reshape((1, n, D_MODEL) if S == 1 else (S // EPI_SPLIT, R, D_MODEL))
        parts.append(_layer_norm(ALPHA * x1[sl_s, sl_r] + g2[sl_s] * f, l2w_ref[...], l2b_ref[...]))
    y = jnp.concatenate(parts, axis=1 if S == 1 else 0)
    if permute:
        for c in range(NT):
            tiles_ref[:, c] = y[:, :, c * LANES:(c + 1) * LANES]
        y = regroup(tiles_ref, lambda v: ((SUBLANES * v) % NV) * SUBLANES + (SUBLANES * v) // NV, SUBLANES)
    y_ref[...] = y


def _ffn_call(x1, mod, conv_in, wup, bup, cw, cb, wdn, l2w, l2b, *, S, R):
    N, T, _ = x1.shape
    permute = (R // SUBLANES) % SUBLANES == 0

    def tok(c):
        return pl.BlockSpec((S, R, c), lambda i, j: (i, j, 0))

    conv_spec = pl.BlockSpec((S, CONV_W - 1, 2 * D_FF), lambda i, j: (i, 0, 0))
    return pl.pallas_call(
        functools.partial(_ffn_kernel, S=S, R=R, permute=permute),
        grid=(N // S, T // R),
        in_specs=[tok(D_MODEL), pl.BlockSpec((S, 6, D_MODEL), lambda i, j: (i, 0, 0)), conv_spec,
                  _const_spec((D_MODEL, 2 * D_FF)), _const_spec((1, 2 * D_FF)),
                  _const_spec((CONV_W, 2 * D_FF)), _const_spec((1, 2 * D_FF)),
                  _const_spec((D_FF, D_MODEL)), _const_spec((1, D_MODEL)), _const_spec((1, D_MODEL))],
        out_specs=[tok(D_MODEL), conv_spec],
        out_shape=[jax.ShapeDtypeStruct((N, T, D_MODEL), F32),
                   jax.ShapeDtypeStruct((N, CONV_W - 1, 2 * D_FF), F32)],
        scratch_shapes=[pltpu.VMEM((S, CONV_W - 1, 2 * D_FF), F32), pltpu.VMEM((S * R, D_FF), BF16)]
        + ([pltpu.VMEM((S, D_MODEL // LANES, R, LANES), F32)] if permute else []),
        compiler_params=pltpu.CompilerParams(dimension_semantics=("parallel", "arbitrary")),
        name="ffn",
    )(x1, mod, conv_in, wup, bup, cw, cb, wdn, l2w, l2b)


def kernel(x_prompt, x_sample, cache_attn_k, cache_attn_v, state_mlstm_C, state_mlstm_n, state_mlstm_m, state_ffn_conv, c_prompt, c_sample, w_ada, b_ada, w_in, b_igate, b_fgate, attn_sinks, mlstm_norm_w, w_proj_a, w_proj_b, w_out, ln1_w, ln1_b, w_up, b_up, conv_w, conv_b, w_down, ln2_w, ln2_b):
    Bp, Tp, D = x_prompt.shape
    Bs, Ts, _ = x_sample.shape
    l = 0

    mod = _mod_call(jnp.concatenate([c_prompt, c_sample], axis=0), w_ada[l], b_ada[l][None])
    mod_p = mod[:Bp].reshape(Bp, 6, D)
    mod_s = mod[Bp:].reshape(Bs, 6, D)

    w = w_in[l]
    o_i, o_f, o_o = Z_OFF[6], Z_OFF[7], Z_OFF[8]
    lane_pad = ((0, 0), (0, LANES - N_HEADS_B))
    w_main = jnp.concatenate([w[:, :o_i], w[:, o_o:],
                              jnp.pad(w[:, o_i:o_f], lane_pad), jnp.pad(w[:, o_f:o_o], lane_pad)],
                             axis=1).astype(BF16)
    b_ic = jnp.pad(b_igate[l][None], lane_pad)
    b_fc = jnp.pad(b_fgate[l][None], lane_pad)
    inproj_w = (w_main, b_ic, b_fc)
    sinks = attn_sinks[l]
    norm_w = mlstm_norm_w[l][None]
    merge_w = (w_proj_a[l].astype(BF16), w_proj_b[l].astype(BF16), w_out[l].astype(BF16),
               ln1_w[l][None], ln1_b[l][None])
    ffn_w = (w_up[l].astype(BF16), b_up[l][None], conv_w[l], conv_b[l][None],
             w_down[l].astype(BF16), ln2_w[l][None], ln2_b[l][None])

    qa, ka, va, qm, kt, vm, og, ga, gb, a_row, b_col, a_col = _inproj_call(
        x_prompt, mod_p, *inproj_w, S=1, R=INPROJ_ROWS, L=MLSTM_CHUNK, q_transposed=True)
    ya = _attn_prompt_call(qa, ka, va, sinks)
    yb, p_c, p_n, p_m = _mlstm_call(
        qm, kt, vm, og, a_row, b_col, a_col, norm_w,
        jnp.zeros((Bp, N_HEADS_B, DV_B, DQK_B), F32), jnp.zeros((Bp, N_HEADS_B, DQK_B, 1), F32),
        jnp.zeros((Bp, 1, LANES), F32), SB=4, L=MLSTM_CHUNK)
    p_m = p_m[:, 0, :N_HEADS_B]
    x1 = _merge_call(x_prompt, mod_p, ya, yb, ga, gb, *merge_w, S=1, R=POST_ROWS)
    y_prompt, p_conv = _ffn_call(x1, mod_p, jnp.zeros((Bp, CONV_W - 1, 2 * D_FF), F32), *ffn_w, S=1, R=POST_ROWS)
    p_k = ka[:, Tp - KV_WIN:].reshape(Bp, KV_WIN, N_KV_A, HEAD_DIM_A)
    p_v = va[:, Tp - KV_WIN:].reshape(Bp, KV_WIN, N_KV_A, HEAD_DIM_A)

    qa, ka, va, qm, kt, vm, og, ga, gb, a_row, b_col, a_col = _inproj_call(
        x_sample, mod_s, *inproj_w, S=16, R=Ts, L=Ts, q_transposed=False)
    ya, s_k, s_v = _attn_sample_call(qa, ka, va, cache_attn_k[l].reshape(Bs, KV_WIN, KV_A_W),
                                     cache_attn_v[l].reshape(Bs, KV_WIN, KV_A_W), sinks, NS=8)
    yb, s_c, s_n, s_m = _mlstm_call(
        qm, kt, vm, og, a_row, b_col, a_col, norm_w,
        state_mlstm_C[l], state_mlstm_n[l][:, :, :, None], jnp.pad(state_mlstm_m[l][:, None, :], ((0, 0),) + lane_pad),
        SB=4, L=Ts)
    s_m = s_m[:, 0, :N_HEADS_B]
    x1 = _merge_call(x_sample, mod_s, ya, yb, ga, gb, *merge_w, S=16, R=Ts)
    y_sample, s_conv = _ffn_call(x1, mod_s, state_ffn_conv[l], *ffn_w, S=8, R=Ts)

    def st(a, shape):
        return a.reshape((1,) + shape)

    return (y_prompt, y_sample,
            st(p_k, (Bp, KV_WIN, N_KV_A, HEAD_DIM_A)), st(p_v, (Bp, KV_WIN, N_KV_A, HEAD_DIM_A)),
            st(p_c, (Bp, N_HEADS_B, DV_B, DQK_B)), st(p_n, (Bp, N_HEADS_B, DQK_B)), st(p_m, (Bp, N_HEADS_B)),
            st(p_conv, (Bp, CONV_W - 1, 2 * D_FF)),
            st(s_k, (Bs, KV_WIN, N_KV_A, HEAD_DIM_A)), st(s_v, (Bs, KV_WIN, N_KV_A, HEAD_DIM_A)),
            st(s_c, (Bs, N_HEADS_B, DV_B, DQK_B)), st(s_n, (Bs, N_HEADS_B, DQK_B)), st(s_m, (Bs, N_HEADS_B)),
            st(s_conv, (Bs, CONV_W - 1, 2 * D_FF)))
```

```python
import functools

import numpy as np
import jax
import jax.numpy as jnp
from jax import lax
from jax.experimental import pallas as pl
from jax.experimental.pallas import tpu as pltpu

F32 = jnp.float32
BF16 = jnp.bfloat16

D_MODEL = 1024
DEPTH = 1
PAST_LEN = 2048
CHUNK = 64
N_HEADS_A = 16
N_KV_A = 2
HEAD_DIM_A = 64
GROUP_A = N_HEADS_A // N_KV_A
WINDOW = 128
WINDOW_CHUNKS = WINDOW // CHUNK
KV_WIN = min(WINDOW, PAST_LEN)
N_HEADS_B = 4
DQK_B = 128
DV_B = 256
D_FF = 2816
CONV_W = 3
LN_EPS = 1e-5
ALPHA = (2 * DEPTH) ** 0.25
Q_A = N_HEADS_A * HEAD_DIM_A
KV_A_W = N_KV_A * HEAD_DIM_A
QK_B = N_HEADS_B * DQK_B
V_B = N_HEADS_B * DV_B
Z_PARTS = (Q_A, KV_A_W, KV_A_W, QK_B, QK_B, V_B, N_HEADS_B, N_HEADS_B, V_B, D_MODEL, D_MODEL)
Z_OFF = tuple(int(v) for v in np.cumsum((0,) + Z_PARTS))

LANES = 128
SUBLANES = 8
LOG2E = float(np.log2(np.e))

M_QA = 0
M_KA = M_QA + Q_A
M_VA = M_KA + KV_A_W
M_QM = M_VA + KV_A_W
M_KM = M_QM + QK_B
M_VM = M_KM + QK_B
M_OG = M_VM + V_B
M_GA = M_OG + V_B
M_GB = M_GA + D_MODEL
M_GI = M_GB + D_MODEL
M_GF = M_GI + LANES
M_END = M_GF + LANES
ATT_QB = 128
ATT_SUB = 8
ATT_W = 256
FFN_CHUNKS = 11
MLSTM_CHUNK = 256
INPROJ_ROWS = 512
POST_ROWS = 512
EPI_SPLIT = 2


def _bdot(a, b):
    return jnp.dot(a.astype(BF16), b.astype(BF16), preferred_element_type=F32)


def _bdot_nt(a, b):
    return lax.dot_general(a.astype(BF16), b.astype(BF16), (((1,), (1,)), ((), ())),
                           preferred_element_type=F32)


def _const_spec(shape):
    nd = len(shape)
    return pl.BlockSpec(shape, lambda *_: (0,) * nd, pipeline_mode=pl.Buffered(1))


def _layer_norm(x, w, b):
    mu = jnp.mean(x, -1, keepdims=True)
    xc = x - mu
    var = jnp.mean(jnp.square(xc), -1, keepdims=True)
    return xc * lax.rsqrt(var + LN_EPS) * w + b


def _mod_kernel(c_ref, w_ref, b_ref, o_ref):
    c = c_ref[...]
    s = c * jax.nn.sigmoid(c)
    o_ref[...] = _bdot(s, w_ref[...]) + b_ref[...]


def _mod_call(c_all, w_ada, b_ada):
    n = c_all.shape[0]
    tn = 1536
    return pl.pallas_call(
        _mod_kernel,
        grid=(6 * D_MODEL // tn,),
        in_specs=[pl.BlockSpec((n, D_MODEL), lambda j: (0, 0)),
                  pl.BlockSpec((D_MODEL, tn), lambda j: (0, j)),
                  pl.BlockSpec((1, tn), lambda j: (0, j))],
        out_specs=pl.BlockSpec((n, tn), lambda j: (0, j)),
        out_shape=jax.ShapeDtypeStruct((n, 6 * D_MODEL), F32),
        name="mod",
    )(c_all, w_ada, b_ada)


def _log_sigmoid(x):
    return jnp.minimum(x, 0.0) - jnp.log1p(jnp.exp(-jnp.abs(x)))


def _seg_scan(x, L, axis, op, ident):
    pos = lax.broadcasted_iota(jnp.int32, x.shape, axis) % L
    step = 1
    while step < L:
        x = op(x, jnp.where(pos >= step, pltpu.roll(x, step, axis=axis), ident))
        step *= 2
    return x


def _inproj_kernel(x_ref, mod_ref, w_ref, bic_ref, bfc_ref,
                   qa_ref, ka_ref, va_ref, qm_ref, kt_ref, vm_ref, og_ref, ga_ref, gb_ref,
                   ar_ref, bc_ref, ac_ref, *, S, R, L, q_transposed):
    SR = S * R
    sh = mod_ref[:, 0:1, :]
    sc = mod_ref[:, 1:2, :]
    h = (x_ref[...] * (1.0 + sc) + sh).reshape(SR, D_MODEL).astype(BF16)

    def proj(lo, hi):
        return jnp.dot(h, w_ref[:, lo:hi], preferred_element_type=F32)

    gates = proj(M_GI, M_END)
    q = proj(M_QA, M_KA) * (HEAD_DIM_A ** -0.5)
    km = proj(M_KM, M_VM) * (DQK_B ** -0.5)

    i_col = gates[:, :LANES] + bic_ref[...]
    logf_col = _log_sigmoid(gates[:, LANES:] + bfc_ref[...])
    b_col = _seg_scan(logf_col, L, 0, jnp.add, 0.0)
    a_col = _seg_scan(i_col - b_col, L, 0, jnp.maximum, -jnp.inf)
    bc_ref[...] = b_col.reshape(S, R, LANES)
    ac_ref[...] = a_col.reshape(S, R, LANES)
    i_row = i_col.T[:8]
    logf_row = logf_col.T[:8]
    a_row = i_row - _seg_scan(logf_row, L, 1, jnp.add, 0.0)
    kt = km.T.astype(BF16)
    for s in range(S):
        kt_ref[s] = kt[:, s * R:(s + 1) * R]
        ar_ref[s] = a_row[:, s * R:(s + 1) * R]
    if q_transposed:
        qt = (q * LOG2E).T.astype(BF16)
        for s in range(S):
            qa_ref[s] = qt[:, s * R:(s + 1) * R]
    else:
        qa_ref[...] = q.astype(BF16).reshape(S, R, Q_A)

    kv = proj(M_KA, M_QM)
    ka_ref[...] = kv[:, :KV_A_W].reshape(S, R, KV_A_W)
    va_ref[...] = kv[:, KV_A_W:].reshape(S, R, KV_A_W)
    qm_ref[...] = proj(M_QM, M_KM).astype(BF16).reshape(S, R, QK_B)
    vm_ref[...] = proj(M_VM, M_OG).astype(BF16).reshape(S, R, V_B)
    og_ref[...] = proj(M_OG, M_GA).reshape(S, R, V_B)
    ga_ref[...] = proj(M_GA, M_GB).reshape(S, R, D_MODEL)
    gb_ref[...] = proj(M_GB, M_GI).reshape(S, R, D_MODEL)


def _inproj_call(x, mod, w_main, b_ic, b_fc, *, S, R, L, q_transposed):
    N, T, _ = x.shape
    nsb, nrb = N // S, T // R

    def tok(c):
        return pl.BlockSpec((S, R, c), lambda i, j: (i, j, 0))

    def tok_shape(c, dt):
        return jax.ShapeDtypeStruct((N, T, c), dt)

    def tr(c):
        return pl.BlockSpec((S, c, R), lambda i, j: (i, 0, j))

    return pl.pallas_call(
        functools.partial(_inproj_kernel, S=S, R=R, L=L, q_transposed=q_transposed),
        grid=(nsb, nrb),
        in_specs=[tok(D_MODEL),
                  pl.BlockSpec((S, 6, D_MODEL), lambda i, j: (i, 0, 0)),
                  _const_spec((D_MODEL, M_END)),
                  _const_spec((1, LANES)), _const_spec((1, LANES))],
        out_specs=[tr(Q_A) if q_transposed else tok(Q_A), tok(KV_A_W), tok(KV_A_W), tok(QK_B), tr(QK_B), tok(V_B),
                   tok(V_B), tok(D_MODEL), tok(D_MODEL), tr(8), tok(LANES), tok(LANES)],
        out_shape=[jax.ShapeDtypeStruct((N, Q_A, T), BF16) if q_transposed else tok_shape(Q_A, BF16),
                   tok_shape(KV_A_W, F32), tok_shape(KV_A_W, F32),
                   tok_shape(QK_B, BF16), jax.ShapeDtypeStruct((N, QK_B, T), BF16), tok_shape(V_B, BF16),
                   tok_shape(V_B, F32), tok_shape(D_MODEL, F32), tok_shape(D_MODEL, F32),
                   jax.ShapeDtypeStruct((N, 8, T), F32), tok_shape(LANES, F32), tok_shape(LANES, F32)],
        compiler_params=pltpu.CompilerParams(dimension_semantics=("parallel", "parallel")),
        name="inproj",
    )(x, mod, w_main, b_ic, b_fc)


def _alibi_slopes():
    return 2.0 ** (-8.0 * np.arange(1, N_HEADS_A + 1, dtype=np.float64) / N_HEADS_A)


def _attn_bias(qpos, kpos, kvalid, transposed=False):
    qpos = np.asarray(qpos)[:, None]
    kpos = np.asarray(kpos)[None, :]
    qc, kc = qpos // CHUNK, kpos // CHUNK
    visible = (kpos >= 0) & (kc <= qc) & (kc >= qc - WINDOW_CHUNKS) & np.asarray(kvalid)[None, :]
    dist = np.abs(qpos - kpos).astype(np.float64)
    slopes = _alibi_slopes()
    out = np.zeros((N_KV_A, GROUP_A // 2, qpos.shape[0], 2, kpos.shape[1]), np.float32)
    for kv in range(N_KV_A):
        for p in range(GROUP_A // 2):
            for half in range(2):
                hd = kv * GROUP_A + 2 * p + half
                out[kv, p, :, half, :] = np.where(visible, -slopes[hd] * dist, -np.inf)
    if transposed:
        return out.transpose(0, 3, 4, 1, 2).reshape(N_KV_A, 2 * kpos.shape[1], (GROUP_A // 2) * qpos.shape[0])
    return out.reshape(N_KV_A, (GROUP_A // 2) * qpos.shape[0], 2 * kpos.shape[1])


def _attn_core(q, kwin, vwin, bias_of, sinks_ref, store):
    Mq = q.shape[0]
    W = kwin.shape[0]
    npair = GROUP_A // 2
    lane = lax.broadcasted_iota(jnp.int32, (npair * Mq, LANES), 1)
    for kv in range(N_KV_A):
        kk = kwin[:, kv * HEAD_DIM_A:(kv + 1) * HEAD_DIM_A].astype(BF16)
        vv = vwin[:, kv * HEAD_DIM_A:(kv + 1) * HEAD_DIM_A].astype(BF16)
        zero = jnp.zeros_like(kk)
        one = jnp.ones_like(vv)
        k2 = jnp.concatenate([jnp.concatenate([kk, zero], axis=1),
                              jnp.concatenate([zero, kk], axis=1)], axis=0)
        v_aug = jnp.concatenate([vv, one, one, vv], axis=1)
        qs = jnp.concatenate([q[:, (kv * npair + p) * LANES:(kv * npair + p + 1) * LANES]
                              for p in range(npair)], axis=0)
        s = _bdot_nt(qs, k2) + bias_of(kv)
        probs, sink_terms = [], []
        for half in range(2):
            sh = s[:, half * W:(half + 1) * W]
            sink = jnp.concatenate(
                [jnp.full((Mq, 1), sinks_ref[kv * GROUP_A + 2 * p + half], F32) for p in range(npair)], axis=0)
            mx = jnp.maximum(jnp.max(sh, axis=1, keepdims=True), sink)
            probs.append(jnp.exp(sh - mx).astype(BF16))
            sink_terms.append(jnp.exp(sink - mx))
        o = jnp.dot(jnp.concatenate(probs, axis=0), v_aug, preferred_element_type=F32)
        oe, oo = o[:npair * Mq], o[npair * Mq:]
        ye = oe[:, :LANES] / (oe[:, LANES:] + sink_terms[0])
        yo = oo[:, LANES:] / (oo[:, :LANES] + sink_terms[1])
        y = jnp.where(lane < HEAD_DIM_A, ye, yo).astype(BF16)
        for p in range(npair):
            store((kv * npair + p) * LANES, y[p * Mq:(p + 1) * Mq])


def _attn_prompt_kernel(sinks_ref, qt_ref, kp_ref, kc_ref, vp_ref, vc_ref, bias0_ref, bias1_ref, o_ref):
    Mq = ATT_QB
    npair = GROUP_A // 2
    keys = jnp.concatenate([kp_ref[0], kc_ref[0]], axis=0)
    vals_t = jnp.concatenate([vp_ref[0], vc_ref[0]], axis=0).T.astype(BF16)
    ones = jnp.ones((HEAD_DIM_A, ATT_W), BF16)
    pair_of_lane = lax.broadcasted_iota(jnp.int32, (1, npair * Mq), 1) // Mq
    units = [(sub, kv) for sub in range(ATT_SUB) for kv in range(N_KV_A)]
    sts = {}
    for sub, kv in units:
        kk = keys[sub * Mq:sub * Mq + ATT_W, kv * HEAD_DIM_A:(kv + 1) * HEAD_DIM_A].astype(BF16)
        zero = jnp.zeros_like(kk)
        k2 = jnp.concatenate([jnp.concatenate([kk, zero], axis=1),
                              jnp.concatenate([zero, kk], axis=1)], axis=0)
        qt = jnp.concatenate([qt_ref[0, (kv * npair + p) * LANES:(kv * npair + p + 1) * LANES, sub * Mq:(sub + 1) * Mq]
                              for p in range(npair)], axis=1)
        sts[sub, kv] = jnp.dot(k2, qt, preferred_element_type=F32)
    probs, sink_terms = {}, {}
    for sub, kv in units:
        st = sts[sub, kv] + (bias0_ref[0, kv] if sub == 0 else bias1_ref[0, kv])
        for half in range(2):
            sh = st[half * ATT_W:(half + 1) * ATT_W]
            sink = jnp.zeros((1, npair * Mq), F32)
            for p in range(npair):
                sink = jnp.where(pair_of_lane == p, sinks_ref[kv * GROUP_A + 2 * p + half] * LOG2E, sink)
            mx = jnp.maximum(jnp.max(sh, axis=0, keepdims=True), sink)
            probs[sub, kv, half] = jnp.exp2(sh - mx).astype(BF16)
            sink_terms[sub, kv, half] = jnp.exp2(sink - mx)
    ots = {}
    for sub, kv in units:
        vt = vals_t[kv * HEAD_DIM_A:(kv + 1) * HEAD_DIM_A, sub * Mq:sub * Mq + ATT_W]
        for half in range(2):
            lhs = jnp.concatenate([vt, ones] if half == 0 else [ones, vt], axis=0)
            ots[sub, kv, half] = jnp.dot(lhs, probs[sub, kv, half], preferred_element_type=F32)
    for sub, kv in units:
        oe, oo = ots[sub, kv, 0], ots[sub, kv, 1]
        ye = oe[:HEAD_DIM_A] / (oe[HEAD_DIM_A:] + sink_terms[sub, kv, 0])
        yo = oo[HEAD_DIM_A:] / (oo[:HEAD_DIM_A] + sink_terms[sub, kv, 1])
        yt = jnp.concatenate([ye, yo], axis=0)
        for p in range(npair):
            col = (kv * npair + p) * LANES
            o_ref[0, sub * Mq:(sub + 1) * Mq, col:col + LANES] = yt[:, p * Mq:(p + 1) * Mq].T.astype(BF16)


def _attn_prompt_call(qa_t, ka, va, sinks):
    N, _, T = qa_t.shape
    rows = ATT_QB * ATT_SUB
    rel_q = ATT_QB + np.arange(ATT_QB)
    rel_k = np.arange(ATT_W)
    bias = np.stack([_attn_bias(rel_q, rel_k, rel_k >= ATT_QB, transposed=True),
                     _attn_bias(rel_q, rel_k, rel_k >= 0, transposed=True)]) * np.float32(LOG2E)
    kv_prev = pl.BlockSpec((1, ATT_QB, KV_A_W), lambda n, j: (n, jnp.maximum(ATT_SUB * j - 1, 0), 0))
    kv_cur = pl.BlockSpec((1, rows, KV_A_W), lambda n, j: (n, j, 0))
    return pl.pallas_call(
        _attn_prompt_kernel,
        grid=(N, T // rows),
        in_specs=[pl.BlockSpec(memory_space=pltpu.SMEM),
                  pl.BlockSpec((1, Q_A, rows), lambda n, j: (n, 0, j)),
                  kv_prev, kv_cur, kv_prev, kv_cur,
                  pl.BlockSpec((1,) + bias.shape[1:], lambda n, j: (jnp.minimum(j, 1), 0, 0, 0)),
                  pl.BlockSpec((1,) + bias.shape[1:], lambda n, j: (1, 0, 0, 0), pipeline_mode=pl.Buffered(1))],
        out_specs=pl.BlockSpec((1, rows, Q_A), lambda n, j: (n, j, 0)),
        out_shape=jax.ShapeDtypeStruct((N, T, Q_A), BF16),
        compiler_params=pltpu.CompilerParams(dimension_semantics=("parallel", "parallel")),
        name="attn_prompt",
    )(sinks, qa_t, ka, ka, va, va, jnp.asarray(bias), jnp.asarray(bias))


def _attn_sample_kernel(sinks_ref, q_ref, kn_ref, vn_ref, kc_ref, vc_ref, bias_ref,
                        o_ref, ko_ref, vo_ref, *, T, NS):
    pad = jnp.zeros((ATT_W - KV_WIN - T, KV_A_W), F32)
    for s in range(NS):
        kwin = jnp.concatenate([kc_ref[s], kn_ref[s], pad], axis=0)
        vwin = jnp.concatenate([vc_ref[s], vn_ref[s], pad], axis=0)
        ko_ref[s] = kwin[T:T + KV_WIN]
        vo_ref[s] = vwin[T:T + KV_WIN]

        def store(col, val, s=s):
            o_ref[s, :, col:col + LANES] = val

        _attn_core(q_ref[s], kwin, vwin, lambda kv: bias_ref[kv], sinks_ref, store)


def _attn_sample_call(qa, ka, va, cache_k, cache_v, sinks, *, NS):
    N, T, _ = qa.shape
    qpos = PAST_LEN + np.arange(T)
    kpos = PAST_LEN - KV_WIN + np.arange(ATT_W)
    bias = _attn_bias(qpos, kpos, np.arange(ATT_W) < KV_WIN + T)
    new = pl.BlockSpec((NS, T, KV_A_W), lambda n: (n, 0, 0))
    cache = pl.BlockSpec((NS, KV_WIN, KV_A_W), lambda n: (n, 0, 0))
    cache_shape = jax.ShapeDtypeStruct((N, KV_WIN, KV_A_W), F32)
    return pl.pallas_call(
        functools.partial(_attn_sample_kernel, T=T, NS=NS),
        grid=(N // NS,),
        in_specs=[pl.BlockSpec(memory_space=pltpu.SMEM),
                  pl.BlockSpec((NS, T, Q_A), lambda n: (n, 0, 0)),
                  new, new, cache, cache,
                  _const_spec(bias.shape)],
        out_specs=[pl.BlockSpec((NS, T, Q_A), lambda n: (n, 0, 0)), cache, cache],
        out_shape=[jax.ShapeDtypeStruct((N, T, Q_A), BF16), cache_shape, cache_shape],
        compiler_params=pltpu.CompilerParams(dimension_semantics=("parallel",)),
        name="attn_sample",
    )(sinks, qa, ka, va, cache_k, cache_v, jnp.asarray(bias))


def _mlstm_kernel(q_ref, kt_ref, v_ref, og_ref, ar_ref, bc_ref, ac_ref, nw_ref, c0_ref, n0_ref, m0_ref,
                  y_ref, c_ref, n_ref, m_ref, ct_ref, *, SB, L):
    j = pl.program_id(1)
    heads = [(sb, hd) for sb in range(SB) for hd in range(N_HEADS_B)]

    @pl.when(j == 0)
    def _():
        for sb, hd in heads:
            ct_ref[sb, hd, :, :DV_B] = c0_ref[sb, hd].T
            ct_ref[sb, hd, :, DV_B:] = jnp.broadcast_to(n0_ref[sb, hd], (DQK_B, LANES))
        m_ref[...] = m0_ref[...]

    causal = (lax.broadcasted_iota(jnp.int32, (L, L), 0) >= lax.broadcasted_iota(jnp.int32, (L, L), 1))
    ones = jnp.ones((L, LANES), BF16)
    gates = []
    for sb in range(SB):
        a4 = ac_ref[sb]
        b4 = bc_ref[sb]
        m_row = m_ref[sb]
        a_last = a4[L - 1:L]
        b_last = b4[L - 1:L]
        mx = jnp.maximum(a_last, m_row)
        m_new = b_last + mx
        gates.append(dict(a4=a4, b4=b4, a_last=a_last, m_row=m_row, decay4=jnp.exp(b_last + m_row - m_new),
                          rho4=jnp.exp(a_last - mx)))
        m_ref[sb] = m_new

    def lanes(x):
        return jnp.broadcast_to(x, (x.shape[0], LANES))

    def col(x, hd):
        return x[:, hd:hd + 1]

    qs = {h: q_ref[h[0], :, h[1] * DQK_B:(h[1] + 1) * DQK_B] for h in heads}
    kts = {h: kt_ref[h[0], h[1] * DQK_B:(h[1] + 1) * DQK_B, :] for h in heads}
    cts = {h: ct_ref[h[0], h[1]] for h in heads}
    qk = {h: jnp.dot(qs[h], kts[h], preferred_element_type=F32) for h in heads}
    qc = {h: jnp.dot(qs[h], cts[h].astype(BF16), preferred_element_type=F32) for h in heads}
    lhs, a_bs = {}, {}
    for h in heads:
        sb, hd = h
        gt = gates[sb]
        a_row = ar_ref[sb, hd:hd + 1, :]
        a_bs[h] = lanes(col(gt["a4"], hd))
        a_full = jnp.concatenate([a_bs[h]] * (L // LANES), axis=1) if L >= LANES else a_bs[h][:, :L]
        w = jnp.exp(jnp.where(causal, a_row - a_full, -jnp.inf))
        s = (qk[h] * w).astype(BF16)
        kw = (kts[h].astype(F32) * jnp.exp(a_row - col(gt["a_last"], hd))).astype(BF16)
        lhs[h] = jnp.concatenate([s, kw], axis=0)
    pu = {}
    for h in heads:
        sb, hd = h
        v_aug = jnp.concatenate([v_ref[sb, :, hd * DV_B:(hd + 1) * DV_B], ones], axis=1)
        pu[h] = jnp.dot(lhs[h], v_aug, preferred_element_type=F32)
    ones_sum = jnp.ones((DV_B, LANES), BF16)
    for h in heads:
        sb, hd = h
        gt = gates[sb]
        a_b = a_bs[h]
        m_b = lanes(col(gt["m_row"], hd))
        big = jnp.maximum(a_b, m_b)
        r_b = jnp.exp(a_b - big)
        g_b = jnp.exp(m_b - big)
        e_b = jnp.exp(-(lanes(col(gt["b4"], hd)) + big))
        p, c = pu[h], qc[h]
        den = r_b * p[:L, DV_B:] + g_b * c[:, DV_B:]
        inv = 1.0 / jnp.maximum(jnp.abs(den), e_b)
        hh = jnp.concatenate([(r_b * p[:L, k * LANES:(k + 1) * LANES] + g_b * c[:, k * LANES:(k + 1) * LANES]) * inv
                              for k in range(DV_B // LANES)], axis=1)
        mu = jnp.dot(hh.astype(BF16), ones_sum, preferred_element_type=F32) * (1.0 / DV_B)
        hc = hh - jnp.concatenate([mu] * (DV_B // LANES), axis=1)
        var = jnp.dot(jnp.square(hc).astype(BF16), ones_sum, preferred_element_type=F32) * (1.0 / DV_B)
        rstd = lax.rsqrt(var + LN_EPS)
        yn = hc * jnp.concatenate([rstd] * (DV_B // LANES), axis=1) * nw_ref[:, hd * DV_B:(hd + 1) * DV_B]
        og = og_ref[sb, :, hd * DV_B:(hd + 1) * DV_B]
        y_ref[sb, :, hd * DV_B:(hd + 1) * DV_B] = (jax.nn.sigmoid(og) * yn).astype(BF16)
        ct_ref[sb, hd] = col(gt["decay4"], hd) * cts[h] + col(gt["rho4"], hd) * pu[h][L:]

    @pl.when(j == pl.num_programs(1) - 1)
    def _():
        for sb, hd in heads:
            c_ref[sb, hd] = ct_ref[sb, hd, :, :DV_B].T
            n_ref[sb, hd] = ct_ref[sb, hd, :, DV_B:DV_B + 1]


def _mlstm_call(qm, kt, vm, og, a_row, b_col, a_col, norm_w, c0, n0, m0, *, SB, L):
    N, T, _ = qm.shape

    def tok(c):
        return pl.BlockSpec((SB, L, c), lambda i, j: (i, j, 0))

    def tr(c):
        return pl.BlockSpec((SB, c, L), lambda i, j: (i, 0, j))

    st_c = pl.BlockSpec((SB, N_HEADS_B, DV_B, DQK_B), lambda i, j: (i, 0, 0, 0))
    st_n = pl.BlockSpec((SB, N_HEADS_B, DQK_B, 1), lambda i, j: (i, 0, 0, 0))
    st_m = pl.BlockSpec((SB, 1, LANES), lambda i, j: (i, 0, 0))
    return pl.pallas_call(
        functools.partial(_mlstm_kernel, SB=SB, L=L),
        grid=(N // SB, T // L),
        in_specs=[tok(QK_B), tr(QK_B), tok(V_B), tok(V_B), tr(8), tok(LANES), tok(LANES),
                  _const_spec((1, V_B)), st_c, st_n, st_m],
        out_specs=[tok(V_B), st_c, st_n, st_m],
        out_shape=[jax.ShapeDtypeStruct((N, T, V_B), BF16),
                   jax.ShapeDtypeStruct((N, N_HEADS_B, DV_B, DQK_B), F32),
                   jax.ShapeDtypeStruct((N, N_HEADS_B, DQK_B, 1), F32),
                   jax.ShapeDtypeStruct((N, 1, LANES), F32)],
        scratch_shapes=[pltpu.VMEM((SB, N_HEADS_B, DQK_B, DV_B + LANES), F32)],
        compiler_params=pltpu.CompilerParams(dimension_semantics=("parallel", "arbitrary")),
        name="mlstm",
    )(qm, kt, vm, og, a_row, b_col, a_col, norm_w, c0, n0, m0)


def _merge_kernel(x_ref, mod_ref, ya_ref, yb_ref, ga_ref, gb_ref,
                  wpa_ref, wpb_ref, wo_ref, l1w_ref, l1b_ref, x1_ref, *, S, R):
    SR = S * R
    g1 = mod_ref[:, 2:3, :]
    pa = jnp.dot(ya_ref[...].reshape(SR, Q_A), wpa_ref[...], preferred_element_type=F32)
    pb = jnp.dot(yb_ref[...].reshape(SR, V_B), wpb_ref[...], preferred_element_type=F32)
    merged = (jax.nn.sigmoid(ga_ref[...].reshape(SR, D_MODEL)) * pa
              + jax.nn.sigmoid(gb_ref[...].reshape(SR, D_MODEL)) * pb)
    merged = merged.astype(BF16).reshape(S, R, D_MODEL)
    for lo in range(0, R, R // EPI_SPLIT):
        rows = slice(lo, lo + R // EPI_SPLIT)
        mo = jnp.dot(merged[:, rows].reshape(S * (R // EPI_SPLIT), D_MODEL), wo_ref[...],
                     preferred_element_type=F32).reshape(S, R // EPI_SPLIT, D_MODEL)
        x1_ref[:, rows, :] = _layer_norm(ALPHA * x_ref[:, rows, :] + g1 * mo, l1w_ref[...], l1b_ref[...])


def _merge_call(x, mod, ya, yb, ga, gb, wpa, wpb, wo, l1w, l1b, *, S, R):
    N, T, _ = x.shape

    def tok(c):
        return pl.BlockSpec((S, R, c), lambda i, j: (i, j, 0))

    return pl.pallas_call(
        functools.partial(_merge_kernel, S=S, R=R),
        grid=(N // S, T // R),
        in_specs=[tok(D_MODEL), pl.BlockSpec((S, 6, D_MODEL), lambda i, j: (i, 0, 0)),
                  tok(Q_A), tok(V_B), tok(D_MODEL), tok(D_MODEL),
                  _const_spec((Q_A, D_MODEL)), _const_spec((V_B, D_MODEL)), _const_spec((D_MODEL, D_MODEL)),
                  _const_spec((1, D_MODEL)), _const_spec((1, D_MODEL))],
        out_specs=tok(D_MODEL),
        out_shape=jax.ShapeDtypeStruct((N, T, D_MODEL), F32),
        compiler_params=pltpu.CompilerParams(dimension_semantics=("parallel", "parallel")),
        name="merge",
    )(x, mod, ya, yb, ga, gb, wpa, wpb, wo, l1w, l1b)


def _ffn_kernel(x1_ref, mod_ref, cin_ref, wup_ref, bup_ref, cw_ref, cb_ref, wdn_ref, l2w_ref, l2b_ref,
                y_ref, cout_ref, carry_ref, act_ref, *perm_refs, S, R, permute):
    SR = S * R
    FC = D_FF // FFN_CHUNKS
    NV = R // SUBLANES
    NT = D_MODEL // LANES

    @pl.when(pl.program_id(1) == 0)
    def _():
        carry_ref[...] = cin_ref[...] - bup_ref[...]

    sh2 = mod_ref[:, 3:4, :]
    sc2 = mod_ref[:, 4:5, :]
    g2 = mod_ref[:, 5:6, :]

    def regroup(src_ref, start_of, stride):
        return jnp.concatenate(
            [jnp.concatenate([src_ref[s, c, pl.ds(start_of(v), SUBLANES, stride=stride), :] for c in range(NT)],
                             axis=1) for s in range(S) for v in range(NV)], axis=0).reshape(S, R, D_MODEL)

    if permute:
        (tiles_ref,) = perm_refs
        for c in range(NT):
            tiles_ref[:, c] = x1_ref[:, :, c * LANES:(c + 1) * LANES]
        x1 = regroup(tiles_ref, lambda v: v, NV)
    else:
        x1 = x1_ref[...]
    h2 = (x1 * (1.0 + sc2) + sh2).reshape(SR, D_MODEL).astype(BF16)
    row8 = lax.broadcasted_iota(jnp.int32, (SUBLANES, FC), 0)

    def wrap(group, first):
        return jnp.where(row8 == 0, first, pltpu.roll(group, 1, axis=0))

    def back_permuted(blk, c0, c1):
        g_last = wrap(blk[(NV - 1) * SUBLANES:], c1)
        g_prev = wrap(blk[(NV - 2) * SUBLANES:(NV - 1) * SUBLANES], c0)
        new = jnp.concatenate([blk[(NV - 1) * SUBLANES - 1:(NV - 1) * SUBLANES], blk[R - 1:R]], axis=0)
        return ([g_last, blk[:(NV - 1) * SUBLANES]], [g_prev, g_last, blk[:(NV - 2) * SUBLANES]], new)

    def back_natural(blk, c0, c1):
        b1 = pltpu.roll(blk, 1, axis=0)
        b1 = [jnp.where(row8 == 0, c1, b1[:SUBLANES]), b1[SUBLANES:]]
        b2 = pltpu.roll(blk, 2, axis=0)
        b2 = [jnp.where(row8 == 0, c0, jnp.where(row8 == 1, c1, b2[:SUBLANES])), b2[SUBLANES:]]
        return b1, b2, blk[R - 2:]

    def conv(lo):
        cs = slice(lo, lo + FC)
        u = jnp.dot(h2, wup_ref[:, cs], preferred_element_type=F32)
        w0, w1, w2 = cw_ref[0:1, cs], cw_ref[1:2, cs], cw_ref[2:3, cs]
        const = cb_ref[:, cs] + bup_ref[:, cs] * (w0 + w1 + w2)
        back1, back2 = [], []
        for s in range(S):
            b1, b2, new = (back_permuted if permute else back_natural)(
                u[s * R:(s + 1) * R], carry_ref[s, 0:1, cs], carry_ref[s, 1:2, cs])
            back1 += b1
            back2 += b2
            carry_ref[s, :, cs] = new
            cout_ref[s, :, cs] = new + bup_ref[:, cs]
        return w0 * jnp.concatenate(back2, axis=0) + w1 * jnp.concatenate(back1, axis=0) + w2 * u + const

    for ci in range(FFN_CHUNKS):
        lo = ci * FC
        a = conv(lo)
        g = conv(D_FF + lo)
        act_ref[:, lo:lo + FC] = (a * jax.nn.gelu(g, approximate=True)).astype(BF16)

    parts = []
    for k in range(EPI_SPLIT):
        if S == 1:
            sl_s, sl_r, n = slice(0, 1), slice(k * R // EPI_SPLIT, (k + 1) * R // EPI_SPLIT), R // EPI_SPLIT
            rows = sl_r
        else:
            sl_s, sl_r, n = slice(k * S // EPI_SPLIT, (k + 1) * S // EPI_SPLIT), slice(0, R), SR // EPI_SPLIT
            rows = slice(k * SR // EPI_SPLIT, (k + 1) * SR // EPI_SPLIT)
        f = jnp.dot(act_ref[rows, :], wdn_ref[...], preferred_element_type=F32)
        f = f.reshape((1, n, D_MODEL) if S == 1 else (S // EPI_SPLIT, R, D_MODEL))
        parts.append(_layer_norm(ALPHA * x1[sl_s, sl_r] + g2[sl_s] * f, l2w_ref[...], l2b_ref[...]))
    y = jnp.concatenate(parts, axis=1 if S == 1 else 0)
    if permute:
        for c in range(NT):
            tiles_ref[:, c] = y[:, :, c * LANES:(c + 1) * LANES]
        y = regroup(tiles_ref, lambda v: ((SUBLANES * v) % NV) * SUBLANES + (SUBLANES * v) // NV, SUBLANES)
    y_ref[...] = y


def _ffn_call(x1, mod, conv_in, wup, bup, cw, cb, wdn, l2w, l2b, *, S, R):
    N, T, _ = x1.shape
    permute = (R // SUBLANES) % SUBLANES == 0

    def tok(c):
        return pl.BlockSpec((S, R, c), lambda i, j: (i, j, 0))

    conv_spec = pl.BlockSpec((S, CONV_W - 1, 2 * D_FF), lambda i, j: (i, 0, 0))
    return pl.pallas_call(
        functools.partial(_ffn_kernel, S=S, R=R, permute=permute),
        grid=(N // S, T // R),
        in_specs=[tok(D_MODEL), pl.BlockSpec((S, 6, D_MODEL), lambda i, j: (i, 0, 0)), conv_spec,
                  _const_spec((D_MODEL, 2 * D_FF)), _const_spec((1, 2 * D_FF)),
                  _const_spec((CONV_W, 2 * D_FF)), _const_spec((1, 2 * D_FF)),
                  _const_spec((D_FF, D_MODEL)), _const_spec((1, D_MODEL)), _const_spec((1, D_MODEL))],
        out_specs=[tok(D_MODEL), conv_spec],
        out_shape=[jax.ShapeDtypeStruct((N, T, D_MODEL), F32),
                   jax.ShapeDtypeStruct((N, CONV_W - 1, 2 * D_FF), F32)],
        scratch_shapes=[pltpu.VMEM((S, CONV_W - 1, 2 * D_FF), F32), pltpu.VMEM((S * R, D_FF), BF16)]
        + ([pltpu.VMEM((S, D_MODEL // LANES, R, LANES), F32)] if permute else []),
        compiler_params=pltpu.CompilerParams(dimension_semantics=("parallel", "arbitrary")),
        name="ffn",
    )(x1, mod, conv_in, wup, bup, cw, cb, wdn, l2w, l2b)


def kernel(x_prompt, x_sample, cache_attn_k, cache_attn_v, state_mlstm_C, state_mlstm_n, state_mlstm_m, state_ffn_conv, c_prompt, c_sample, w_ada, b_ada, w_in, b_igate, b_fgate, attn_sinks, mlstm_norm_w, w_proj_a, w_proj_b, w_out, ln1_w, ln1_b, w_up, b_up, conv_w, conv_b, w_down, ln2_w, ln2_b):
    Bp, Tp, D = x_prompt.shape
    Bs, Ts, _ = x_sample.shape
    l = 0

    mod = _mod_call(jnp.concatenate([c_prompt, c_sample], axis=0), w_ada[l], b_ada[l][None])
    mod_p = mod[:Bp].reshape(Bp, 6, D)
    mod_s = mod[Bp:].reshape(Bs, 6, D)

    w = w_in[l]
    o_i, o_f, o_o = Z_OFF[6], Z_OFF[7], Z_OFF[8]
    lane_pad = ((0, 0), (0, LANES - N_HEADS_B))
    w_main = jnp.concatenate([w[:, :o_i], w[:, o_o:],
                              jnp.pad(w[:, o_i:o_f], lane_pad), jnp.pad(w[:, o_f:o_o], lane_pad)],
                             axis=1).astype(BF16)
    b_ic = jnp.pad(b_igate[l][None], lane_pad)
    b_fc = jnp.pad(b_fgate[l][None], lane_pad)
    inproj_w = (w_main, b_ic, b_fc)
    sinks = attn_sinks[l]
    norm_w = mlstm_norm_w[l][None]
    merge_w = (w_proj_a[l].astype(BF16), w_proj_b[l].astype(BF16), w_out[l].astype(BF16),
               ln1_w[l][None], ln1_b[l][None])
    ffn_w = (w_up[l].astype(BF16), b_up[l][None], conv_w[l], conv_b[l][None],
             w_down[l].astype(BF16), ln2_w[l][None], ln2_b[l][None])

    qa, ka, va, qm, kt, vm, og, ga, gb, a_row, b_col, a_col = _inproj_call(
        x_prompt, mod_p, *inproj_w, S=1, R=INPROJ_ROWS, L=MLSTM_CHUNK, q_transposed=True)
    ya = _attn_prompt_call(qa, ka, va, sinks)
    yb, p_c, p_n, p_m = _mlstm_call(
        qm, kt, vm, og, a_row, b_col, a_col, norm_w,
        jnp.zeros((Bp, N_HEADS_B, DV_B, DQK_B), F32), jnp.zeros((Bp, N_HEADS_B, DQK_B, 1), F32),
        jnp.zeros((Bp, 1, LANES), F32), SB=4, L=MLSTM_CHUNK)
    p_m = p_m[:, 0, :N_HEADS_B]
    x1 = _merge_call(x_prompt, mod_p, ya, yb, ga, gb, *merge_w, S=1, R=POST_ROWS)
    y_prompt, p_conv = _ffn_call(x1, mod_p, jnp.zeros((Bp, CONV_W - 1, 2 * D_FF), F32), *ffn_w, S=1, R=POST_ROWS)
    p_k = ka[:, Tp - KV_WIN:].reshape(Bp, KV_WIN, N_KV_A, HEAD_DIM_A)
    p_v = va[:, Tp - KV_WIN:].reshape(Bp, KV_WIN, N_KV_A, HEAD_DIM_A)

    qa, ka, va, qm, kt, vm, og, ga, gb, a_row, b_col, a_col = _inproj_call(
        x_sample, mod_s, *inproj_w, S=16, R=Ts, L=Ts, q_transposed=False)
    ya, s_k, s_v = _attn_sample_call(qa, ka, va, cache_attn_k[l].reshape(Bs, KV_WIN, KV_A_W),
                                     cache_attn_v[l].reshape(Bs, KV_WIN, KV_A_W), sinks, NS=8)
    yb, s_c, s_n, s_m = _mlstm_call(
        qm, kt, vm, og, a_row, b_col, a_col, norm_w,
        state_mlstm_C[l], state_mlstm_n[l][:, :, :, None], jnp.pad(state_mlstm_m[l][:, None, :], ((0, 0),) + lane_pad),
        SB=4, L=Ts)
    s_m = s_m[:, 0, :N_HEADS_B]
    x1 = _merge_call(x_sample, mod_s, ya, yb, ga, gb, *merge_w, S=16, R=Ts)
    y_sample, s_conv = _ffn_call(x1, mod_s, state_ffn_conv[l], *ffn_w, S=8, R=Ts)

    def st(a, shape):
        return a.reshape((1,) + shape)

    return (y_prompt, y_sample,
            st(p_k, (Bp, KV_WIN, N_KV_A, HEAD_DIM_A)), st(p_v, (Bp, KV_WIN, N_KV_A, HEAD_DIM_A)),
            st(p_c, (Bp, N_HEADS_B, DV_B, DQK_B)), st(p_n, (Bp, N_HEADS_B, DQK_B)), st(p_m, (Bp, N_HEADS_B)),
            st(p_conv, (Bp, CONV_W - 1, 2 * D_FF)),
            st(s_k, (Bs, KV_WIN, N_KV_A, HEAD_DIM_A)), st(s_v, (Bs, KV_WIN, N_KV_A, HEAD_DIM_A)),
            st(s_c, (Bs, N_HEADS_B, DV_B, DQK_B)), st(s_n, (Bs, N_HEADS_B, DQK_B)), st(s_m, (Bs, N_HEADS_B)),
            st(s_conv, (Bs, CONV_W - 1, 2 * D_FF)))
```

```python
import functools

import numpy as np
import jax
import jax.numpy as jnp
from jax import lax
from jax.experimental import pallas as pl
from jax.experimental.pallas import tpu as pltpu

F32 = jnp.float32
BF16 = jnp.bfloat16

D_MODEL = 1024
DEPTH = 1
PAST_LEN = 2048
CHUNK = 64
N_HEADS_A = 16
N_KV_A = 2
HEAD_DIM_A = 64
GROUP_A = N_HEADS_A // N_KV_A
WINDOW = 128
WINDOW_CHUNKS = WINDOW // CHUNK
KV_WIN = min(WINDOW, PAST_LEN)
N_HEADS_B = 4
DQK_B = 128
DV_B = 256
D_FF = 2816
CONV_W = 3
LN_EPS = 1e-5
ALPHA = (2 * DEPTH) ** 0.25
Q_A = N_HEADS_A * HEAD_DIM_A
KV_A_W = N_KV_A * HEAD_DIM_A
QK_B = N_HEADS_B * DQK_B
V_B = N_HEADS_B * DV_B
Z_PARTS = (Q_A, KV_A_W, KV_A_W, QK_B, QK_B, V_B, N_HEADS_B, N_HEADS_B, V_B, D_MODEL, D_MODEL)
Z_OFF = tuple(int(v) for v in np.cumsum((0,) + Z_PARTS))

LANES = 128
SUBLANES = 8
LOG2E = float(np.log2(np.e))

M_QA = 0
M_KA = M_QA + Q_A
M_VA = M_KA + KV_A_W
M_QM = M_VA + KV_A_W
M_KM = M_QM + QK_B
M_VM = M_KM + QK_B
M_OG = M_VM + V_B
M_GA = M_OG + V_B
M_GB = M_GA + D_MODEL
M_GI = M_GB + D_MODEL
M_GF = M_GI + LANES
M_END = M_GF + LANES
ATT_QB = 128
ATT_SUB = 16
ATT_W = 256
FFN_CHUNKS = 11
MLSTM_CHUNK = 256
INPROJ_ROWS = 512
POST_ROWS = 512
EPI_SPLIT = 2


def _bdot(a, b):
    return jnp.dot(a.astype(BF16), b.astype(BF16), preferred_element_type=F32)


def _bdot_nt(a, b):
    return lax.dot_general(a.astype(BF16), b.astype(BF16), (((1,), (1,)), ((), ())),
                           preferred_element_type=F32)


def _const_spec(shape):
    nd = len(shape)
    return pl.BlockSpec(shape, lambda *_: (0,) * nd, pipeline_mode=pl.Buffered(1))


def _layer_norm(x, w, b):
    mu = jnp.mean(x, -1, keepdims=True)
    xc = x - mu
    var = jnp.mean(jnp.square(xc), -1, keepdims=True)
    return xc * lax.rsqrt(var + LN_EPS) * w + b


def _mod_kernel(c_ref, w_ref, b_ref, o_ref):
    c = c_ref[...]
    s = c * jax.nn.sigmoid(c)
    o_ref[...] = _bdot(s, w_ref[...]) + b_ref[...]


def _mod_call(c_all, w_ada, b_ada):
    n = c_all.shape[0]
    tn = 1536
    return pl.pallas_call(
        _mod_kernel,
        grid=(6 * D_MODEL // tn,),
        in_specs=[pl.BlockSpec((n, D_MODEL), lambda j: (0, 0)),
                  pl.BlockSpec((D_MODEL, tn), lambda j: (0, j)),
                  pl.BlockSpec((1, tn), lambda j: (0, j))],
        out_specs=pl.BlockSpec((n, tn), lambda j: (0, j)),
        out_shape=jax.ShapeDtypeStruct((n, 6 * D_MODEL), F32),
        name="mod",
    )(c_all, w_ada, b_ada)


def _log_sigmoid(x):
    return jnp.minimum(x, 0.0) - jnp.log1p(jnp.exp(-jnp.abs(x)))


def _seg_scan(x, L, axis, op, ident):
    pos = lax.broadcasted_iota(jnp.int32, x.shape, axis) % L
    step = 1
    while step < L:
        x = op(x, jnp.where(pos >= step, pltpu.roll(x, step, axis=axis), ident))
        step *= 2
    return x


def _inproj_kernel(x_ref, mod_ref, w_ref, bic_ref, bfc_ref,
                   qa_ref, ka_ref, va_ref, qm_ref, kt_ref, vm_ref, og_ref, ga_ref, gb_ref,
                   ar_ref, bc_ref, ac_ref, *, S, R, L, q_transposed):
    SR = S * R
    sh = mod_ref[:, 0:1, :]
    sc = mod_ref[:, 1:2, :]
    h = (x_ref[...] * (1.0 + sc) + sh).reshape(SR, D_MODEL).astype(BF16)

    def proj(lo, hi):
        return jnp.dot(h, w_ref[:, lo:hi], preferred_element_type=F32)

    gates = proj(M_GI, M_END)
    q = proj(M_QA, M_KA) * (HEAD_DIM_A ** -0.5)
    km = proj(M_KM, M_VM) * (DQK_B ** -0.5)

    i_col = gates[:, :LANES] + bic_ref[...]
    logf_col = _log_sigmoid(gates[:, LANES:] + bfc_ref[...])
    b_col = _seg_scan(logf_col, L, 0, jnp.add, 0.0)
    a_col = _seg_scan(i_col - b_col, L, 0, jnp.maximum, -jnp.inf)
    bc_ref[...] = b_col.reshape(S, R, LANES)
    ac_ref[...] = a_col.reshape(S, R, LANES)
    i_row = i_col.T[:8]
    logf_row = logf_col.T[:8]
    a_row = i_row - _seg_scan(logf_row, L, 1, jnp.add, 0.0)
    kt = km.T.astype(BF16)
    for s in range(S):
        kt_ref[s] = kt[:, s * R:(s + 1) * R]
        ar_ref[s] = a_row[:, s * R:(s + 1) * R]
    if q_transposed:
        qt = (q * LOG2E).T.astype(BF16)
        for s in range(S):
            qa_ref[s] = qt[:, s * R:(s + 1) * R]
    else:
        qa_ref[...] = q.astype(BF16).reshape(S, R, Q_A)

    kv = proj(M_KA, M_QM)
    ka_ref[...] = kv[:, :KV_A_W].reshape(S, R, KV_A_W)
    va_ref[...] = kv[:, KV_A_W:].reshape(S, R, KV_A_W)
    qm_ref[...] = proj(M_QM, M_KM).astype(BF16).reshape(S, R, QK_B)
    vm_ref[...] = proj(M_VM, M_OG).astype(BF16).reshape(S, R, V_B)
    og_ref[...] = proj(M_OG, M_GA).reshape(S, R, V_B)
    ga_ref[...] = proj(M_GA, M_GB).reshape(S, R, D_MODEL)
    gb_ref[...] = proj(M_GB, M_GI).reshape(S, R, D_MODEL)


def _inproj_call(x, mod, w_main, b_ic, b_fc, *, S, R, L, q_transposed):
    N, T, _ = x.shape
    nsb, nrb = N // S, T // R

    def tok(c):
        return pl.BlockSpec((S, R, c), lambda i, j: (i, j, 0))

    def tok_shape(c, dt):
        return jax.ShapeDtypeStruct((N, T, c), dt)

    def tr(c):
        return pl.BlockSpec((S, c, R), lambda i, j: (i, 0, j))

    return pl.pallas_call(
        functools.partial(_inproj_kernel, S=S, R=R, L=L, q_transposed=q_transposed),
        grid=(nsb, nrb),
        in_specs=[tok(D_MODEL),
                  pl.BlockSpec((S, 6, D_MODEL), lambda i, j: (i, 0, 0)),
                  _const_spec((D_MODEL, M_END)),
                  _const_spec((1, LANES)), _const_spec((1, LANES))],
        out_specs=[tr(Q_A) if q_transposed else tok(Q_A), tok(KV_A_W), tok(KV_A_W), tok(QK_B), tr(QK_B), tok(V_B),
                   tok(V_B), tok(D_MODEL), tok(D_MODEL), tr(8), tok(LANES), tok(LANES)],
        out_shape=[jax.ShapeDtypeStruct((N, Q_A, T), BF16) if q_transposed else tok_shape(Q_A, BF16),
                   tok_shape(KV_A_W, F32), tok_shape(KV_A_W, F32),
                   tok_shape(QK_B, BF16), jax.ShapeDtypeStruct((N, QK_B, T), BF16), tok_shape(V_B, BF16),
                   tok_shape(V_B, F32), tok_shape(D_MODEL, F32), tok_shape(D_MODEL, F32),
                   jax.ShapeDtypeStruct((N, 8, T), F32), tok_shape(LANES, F32), tok_shape(LANES, F32)],
        compiler_params=pltpu.CompilerParams(dimension_semantics=("parallel", "parallel")),
        name="inproj",
    )(x, mod, w_main, b_ic, b_fc)


def _alibi_slopes():
    return 2.0 ** (-8.0 * np.arange(1, N_HEADS_A + 1, dtype=np.float64) / N_HEADS_A)


def _attn_bias(qpos, kpos, kvalid, transposed=False):
    qpos = np.asarray(qpos)[:, None]
    kpos = np.asarray(kpos)[None, :]
    qc, kc = qpos // CHUNK, kpos // CHUNK
    visible = (kpos >= 0) & (kc <= qc) & (kc >= qc - WINDOW_CHUNKS) & np.asarray(kvalid)[None, :]
    dist = np.abs(qpos - kpos).astype(np.float64)
    slopes = _alibi_slopes()
    out = np.zeros((N_KV_A, GROUP_A // 2, qpos.shape[0], 2, kpos.shape[1]), np.float32)
    for kv in range(N_KV_A):
        for p in range(GROUP_A // 2):
            for half in range(2):
                hd = kv * GROUP_A + 2 * p + half
                out[kv, p, :, half, :] = np.where(visible, -slopes[hd] * dist, -np.inf)
    if transposed:
        return out.transpose(0, 3, 4, 1, 2).reshape(N_KV_A, 2 * kpos.shape[1], (GROUP_A // 2) * qpos.shape[0])
    return out.reshape(N_KV_A, (GROUP_A // 2) * qpos.shape[0], 2 * kpos.shape[1])


def _attn_core(q, kwin, vwin, bias_of, sinks_ref, store):
    Mq = q.shape[0]
    W = kwin.shape[0]
    npair = GROUP_A // 2
    lane = lax.broadcasted_iota(jnp.int32, (npair * Mq, LANES), 1)
    for kv in range(N_KV_A):
        kk = kwin[:, kv * HEAD_DIM_A:(kv + 1) * HEAD_DIM_A].astype(BF16)
        vv = vwin[:, kv * HEAD_DIM_A:(kv + 1) * HEAD_DIM_A].astype(BF16)
        zero = jnp.zeros_like(kk)
        one = jnp.ones_like(vv)
        k2 = jnp.concatenate([jnp.concatenate([kk, zero], axis=1),
                              jnp.concatenate([zero, kk], axis=1)], axis=0)
        v_aug = jnp.concatenate([vv, one, one, vv], axis=1)
        qs = jnp.concatenate([q[:, (kv * npair + p) * LANES:(kv * npair + p + 1) * LANES]
                              for p in range(npair)], axis=0)
        s = _bdot_nt(qs, k2) + bias_of(kv)
        probs, sink_terms = [], []
        for half in range(2):
            sh = s[:, half * W:(half + 1) * W]
            sink = jnp.concatenate(
                [jnp.full((Mq, 1), sinks_ref[kv * GROUP_A + 2 * p + half], F32) for p in range(npair)], axis=0)
            mx = jnp.maximum(jnp.max(sh, axis=1, keepdims=True), sink)
            probs.append(jnp.exp(sh - mx).astype(BF16))
            sink_terms.append(jnp.exp(sink - mx))
        o = jnp.dot(jnp.concatenate(probs, axis=0), v_aug, preferred_element_type=F32)
        oe, oo = o[:npair * Mq], o[npair * Mq:]
        ye = oe[:, :LANES] / (oe[:, LANES:] + sink_terms[0])
        yo = oo[:, LANES:] / (oo[:, :LANES] + sink_terms[1])
        y = jnp.where(lane < HEAD_DIM_A, ye, yo).astype(BF16)
        for p in range(npair):
            store((kv * npair + p) * LANES, y[p * Mq:(p + 1) * Mq])


def _attn_prompt_kernel(sinks_ref, qt_ref, kp_ref, kc_ref, vp_ref, vc_ref, bias0_ref, bias1_ref, o_ref):
    Mq = ATT_QB
    npair = GROUP_A // 2
    keys = jnp.concatenate([kp_ref[0], kc_ref[0]], axis=0)
    vals_t = jnp.concatenate([vp_ref[0], vc_ref[0]], axis=0).T.astype(BF16)
    ones = jnp.ones((HEAD_DIM_A, ATT_W), BF16)
    pair_of_lane = lax.broadcasted_iota(jnp.int32, (1, npair * Mq), 1) // Mq
    units = [(sub, kv) for sub in range(ATT_SUB) for kv in range(N_KV_A)]
    sts = {}
    for sub, kv in units:
        kk = keys[sub * Mq:sub * Mq + ATT_W, kv * HEAD_DIM_A:(kv + 1) * HEAD_DIM_A].astype(BF16)
        zero = jnp.zeros_like(kk)
        k2 = jnp.concatenate([jnp.concatenate([kk, zero], axis=1),
                              jnp.concatenate([zero, kk], axis=1)], axis=0)
        qt = jnp.concatenate([qt_ref[0, (kv * npair + p) * LANES:(kv * npair + p + 1) * LANES, sub * Mq:(sub + 1) * Mq]
                              for p in range(npair)], axis=1)
        sts[sub, kv] = jnp.dot(k2, qt, preferred_element_type=F32)
    probs, sink_terms = {}, {}
    for sub, kv in units:
        st = sts[sub, kv] + (bias0_ref[0, kv] if sub == 0 else bias1_ref[0, kv])
        for half in range(2):
            sh = st[half * ATT_W:(half + 1) * ATT_W]
            sink = jnp.zeros((1, npair * Mq), F32)
            for p in range(npair):
                sink = jnp.where(pair_of_lane == p, sinks_ref[kv * GROUP_A + 2 * p + half] * LOG2E, sink)
            mx = jnp.maximum(jnp.max(sh, axis=0, keepdims=True), sink)
            probs[sub, kv, half] = jnp.exp2(sh - mx).astype(BF16)
            sink_terms[sub, kv, half] = jnp.exp2(sink - mx)
    ots = {}
    for sub, kv in units:
        vt = vals_t[kv * HEAD_DIM_A:(kv + 1) * HEAD_DIM_A, sub * Mq:sub * Mq + ATT_W]
        for half in range(2):
            lhs = jnp.concatenate([vt, ones] if half == 0 else [ones, vt], axis=0)
            ots[sub, kv, half] = jnp.dot(lhs, probs[sub, kv, half], preferred_element_type=F32)
    for sub, kv in units:
        oe, oo = ots[sub, kv, 0], ots[sub, kv, 1]
        ye = oe[:HEAD_DIM_A] / (oe[HEAD_DIM_A:] + sink_terms[sub, kv, 0])
        yo = oo[HEAD_DIM_A:] / (oo[:HEAD_DIM_A] + sink_terms[sub, kv, 1])
        yt = jnp.concatenate([ye, yo], axis=0)
        for p in range(npair):
            col = (kv * npair + p) * LANES
            o_ref[0, sub * Mq:(sub + 1) * Mq, col:col + LANES] = yt[:, p * Mq:(p + 1) * Mq].T.astype(BF16)


def _attn_prompt_call(qa_t, ka, va, sinks):
    N, _, T = qa_t.shape
    rows = ATT_QB * ATT_SUB
    rel_q = ATT_QB + np.arange(ATT_QB)
    rel_k = np.arange(ATT_W)
    bias = np.stack([_attn_bias(rel_q, rel_k, rel_k >= ATT_QB, transposed=True),
                     _attn_bias(rel_q, rel_k, rel_k >= 0, transposed=True)]) * np.float32(LOG2E)
    kv_prev = pl.BlockSpec((1, ATT_QB, KV_A_W), lambda n, j: (n, jnp.maximum(ATT_SUB * j - 1, 0), 0))
    kv_cur = pl.BlockSpec((1, rows, KV_A_W), lambda n, j: (n, j, 0))
    return pl.pallas_call(
        _attn_prompt_kernel,
        grid=(N, T // rows),
        in_specs=[pl.BlockSpec(memory_space=pltpu.SMEM),
                  pl.BlockSpec((1, Q_A, rows), lambda n, j: (n, 0, j)),
                  kv_prev, kv_cur, kv_prev, kv_cur,
                  pl.BlockSpec((1,) + bias.shape[1:], lambda n, j: (jnp.minimum(j, 1), 0, 0, 0)),
                  pl.BlockSpec((1,) + bias.shape[1:], lambda n, j: (1, 0, 0, 0), pipeline_mode=pl.Buffered(1))],
        out_specs=pl.BlockSpec((1, rows, Q_A), lambda n, j: (n, j, 0)),
        out_shape=jax.ShapeDtypeStruct((N, T, Q_A), BF16),
        compiler_params=pltpu.CompilerParams(dimension_semantics=("parallel", "parallel")),
        name="attn_prompt",
    )(sinks, qa_t, ka, ka, va, va, jnp.asarray(bias), jnp.asarray(bias))


def _attn_sample_kernel(sinks_ref, q_ref, kn_ref, vn_ref, kc_ref, vc_ref, bias_ref,
                        o_ref, ko_ref, vo_ref, *, T, NS):
    pad = jnp.zeros((ATT_W - KV_WIN - T, KV_A_W), F32)
    for s in range(NS):
        kwin = jnp.concatenate([kc_ref[s], kn_ref[s], pad], axis=0)
        vwin = jnp.concatenate([vc_ref[s], vn_ref[s], pad], axis=0)
        ko_ref[s] = kwin[T:T + KV_WIN]
        vo_ref[s] = vwin[T:T + KV_WIN]

        def store(col, val, s=s):
            o_ref[s, :, col:col + LANES] = val

        _attn_core(q_ref[s], kwin, vwin, lambda kv: bias_ref[kv], sinks_ref, store)


def _attn_sample_call(qa, ka, va, cache_k, cache_v, sinks, *, NS):
    N, T, _ = qa.shape
    qpos = PAST_LEN + np.arange(T)
    kpos = PAST_LEN - KV_WIN + np.arange(ATT_W)
    bias = _attn_bias(qpos, kpos, np.arange(ATT_W) < KV_WIN + T)
    new = pl.BlockSpec((NS, T, KV_A_W), lambda n: (n, 0, 0))
    cache = pl.BlockSpec((NS, KV_WIN, KV_A_W), lambda n: (n, 0, 0))
    cache_shape = jax.ShapeDtypeStruct((N, KV_WIN, KV_A_W), F32)
    return pl.pallas_call(
        functools.partial(_attn_sample_kernel, T=T, NS=NS),
        grid=(N // NS,),
        in_specs=[pl.BlockSpec(memory_space=pltpu.SMEM),
                  pl.BlockSpec((NS, T, Q_A), lambda n: (n, 0, 0)),
                  new, new, cache, cache,
                  _const_spec(bias.shape)],
        out_specs=[pl.BlockSpec((NS, T, Q_A), lambda n: (n, 0, 0)), cache, cache],
        out_shape=[jax.ShapeDtypeStruct((N, T, Q_A), BF16), cache_shape, cache_shape],
        compiler_params=pltpu.CompilerParams(dimension_semantics=("parallel",)),
        name="attn_sample",
    )(sinks, qa, ka, va, cache_k, cache_v, jnp.asarray(bias))


def _mlstm_kernel(q_ref, kt_ref, v_ref, og_ref, ar_ref, bc_ref, ac_ref, nw_ref, c0_ref, n0_ref, m0_ref,
                  y_ref, c_ref, n_ref, m_ref, ct_ref, *, SB, L):
    j = pl.program_id(1)
    heads = [(sb, hd) for sb in range(SB) for hd in range(N_HEADS_B)]

    @pl.when(j == 0)
    def _():
        for sb, hd in heads:
            ct_ref[sb, hd, :, :DV_B] = c0_ref[sb, hd].T
            ct_ref[sb, hd, :, DV_B:] = jnp.broadcast_to(n0_ref[sb, hd], (DQK_B, LANES))
        m_ref[...] = m0_ref[...]

    causal = (lax.broadcasted_iota(jnp.int32, (L, L), 0) >= lax.broadcasted_iota(jnp.int32, (L, L), 1))
    ones = jnp.ones((L, LANES), BF16)
    gates = []
    for sb in range(SB):
        a4 = ac_ref[sb]
        b4 = bc_ref[sb]
        m_row = m_ref[sb]
        a_last = a4[L - 1:L]
        b_last = b4[L - 1:L]
        mx = jnp.maximum(a_last, m_row)
        m_new = b_last + mx
        gates.append(dict(a4=a4, b4=b4, a_last=a_last, m_row=m_row, decay4=jnp.exp(b_last + m_row - m_new),
                          rho4=jnp.exp(a_last - mx)))
        m_ref[sb] = m_new

    def lanes(x):
        return jnp.broadcast_to(x, (x.shape[0], LANES))

    def col(x, hd):
        return x[:, hd:hd + 1]

    qs = {h: q_ref[h[0], :, h[1] * DQK_B:(h[1] + 1) * DQK_B] for h in heads}
    kts = {h: kt_ref[h[0], h[1] * DQK_B:(h[1] + 1) * DQK_B, :] for h in heads}
    cts = {h: ct_ref[h[0], h[1]] for h in heads}
    qk = {h: jnp.dot(qs[h], kts[h], preferred_element_type=F32) for h in heads}
    qc = {h: jnp.dot(qs[h], cts[h].astype(BF16), preferred_element_type=F32) for h in heads}
    lhs, a_bs = {}, {}
    for h in heads:
        sb, hd = h
        gt = gates[sb]
        a_row = ar_ref[sb, hd:hd + 1, :]
        a_bs[h] = lanes(col(gt["a4"], hd))
        a_full = jnp.concatenate([a_bs[h]] * (L // LANES), axis=1) if L >= LANES else a_bs[h][:, :L]
        w = jnp.exp(jnp.where(causal, a_row - a_full, -jnp.inf))
        s = (qk[h] * w).astype(BF16)
        kw = (kts[h].astype(F32) * jnp.exp(a_row - col(gt["a_last"], hd))).astype(BF16)
        lhs[h] = jnp.concatenate([s, kw], axis=0)
    pu = {}
    for h in heads:
        sb, hd = h
        v_aug = jnp.concatenate([v_ref[sb, :, hd * DV_B:(hd + 1) * DV_B], ones], axis=1)
        pu[h] = jnp.dot(lhs[h], v_aug, preferred_element_type=F32)
    ones_sum = jnp.ones((DV_B, LANES), BF16)
    for h in heads:
        sb, hd = h
        gt = gates[sb]
        a_b = a_bs[h]
        m_b = lanes(col(gt["m_row"], hd))
        big = jnp.maximum(a_b, m_b)
        r_b = jnp.exp(a_b - big)
        g_b = jnp.exp(m_b - big)
        e_b = jnp.exp(-(lanes(col(gt["b4"], hd)) + big))
        p, c = pu[h], qc[h]
        den = r_b * p[:L, DV_B:] + g_b * c[:, DV_B:]
        inv = 1.0 / jnp.maximum(jnp.abs(den), e_b)
        hh = jnp.concatenate([(r_b * p[:L, k * LANES:(k + 1) * LANES] + g_b * c[:, k * LANES:(k + 1) * LANES]) * inv
                              for k in range(DV_B // LANES)], axis=1)
        mu = jnp.dot(hh.astype(BF16), ones_sum, preferred_element_type=F32) * (1.0 / DV_B)
        hc = hh - jnp.concatenate([mu] * (DV_B // LANES), axis=1)
        var = jnp.dot(jnp.square(hc).astype(BF16), ones_sum, preferred_element_type=F32) * (1.0 / DV_B)
        rstd = lax.rsqrt(var + LN_EPS)
        yn = hc * jnp.concatenate([rstd] * (DV_B // LANES), axis=1) * nw_ref[:, hd * DV_B:(hd + 1) * DV_B]
        og = og_ref[sb, :, hd * DV_B:(hd + 1) * DV_B]
        y_ref[sb, :, hd * DV_B:(hd + 1) * DV_B] = (jax.nn.sigmoid(og) * yn).astype(BF16)
        ct_ref[sb, hd] = col(gt["decay4"], hd) * cts[h] + col(gt["rho4"], hd) * pu[h][L:]

    @pl.when(j == pl.num_programs(1) - 1)
    def _():
        for sb, hd in heads:
            c_ref[sb, hd] = ct_ref[sb, hd, :, :DV_B].T
            n_ref[sb, hd] = ct_ref[sb, hd, :, DV_B:DV_B + 1]


def _mlstm_call(qm, kt, vm, og, a_row, b_col, a_col, norm_w, c0, n0, m0, *, SB, L):
    N, T, _ = qm.shape

    def tok(c):
        return pl.BlockSpec((SB, L, c), lambda i, j: (i, j, 0))

    def tr(c):
        return pl.BlockSpec((SB, c, L), lambda i, j: (i, 0, j))

    st_c = pl.BlockSpec((SB, N_HEADS_B, DV_B, DQK_B), lambda i, j: (i, 0, 0, 0))
    st_n = pl.BlockSpec((SB, N_HEADS_B, DQK_B, 1), lambda i, j: (i, 0, 0, 0))
    st_m = pl.BlockSpec((SB, 1, LANES), lambda i, j: (i, 0, 0))
    return pl.pallas_call(
        functools.partial(_mlstm_kernel, SB=SB, L=L),
        grid=(N // SB, T // L),
        in_specs=[tok(QK_B), tr(QK_B), tok(V_B), tok(V_B), tr(8), tok(LANES), tok(LANES),
                  _const_spec((1, V_B)), st_c, st_n, st_m],
        out_specs=[tok(V_B), st_c, st_n, st_m],
        out_shape=[jax.ShapeDtypeStruct((N, T, V_B), BF16),
                   jax.ShapeDtypeStruct((N, N_HEADS_B, DV_B, DQK_B), F32),
                   jax.ShapeDtypeStruct((N, N_HEADS_B, DQK_B, 1), F32),
                   jax.ShapeDtypeStruct((N, 1, LANES), F32)],
        scratch_shapes=[pltpu.VMEM((SB, N_HEADS_B, DQK_B, DV_B + LANES), F32)],
        compiler_params=pltpu.CompilerParams(dimension_semantics=("parallel", "arbitrary")),
        name="mlstm",
    )(qm, kt, vm, og, a_row, b_col, a_col, norm_w, c0, n0, m0)


def _merge_kernel(x_ref, mod_ref, ya_ref, yb_ref, ga_ref, gb_ref,
                  wpa_ref, wpb_ref, wo_ref, l1w_ref, l1b_ref, x1_ref, *, S, R):
    SR = S * R
    g1 = mod_ref[:, 2:3, :]
    pa = jnp.dot(ya_ref[...].reshape(SR, Q_A), wpa_ref[...], preferred_element_type=F32)
    pb = jnp.dot(yb_ref[...].reshape(SR, V_B), wpb_ref[...], preferred_element_type=F32)
    merged = (jax.nn.sigmoid(ga_ref[...].reshape(SR, D_MODEL)) * pa
              + jax.nn.sigmoid(gb_ref[...].reshape(SR, D_MODEL)) * pb)
    merged = merged.astype(BF16).reshape(S, R, D_MODEL)
    for lo in range(0, R, R // EPI_SPLIT):
        rows = slice(lo, lo + R // EPI_SPLIT)
        mo = jnp.dot(merged[:, rows].reshape(S * (R // EPI_SPLIT), D_MODEL), wo_ref[...],
                     preferred_element_type=F32).reshape(S, R // EPI_SPLIT, D_MODEL)
        x1_ref[:, rows, :] = _layer_norm(ALPHA * x_ref[:, rows, :] + g1 * mo, l1w_ref[...], l1b_ref[...])


def _merge_call(x, mod, ya, yb, ga, gb, wpa, wpb, wo, l1w, l1b, *, S, R):
    N, T, _ = x.shape

    def tok(c):
        return pl.BlockSpec((S, R, c), lambda i, j: (i, j, 0))

    return pl.pallas_call(
        functools.partial(_merge_kernel, S=S, R=R),
        grid=(N // S, T // R),
        in_specs=[tok(D_MODEL), pl.BlockSpec((S, 6, D_MODEL), lambda i, j: (i, 0, 0)),
                  tok(Q_A), tok(V_B), tok(D_MODEL), tok(D_MODEL),
                  _const_spec((Q_A, D_MODEL)), _const_spec((V_B, D_MODEL)), _const_spec((D_MODEL, D_MODEL)),
                  _const_spec((1, D_MODEL)), _const_spec((1, D_MODEL))],
        out_specs=tok(D_MODEL),
        out_shape=jax.ShapeDtypeStruct((N, T, D_MODEL), F32),
        compiler_params=pltpu.CompilerParams(dimension_semantics=("parallel", "parallel")),
        name="merge",
    )(x, mod, ya, yb, ga, gb, wpa, wpb, wo, l1w, l1b)


def _ffn_kernel(x1_ref, mod_ref, cin_ref, wup_ref, bup_ref, cw_ref, cb_ref, wdn_ref, l2w_ref, l2b_ref,
                y_ref, cout_ref, carry_ref, act_ref, *perm_refs, S, R, permute):
    SR = S * R
    FC = D_FF // FFN_CHUNKS
    NV = R // SUBLANES
    NT = D_MODEL // LANES

    @pl.when(pl.program_id(1) == 0)
    def _():
        carry_ref[...] = cin_ref[...] - bup_ref[...]

    sh2 = mod_ref[:, 3:4, :]
    sc2 = mod_ref[:, 4:5, :]
    g2 = mod_ref[:, 5:6, :]

    def regroup(src_ref, start_of, stride):
        return jnp.concatenate(
            [jnp.concatenate([src_ref[s, c, pl.ds(start_of(v), SUBLANES, stride=stride), :] for c in range(NT)],
                             axis=1) for s in range(S) for v in range(NV)], axis=0).reshape(S, R, D_MODEL)

    if permute:
        (tiles_ref,) = perm_refs
        for c in range(NT):
            tiles_ref[:, c] = x1_ref[:, :, c * LANES:(c + 1) * LANES]
        x1 = regroup(tiles_ref, lambda v: v, NV)
    else:
        x1 = x1_ref[...]
    h2 = (x1 * (1.0 + sc2) + sh2).reshape(SR, D_MODEL).astype(BF16)
    row8 = lax.broadcasted_iota(jnp.int32, (SUBLANES, FC), 0)

    def wrap(group, first):
        return jnp.where(row8 == 0, first, pltpu.roll(group, 1, axis=0))

    def back_permuted(blk, c0, c1):
        g_last = wrap(blk[(NV - 1) * SUBLANES:], c1)
        g_prev = wrap(blk[(NV - 2) * SUBLANES:(NV - 1) * SUBLANES], c0)
        new = jnp.concatenate([blk[(NV - 1) * SUBLANES - 1:(NV - 1) * SUBLANES], blk[R - 1:R]], axis=0)
        return ([g_last, blk[:(NV - 1) * SUBLANES]], [g_prev, g_last, blk[:(NV - 2) * SUBLANES]], new)

    def back_natural(blk, c0, c1):
        b1 = pltpu.roll(blk, 1, axis=0)
        b1 = [jnp.where(row8 == 0, c1, b1[:SUBLANES]), b1[SUBLANES:]]
        b2 = pltpu.roll(blk, 2, axis=0)
        b2 = [jnp.where(row8 == 0, c0, jnp.where(row8 == 1, c1, b2[:SUBLANES])), b2[SUBLANES:]]
        return b1, b2, blk[R - 2:]

    def conv(lo):
        cs = slice(lo, lo + FC)
        u = jnp.dot(h2, wup_ref[:, cs], preferred_element_type=F32)
        w0, w1, w2 = cw_ref[0:1, cs], cw_ref[1:2, cs], cw_ref[2:3, cs]
        const = cb_ref[:, cs] + bup_ref[:, cs] * (w0 + w1 + w2)
        back1, back2 = [], []
        for s in range(S):
            b1, b2, new = (back_permuted if permute else back_natural)(
                u[s * R:(s + 1) * R], carry_ref[s, 0:1, cs], carry_ref[s, 1:2, cs])
            back1 += b1
            back2 += b2
            carry_ref[s, :, cs] = new
            cout_ref[s, :, cs] = new + bup_ref[:, cs]
        return w0 * jnp.concatenate(back2, axis=0) + w1 * jnp.concatenate(back1, axis=0) + w2 * u + const

    for ci in range(FFN_CHUNKS):
        lo = ci * FC
        a = conv(lo)
        g = conv(D_FF + lo)
        act_ref[:, lo:lo + FC] = (a * jax.nn.gelu(g, approximate=True)).astype(BF16)

    parts = []
    for k in range(EPI_SPLIT):
        if S == 1:
            sl_s, sl_r, n = slice(0, 1), slice(k * R // EPI_SPLIT, (k + 1) * R // EPI_SPLIT), R // EPI_SPLIT
            rows = sl_r
        else:
            sl_s, sl_r, n = slice(k * S // EPI_SPLIT, (k + 1) * S // EPI_SPLIT), slice(0, R), SR // EPI_SPLIT
            rows = slice(k * SR // EPI_SPLIT, (k + 1) * SR // EPI_SPLIT)
        f = jnp.dot(act_ref[rows, :], wdn_ref[...], preferred_element_type=F32)
        f = f.reshape((1, n, D_MODEL) if S == 1 else (S // EPI_SPLIT, R, D_MODEL))
        parts.append(_layer_norm(ALPHA * x1[sl_s, sl_r] + g2[sl_s] * f, l2w_ref[...], l2b_ref[...]))
    y = jnp.concatenate(parts, axis=1 if S == 1 else 0)
    if permute:
        for c in range(NT):
            tiles_ref[:, c] = y[:, :, c * LANES:(c + 1) * LANES]
        y = regroup(tiles_ref, lambda v: ((SUBLANES * v) % NV) * SUBLANES + (SUBLANES * v) // NV, SUBLANES)
    y_ref[...] = y


def _ffn_call(x1, mod, conv_in, wup, bup, cw, cb, wdn, l2w, l2b, *, S, R):
    N, T, _ = x1.shape
    permute = (R // SUBLANES) % SUBLANES == 0

    def tok(c):
        return pl.BlockSpec((S, R, c), lambda i, j: (i, j, 0))

    conv_spec = pl.BlockSpec((S, CONV_W - 1, 2 * D_FF), lambda i, j: (i, 0, 0))
    return pl.pallas_call(
        functools.partial(_ffn_kernel, S=S, R=R, permute=permute),
        grid=(N // S, T // R),
        in_specs=[tok(D_MODEL), pl.BlockSpec((S, 6, D_MODEL), lambda i, j: (i, 0, 0)), conv_spec,
                  _const_spec((D_MODEL, 2 * D_FF)), _const_spec((1, 2 * D_FF)),
                  _const_spec((CONV_W, 2 * D_FF)), _const_spec((1, 2 * D_FF)),
                  _const_spec((D_FF, D_MODEL)), _const_spec((1, D_MODEL)), _const_spec((1, D_MODEL))],
        out_specs=[tok(D_MODEL), conv_spec],
        out_shape=[jax.ShapeDtypeStruct((N, T, D_MODEL), F32),
                   jax.ShapeDtypeStruct((N, CONV_W - 1, 2 * D_FF), F32)],
        scratch_shapes=[pltpu.VMEM((S, CONV_W - 1, 2 * D_FF), F32), pltpu.VMEM((S * R, D_FF), BF16)]
        + ([pltpu.VMEM((S, D_MODEL // LANES, R, LANES), F32)] if permute else []),
        compiler_params=pltpu.CompilerParams(dimension_semantics=("parallel", "arbitrary")),
        name="ffn",
    )(x1, mod, conv_in, wup, bup, cw, cb, wdn, l2w, l2b)


def kernel(x_prompt, x_sample, cache_attn_k, cache_attn_v, state_mlstm_C, state_mlstm_n, state_mlstm_m, state_ffn_conv, c_prompt, c_sample, w_ada, b_ada, w_in, b_igate, b_fgate, attn_sinks, mlstm_norm_w, w_proj_a, w_proj_b, w_out, ln1_w, ln1_b, w_up, b_up, conv_w, conv_b, w_down, ln2_w, ln2_b):
    Bp, Tp, D = x_prompt.shape
    Bs, Ts, _ = x_sample.shape
    l = 0

    mod = _mod_call(jnp.concatenate([c_prompt, c_sample], axis=0), w_ada[l], b_ada[l][None])
    mod_p = mod[:Bp].reshape(Bp, 6, D)
    mod_s = mod[Bp:].reshape(Bs, 6, D)

    w = w_in[l]
    o_i, o_f, o_o = Z_OFF[6], Z_OFF[7], Z_OFF[8]
    lane_pad = ((0, 0), (0, LANES - N_HEADS_B))
    w_main = jnp.concatenate([w[:, :o_i], w[:, o_o:],
                              jnp.pad(w[:, o_i:o_f], lane_pad), jnp.pad(w[:, o_f:o_o], lane_pad)],
                             axis=1).astype(BF16)
    b_ic = jnp.pad(b_igate[l][None], lane_pad)
    b_fc = jnp.pad(b_fgate[l][None], lane_pad)
    inproj_w = (w_main, b_ic, b_fc)
    sinks = attn_sinks[l]
    norm_w = mlstm_norm_w[l][None]
    merge_w = (w_proj_a[l].astype(BF16), w_proj_b[l].astype(BF16), w_out[l].astype(BF16),
               ln1_w[l][None], ln1_b[l][None])
    ffn_w = (w_up[l].astype(BF16), b_up[l][None], conv_w[l], conv_b[l][None],
             w_down[l].astype(BF16), ln2_w[l][None], ln2_b[l][None])

    qa, ka, va, qm, kt, vm, og, ga, gb, a_row, b_col, a_col = _inproj_call(
        x_prompt, mod_p, *inproj_w, S=1, R=INPROJ_ROWS, L=MLSTM_CHUNK, q_transposed=True)
    ya = _attn_prompt_call(qa, ka, va, sinks)
    yb, p_c, p_n, p_m = _mlstm_call(
        qm, kt, vm, og, a_row, b_col, a_col, norm_w,
        jnp.zeros((Bp, N_HEADS_B, DV_B, DQK_B), F32), jnp.zeros((Bp, N_HEADS_B, DQK_B, 1), F32),
        jnp.zeros((Bp, 1, LANES), F32), SB=4, L=MLSTM_CHUNK)
    p_m = p_m[:, 0, :N_HEADS_B]
    x1 = _merge_call(x_prompt, mod_p, ya, yb, ga, gb, *merge_w, S=1, R=POST_ROWS)
    y_prompt, p_conv = _ffn_call(x1, mod_p, jnp.zeros((Bp, CONV_W - 1, 2 * D_FF), F32), *ffn_w, S=1, R=POST_ROWS)
    p_k = ka[:, Tp - KV_WIN:].reshape(Bp, KV_WIN, N_KV_A, HEAD_DIM_A)
    p_v = va[:, Tp - KV_WIN:].reshape(Bp, KV_WIN, N_KV_A, HEAD_DIM_A)

    qa, ka, va, qm, kt, vm, og, ga, gb, a_row, b_col, a_col = _inproj_call(
        x_sample, mod_s, *inproj_w, S=16, R=Ts, L=Ts, q_transposed=False)
    ya, s_k, s_v = _attn_sample_call(qa, ka, va, cache_attn_k[l].reshape(Bs, KV_WIN, KV_A_W),
                                     cache_attn_v[l].reshape(Bs, KV_WIN, KV_A_W), sinks, NS=8)
    yb, s_c, s_n, s_m = _mlstm_call(
        qm, kt, vm, og, a_row, b_col, a_col, norm_w,
        state_mlstm_C[l], state_mlstm_n[l][:, :, :, None], jnp.pad(state_mlstm_m[l][:, None, :], ((0, 0),) + lane_pad),
        SB=4, L=Ts)
    s_m = s_m[:, 0, :N_HEADS_B]
    x1 = _merge_call(x_sample, mod_s, ya, yb, ga, gb, *merge_w, S=16, R=Ts)
    y_sample, s_conv = _ffn_call(x1, mod_s, state_ffn_conv[l], *ffn_w, S=8, R=Ts)

    def st(a, shape):
        return a.reshape((1,) + shape)

    return (y_prompt, y_sample,
            st(p_k, (Bp, KV_WIN, N_KV_A, HEAD_DIM_A)), st(p_v, (Bp, KV_WIN, N_KV_A, HEAD_DIM_A)),
            st(p_c, (Bp, N_HEADS_B, DV_B, DQK_B)), st(p_n, (Bp, N_HEADS_B, DQK_B)), st(p_m, (Bp, N_HEADS_B)),
            st(p_conv, (Bp, CONV_W - 1, 2 * D_FF)),
            st(s_k, (Bs, KV_WIN, N_KV_A, HEAD_DIM_A)), st(s_v, (Bs, KV_WIN, N_KV_A, HEAD_DIM_A)),
            st(s_c, (Bs, N_HEADS_B, DV_B, DQK_B)), st(s_n, (Bs, N_HEADS_B, DQK_B)), st(s_m, (Bs, N_HEADS_B)),
            st(s_conv, (Bs, CONV_W - 1, 2 * D_FF)))
```

```python
import functools

import numpy as np
import jax
import jax.numpy as jnp
from jax import lax
from jax.experimental import pallas as pl
from jax.experimental.pallas import tpu as pltpu

F32 = jnp.float32
BF16 = jnp.bfloat16

D_MODEL = 1024
DEPTH = 1
PAST_LEN = 2048
CHUNK = 64
N_HEADS_A = 16
N_KV_A = 2
HEAD_DIM_A = 64
GROUP_A = N_HEADS_A // N_KV_A
WINDOW = 128
WINDOW_CHUNKS = WINDOW // CHUNK
KV_WIN = min(WINDOW, PAST_LEN)
N_HEADS_B = 4
DQK_B = 128
DV_B = 256
D_FF = 2816
CONV_W = 3
LN_EPS = 1e-5
ALPHA = (2 * DEPTH) ** 0.25
Q_A = N_HEADS_A * HEAD_DIM_A
KV_A_W = N_KV_A * HEAD_DIM_A
QK_B = N_HEADS_B * DQK_B
V_B = N_HEADS_B * DV_B
Z_PARTS = (Q_A, KV_A_W, KV_A_W, QK_B, QK_B, V_B, N_HEADS_B, N_HEADS_B, V_B, D_MODEL, D_MODEL)
Z_OFF = tuple(int(v) for v in np.cumsum((0,) + Z_PARTS))

LANES = 128
SUBLANES = 8
LOG2E = float(np.log2(np.e))

M_QA = 0
M_KA = M_QA + Q_A
M_VA = M_KA + KV_A_W
M_QM = M_VA + KV_A_W
M_KM = M_QM + QK_B
M_VM = M_KM + QK_B
M_OG = M_VM + V_B
M_GI = M_OG + V_B
M_GF = M_GI + LANES
M_END = M_GF + LANES
ATT_QB = 128
ATT_SUB = 8
ATT_W = 256
FFN_CHUNKS = 11
MLSTM_CHUNK = 256
INPROJ_ROWS = 512
POST_ROWS = 512
EPI_SPLIT = 2


def _bdot(a, b):
    return jnp.dot(a.astype(BF16), b.astype(BF16), preferred_element_type=F32)


def _bdot_nt(a, b):
    return lax.dot_general(a.astype(BF16), b.astype(BF16), (((1,), (1,)), ((), ())),
                           preferred_element_type=F32)


def _const_spec(shape):
    nd = len(shape)
    return pl.BlockSpec(shape, lambda *_: (0,) * nd, pipeline_mode=pl.Buffered(1))


def _layer_norm(x, w, b):
    mu = jnp.mean(x, -1, keepdims=True)
    xc = x - mu
    var = jnp.mean(jnp.square(xc), -1, keepdims=True)
    return xc * lax.rsqrt(var + LN_EPS) * w + b


def _mod_kernel(c_ref, w_ref, b_ref, o_ref):
    c = c_ref[...]
    s = c * jax.nn.sigmoid(c)
    o_ref[...] = _bdot(s, w_ref[...]) + b_ref[...]


def _mod_call(c_all, w_ada, b_ada):
    n = c_all.shape[0]
    tn = 1536
    return pl.pallas_call(
        _mod_kernel,
        grid=(6 * D_MODEL // tn,),
        in_specs=[pl.BlockSpec((n, D_MODEL), lambda j: (0, 0)),
                  pl.BlockSpec((D_MODEL, tn), lambda j: (0, j)),
                  pl.BlockSpec((1, tn), lambda j: (0, j))],
        out_specs=pl.BlockSpec((n, tn), lambda j: (0, j)),
        out_shape=jax.ShapeDtypeStruct((n, 6 * D_MODEL), F32),
        name="mod",
    )(c_all, w_ada, b_ada)


def _log_sigmoid(x):
    return jnp.minimum(x, 0.0) - jnp.log1p(jnp.exp(-jnp.abs(x)))


def _seg_scan(x, L, axis, op, ident):
    pos = lax.broadcasted_iota(jnp.int32, x.shape, axis) % L
    step = 1
    while step < L:
        x = op(x, jnp.where(pos >= step, pltpu.roll(x, step, axis=axis), ident))
        step *= 2
    return x


def _inproj_kernel(x_ref, mod_ref, w_ref, bic_ref, bfc_ref,
                   qa_ref, ka_ref, va_ref, qm_ref, kt_ref, vm_ref, og_ref,
                   ar_ref, bc_ref, ac_ref, *, S, R, L, q_transposed):
    SR = S * R
    sh = mod_ref[:, 0:1, :]
    sc = mod_ref[:, 1:2, :]
    h = (x_ref[...] * (1.0 + sc) + sh).reshape(SR, D_MODEL).astype(BF16)

    def proj(lo, hi):
        return jnp.dot(h, w_ref[:, lo:hi], preferred_element_type=F32)

    gates = proj(M_GI, M_END)
    q = proj(M_QA, M_KA) * (HEAD_DIM_A ** -0.5)
    km = proj(M_KM, M_VM) * (DQK_B ** -0.5)

    i_col = gates[:, :LANES] + bic_ref[...]
    logf_col = _log_sigmoid(gates[:, LANES:] + bfc_ref[...])
    b_col = _seg_scan(logf_col, L, 0, jnp.add, 0.0)
    a_col = _seg_scan(i_col - b_col, L, 0, jnp.maximum, -jnp.inf)
    bc_ref[...] = b_col.reshape(S, R, LANES)
    ac_ref[...] = a_col.reshape(S, R, LANES)
    i_row = i_col.T[:8]
    logf_row = logf_col.T[:8]
    a_row = i_row - _seg_scan(logf_row, L, 1, jnp.add, 0.0)
    kt = km.T.astype(BF16)
    for s in range(S):
        kt_ref[s] = kt[:, s * R:(s + 1) * R]
        ar_ref[s] = a_row[:, s * R:(s + 1) * R]
    if q_transposed:
        qt = (q * LOG2E).T.astype(BF16)
        for s in range(S):
            qa_ref[s] = qt[:, s * R:(s + 1) * R]
    else:
        qa_ref[...] = q.astype(BF16).reshape(S, R, Q_A)

    kv = proj(M_KA, M_QM)
    ka_ref[...] = kv[:, :KV_A_W].reshape(S, R, KV_A_W)
    va_ref[...] = kv[:, KV_A_W:].reshape(S, R, KV_A_W)
    qm_ref[...] = proj(M_QM, M_KM).astype(BF16).reshape(S, R, QK_B)
    vm_ref[...] = proj(M_VM, M_OG).astype(BF16).reshape(S, R, V_B)
    og_ref[...] = proj(M_OG, M_GI).reshape(S, R, V_B)


def _inproj_call(x, mod, w_main, b_ic, b_fc, *, S, R, L, q_transposed):
    N, T, _ = x.shape
    nsb, nrb = N // S, T // R

    def tok(c):
        return pl.BlockSpec((S, R, c), lambda i, j: (i, j, 0))

    def tok_shape(c, dt):
        return jax.ShapeDtypeStruct((N, T, c), dt)

    def tr(c):
        return pl.BlockSpec((S, c, R), lambda i, j: (i, 0, j))

    return pl.pallas_call(
        functools.partial(_inproj_kernel, S=S, R=R, L=L, q_transposed=q_transposed),
        grid=(nsb, nrb),
        in_specs=[tok(D_MODEL),
                  pl.BlockSpec((S, 6, D_MODEL), lambda i, j: (i, 0, 0)),
                  _const_spec((D_MODEL, M_END)),
                  _const_spec((1, LANES)), _const_spec((1, LANES))],
        out_specs=[tr(Q_A) if q_transposed else tok(Q_A), tok(KV_A_W), tok(KV_A_W), tok(QK_B), tr(QK_B), tok(V_B),
                   tok(V_B), tr(8), tok(LANES), tok(LANES)],
        out_shape=[jax.ShapeDtypeStruct((N, Q_A, T), BF16) if q_transposed else tok_shape(Q_A, BF16),
                   tok_shape(KV_A_W, F32), tok_shape(KV_A_W, F32),
                   tok_shape(QK_B, BF16), jax.ShapeDtypeStruct((N, QK_B, T), BF16), tok_shape(V_B, BF16),
                   tok_shape(V_B, F32),
                   jax.ShapeDtypeStruct((N, 8, T), F32), tok_shape(LANES, F32), tok_shape(LANES, F32)],
        compiler_params=pltpu.CompilerParams(dimension_semantics=("parallel", "parallel")),
        name="inproj",
    )(x, mod, w_main, b_ic, b_fc)


def _alibi_slopes():
    return 2.0 ** (-8.0 * np.arange(1, N_HEADS_A + 1, dtype=np.float64) / N_HEADS_A)


def _attn_bias(qpos, kpos, kvalid, transposed=False):
    qpos = np.asarray(qpos)[:, None]
    kpos = np.asarray(kpos)[None, :]
    qc, kc = qpos // CHUNK, kpos // CHUNK
    visible = (kpos >= 0) & (kc <= qc) & (kc >= qc - WINDOW_CHUNKS) & np.asarray(kvalid)[None, :]
    dist = np.abs(qpos - kpos).astype(np.float64)
    slopes = _alibi_slopes()
    out = np.zeros((N_KV_A, GROUP_A // 2, qpos.shape[0], 2, kpos.shape[1]), np.float32)
    for kv in range(N_KV_A):
        for p in range(GROUP_A // 2):
            for half in range(2):
                hd = kv * GROUP_A + 2 * p + half
                out[kv, p, :, half, :] = np.where(visible, -slopes[hd] * dist, -np.inf)
    if transposed:
        return out.transpose(0, 3, 4, 1, 2).reshape(N_KV_A, 2 * kpos.shape[1], (GROUP_A // 2) * qpos.shape[0])
    return out.reshape(N_KV_A, (GROUP_A // 2) * qpos.shape[0], 2 * kpos.shape[1])


def _attn_core(q, kwin, vwin, bias_of, sinks_ref, store):
    Mq = q.shape[0]
    W = kwin.shape[0]
    npair = GROUP_A // 2
    lane = lax.broadcasted_iota(jnp.int32, (npair * Mq, LANES), 1)
    for kv in range(N_KV_A):
        kk = kwin[:, kv * HEAD_DIM_A:(kv + 1) * HEAD_DIM_A].astype(BF16)
        vv = vwin[:, kv * HEAD_DIM_A:(kv + 1) * HEAD_DIM_A].astype(BF16)
        zero = jnp.zeros_like(kk)
        one = jnp.ones_like(vv)
        k2 = jnp.concatenate([jnp.concatenate([kk, zero], axis=1),
                              jnp.concatenate([zero, kk], axis=1)], axis=0)
        v_aug = jnp.concatenate([vv, one, one, vv], axis=1)
        qs = jnp.concatenate([q[:, (kv * npair + p) * LANES:(kv * npair + p + 1) * LANES]
                              for p in range(npair)], axis=0)
        s = _bdot_nt(qs, k2) + bias_of(kv)
        probs, sink_terms = [], []
        for half in range(2):
            sh = s[:, half * W:(half + 1) * W]
            sink = jnp.concatenate(
                [jnp.full((Mq, 1), sinks_ref[kv * GROUP_A + 2 * p + half], F32) for p in range(npair)], axis=0)
            mx = jnp.maximum(jnp.max(sh, axis=1, keepdims=True), sink)
            probs.append(jnp.exp(sh - mx).astype(BF16))
            sink_terms.append(jnp.exp(sink - mx))
        o = jnp.dot(jnp.concatenate(probs, axis=0), v_aug, preferred_element_type=F32)
        oe, oo = o[:npair * Mq], o[npair * Mq:]
        ye = oe[:, :LANES] / (oe[:, LANES:] + sink_terms[0])
        yo = oo[:, LANES:] / (oo[:, :LANES] + sink_terms[1])
        y = jnp.where(lane < HEAD_DIM_A, ye, yo).astype(BF16)
        for p in range(npair):
            store((kv * npair + p) * LANES, y[p * Mq:(p + 1) * Mq])


def _attn_prompt_kernel(sinks_ref, qt_ref, kp_ref, kc_ref, vp_ref, vc_ref, bias0_ref, bias1_ref, o_ref):
    Mq = ATT_QB
    npair = GROUP_A // 2
    keys = jnp.concatenate([kp_ref[0], kc_ref[0]], axis=0)
    vals_t = jnp.concatenate([vp_ref[0], vc_ref[0]], axis=0).T.astype(BF16)
    ones = jnp.ones((HEAD_DIM_A, ATT_W), BF16)
    pair_of_lane = lax.broadcasted_iota(jnp.int32, (1, npair * Mq), 1) // Mq
    units = [(sub, kv) for sub in range(ATT_SUB) for kv in range(N_KV_A)]
    sts = {}
    for sub, kv in units:
        kk = keys[sub * Mq:sub * Mq + ATT_W, kv * HEAD_DIM_A:(kv + 1) * HEAD_DIM_A].astype(BF16)
        zero = jnp.zeros_like(kk)
        k2 = jnp.concatenate([jnp.concatenate([kk, zero], axis=1),
                              jnp.concatenate([zero, kk], axis=1)], axis=0)
        qt = jnp.concatenate([qt_ref[0, (kv * npair + p) * LANES:(kv * npair + p + 1) * LANES, sub * Mq:(sub + 1) * Mq]
                              for p in range(npair)], axis=1)
        sts[sub, kv] = jnp.dot(k2, qt, preferred_element_type=F32)
    probs, sink_terms = {}, {}
    for sub, kv in units:
        st = sts[sub, kv] + (bias0_ref[0, kv] if sub == 0 else bias1_ref[0, kv])
        for half in range(2):
            sh = st[half * ATT_W:(half + 1) * ATT_W]
            sink = jnp.zeros((1, npair * Mq), F32)
            for p in range(npair):
                sink = jnp.where(pair_of_lane == p, sinks_ref[kv * GROUP_A + 2 * p + half] * LOG2E, sink)
            mx = jnp.maximum(jnp.max(sh, axis=0, keepdims=True), sink)
            probs[sub, kv, half] = jnp.exp2(sh - mx).astype(BF16)
            sink_terms[sub, kv, half] = jnp.exp2(sink - mx)
    ots = {}
    for sub, kv in units:
        vt = vals_t[kv * HEAD_DIM_A:(kv + 1) * HEAD_DIM_A, sub * Mq:sub * Mq + ATT_W]
        for half in range(2):
            lhs = jnp.concatenate([vt, ones] if half == 0 else [ones, vt], axis=0)
            ots[sub, kv, half] = jnp.dot(lhs, probs[sub, kv, half], preferred_element_type=F32)
    for sub, kv in units:
        oe, oo = ots[sub, kv, 0], ots[sub, kv, 1]
        ye = oe[:HEAD_DIM_A] / (oe[HEAD_DIM_A:] + sink_terms[sub, kv, 0])
        yo = oo[HEAD_DIM_A:] / (oo[:HEAD_DIM_A] + sink_terms[sub, kv, 1])
        yt = jnp.concatenate([ye, yo], axis=0)
        for p in range(npair):
            col = (kv * npair + p) * LANES
            o_ref[0, sub * Mq:(sub + 1) * Mq, col:col + LANES] = yt[:, p * Mq:(p + 1) * Mq].T.astype(BF16)


def _attn_prompt_call(qa_t, ka, va, sinks):
    N, _, T = qa_t.shape
    rows = ATT_QB * ATT_SUB
    rel_q = ATT_QB + np.arange(ATT_QB)
    rel_k = np.arange(ATT_W)
    bias = np.stack([_attn_bias(rel_q, rel_k, rel_k >= ATT_QB, transposed=True),
                     _attn_bias(rel_q, rel_k, rel_k >= 0, transposed=True)]) * np.float32(LOG2E)
    kv_prev = pl.BlockSpec((1, ATT_QB, KV_A_W), lambda n, j: (n, jnp.maximum(ATT_SUB * j - 1, 0), 0))
    kv_cur = pl.BlockSpec((1, rows, KV_A_W), lambda n, j: (n, j, 0))
    return pl.pallas_call(
        _attn_prompt_kernel,
        grid=(N, T // rows),
        in_specs=[pl.BlockSpec(memory_space=pltpu.SMEM),
                  pl.BlockSpec((1, Q_A, rows), lambda n, j: (n, 0, j)),
                  kv_prev, kv_cur, kv_prev, kv_cur,
                  pl.BlockSpec((1,) + bias.shape[1:], lambda n, j: (jnp.minimum(j, 1), 0, 0, 0)),
                  pl.BlockSpec((1,) + bias.shape[1:], lambda n, j: (1, 0, 0, 0), pipeline_mode=pl.Buffered(1))],
        out_specs=pl.BlockSpec((1, rows, Q_A), lambda n, j: (n, j, 0)),
        out_shape=jax.ShapeDtypeStruct((N, T, Q_A), BF16),
        compiler_params=pltpu.CompilerParams(dimension_semantics=("parallel", "parallel")),
        name="attn_prompt",
    )(sinks, qa_t, ka, ka, va, va, jnp.asarray(bias), jnp.asarray(bias))


def _attn_sample_kernel(sinks_ref, q_ref, kn_ref, vn_ref, kc_ref, vc_ref, bias_ref,
                        o_ref, ko_ref, vo_ref, *, T, NS):
    pad = jnp.zeros((ATT_W - KV_WIN - T, KV_A_W), F32)
    for s in range(NS):
        kwin = jnp.concatenate([kc_ref[s], kn_ref[s], pad], axis=0)
        vwin = jnp.concatenate([vc_ref[s], vn_ref[s], pad], axis=0)
        ko_ref[s] = kwin[T:T + KV_WIN]
        vo_ref[s] = vwin[T:T + KV_WIN]

        def store(col, val, s=s):
            o_ref[s, :, col:col + LANES] = val

        _attn_core(q_ref[s], kwin, vwin, lambda kv: bias_ref[kv], sinks_ref, store)


def _attn_sample_call(qa, ka, va, cache_k, cache_v, sinks, *, NS):
    N, T, _ = qa.shape
    qpos = PAST_LEN + np.arange(T)
    kpos = PAST_LEN - KV_WIN + np.arange(ATT_W)
    bias = _attn_bias(qpos, kpos, np.arange(ATT_W) < KV_WIN + T)
    new = pl.BlockSpec((NS, T, KV_A_W), lambda n: (n, 0, 0))
    cache = pl.BlockSpec((NS, KV_WIN, KV_A_W), lambda n: (n, 0, 0))
    cache_shape = jax.ShapeDtypeStruct((N, KV_WIN, KV_A_W), F32)
    return pl.pallas_call(
        functools.partial(_attn_sample_kernel, T=T, NS=NS),
        grid=(N // NS,),
        in_specs=[pl.BlockSpec(memory_space=pltpu.SMEM),
                  pl.BlockSpec((NS, T, Q_A), lambda n: (n, 0, 0)),
                  new, new, cache, cache,
                  _const_spec(bias.shape)],
        out_specs=[pl.BlockSpec((NS, T, Q_A), lambda n: (n, 0, 0)), cache, cache],
        out_shape=[jax.ShapeDtypeStruct((N, T, Q_A), BF16), cache_shape, cache_shape],
        compiler_params=pltpu.CompilerParams(dimension_semantics=("parallel",)),
        name="attn_sample",
    )(sinks, qa, ka, va, cache_k, cache_v, jnp.asarray(bias))


def _mlstm_kernel(q_ref, kt_ref, v_ref, og_ref, ar_ref, bc_ref, ac_ref, nw_ref, c0_ref, n0_ref, m0_ref,
                  y_ref, c_ref, n_ref, m_ref, ct_ref, *, SB, L):
    j = pl.program_id(1)
    heads = [(sb, hd) for sb in range(SB) for hd in range(N_HEADS_B)]

    @pl.when(j == 0)
    def _():
        for sb, hd in heads:
            ct_ref[sb, hd, :, :DV_B] = c0_ref[sb, hd].T
            ct_ref[sb, hd, :, DV_B:] = jnp.broadcast_to(n0_ref[sb, hd], (DQK_B, LANES))
        m_ref[...] = m0_ref[...]

    causal = (lax.broadcasted_iota(jnp.int32, (L, L), 0) >= lax.broadcasted_iota(jnp.int32, (L, L), 1))
    ones = jnp.ones((L, LANES), BF16)
    gates = []
    for sb in range(SB):
        a4 = ac_ref[sb]
        b4 = bc_ref[sb]
        m_row = m_ref[sb]
        a_last = a4[L - 1:L]
        b_last = b4[L - 1:L]
        mx = jnp.maximum(a_last, m_row)
        m_new = b_last + mx
        gates.append(dict(a4=a4, b4=b4, a_last=a_last, m_row=m_row, decay4=jnp.exp(b_last + m_row - m_new),
                          rho4=jnp.exp(a_last - mx)))
        m_ref[sb] = m_new

    def lanes(x):
        return jnp.broadcast_to(x, (x.shape[0], LANES))

    def col(x, hd):
        return x[:, hd:hd + 1]

    qs = {h: q_ref[h[0], :, h[1] * DQK_B:(h[1] + 1) * DQK_B] for h in heads}
    kts = {h: kt_ref[h[0], h[1] * DQK_B:(h[1] + 1) * DQK_B, :] for h in heads}
    cts = {h: ct_ref[h[0], h[1]] for h in heads}
    qk = {h: jnp.dot(qs[h], kts[h], preferred_element_type=F32) for h in heads}
    qc = {h: jnp.dot(qs[h], cts[h].astype(BF16), preferred_element_type=F32) for h in heads}
    lhs, a_bs = {}, {}
    for h in heads:
        sb, hd = h
        gt = gates[sb]
        a_row = ar_ref[sb, hd:hd + 1, :]
        a_bs[h] = lanes(col(gt["a4"], hd))
        a_full = jnp.concatenate([a_bs[h]] * (L // LANES), axis=1) if L >= LANES else a_bs[h][:, :L]
        w = jnp.exp(jnp.where(causal, a_row - a_full, -jnp.inf))
        s = (qk[h] * w).astype(BF16)
        kw = (kts[h].astype(F32) * jnp.exp(a_row - col(gt["a_last"], hd))).astype(BF16)
        lhs[h] = jnp.concatenate([s, kw], axis=0)
    pu = {}
    for h in heads:
        sb, hd = h
        v_aug = jnp.concatenate([v_ref[sb, :, hd * DV_B:(hd + 1) * DV_B], ones], axis=1)
        pu[h] = jnp.dot(lhs[h], v_aug, preferred_element_type=F32)
    ones_sum = jnp.ones((DV_B, LANES), BF16)
    for h in heads:
        sb, hd = h
        gt = gates[sb]
        a_b = a_bs[h]
        m_b = lanes(col(gt["m_row"], hd))
        big = jnp.maximum(a_b, m_b)
        r_b = jnp.exp(a_b - big)
        g_b = jnp.exp(m_b - big)
        e_b = jnp.exp(-(lanes(col(gt["b4"], hd)) + big))
        p, c = pu[h], qc[h]
        den = r_b * p[:L, DV_B:] + g_b * c[:, DV_B:]
        inv = 1.0 / jnp.maximum(jnp.abs(den), e_b)
        hh = jnp.concatenate([(r_b * p[:L, k * LANES:(k + 1) * LANES] + g_b * c[:, k * LANES:(k + 1) * LANES]) * inv
                              for k in range(DV_B // LANES)], axis=1)
        mu = jnp.dot(hh.astype(BF16), ones_sum, preferred_element_type=F32) * (1.0 / DV_B)
        hc = hh - jnp.concatenate([mu] * (DV_B // LANES), axis=1)
        var = jnp.dot(jnp.square(hc).astype(BF16), ones_sum, preferred_element_type=F32) * (1.0 / DV_B)
        rstd = lax.rsqrt(var + LN_EPS)
        yn = hc * jnp.concatenate([rstd] * (DV_B // LANES), axis=1) * nw_ref[:, hd * DV_B:(hd + 1) * DV_B]
        og = og_ref[sb, :, hd * DV_B:(hd + 1) * DV_B]
        y_ref[sb, :, hd * DV_B:(hd + 1) * DV_B] = (jax.nn.sigmoid(og) * yn).astype(BF16)
        ct_ref[sb, hd] = col(gt["decay4"], hd) * cts[h] + col(gt["rho4"], hd) * pu[h][L:]

    @pl.when(j == pl.num_programs(1) - 1)
    def _():
        for sb, hd in heads:
            c_ref[sb, hd] = ct_ref[sb, hd, :, :DV_B].T
            n_ref[sb, hd] = ct_ref[sb, hd, :, DV_B:DV_B + 1]


def _mlstm_call(qm, kt, vm, og, a_row, b_col, a_col, norm_w, c0, n0, m0, *, SB, L):
    N, T, _ = qm.shape

    def tok(c):
        return pl.BlockSpec((SB, L, c), lambda i, j: (i, j, 0))

    def tr(c):
        return pl.BlockSpec((SB, c, L), lambda i, j: (i, 0, j))

    st_c = pl.BlockSpec((SB, N_HEADS_B, DV_B, DQK_B), lambda i, j: (i, 0, 0, 0))
    st_n = pl.BlockSpec((SB, N_HEADS_B, DQK_B, 1), lambda i, j: (i, 0, 0, 0))
    st_m = pl.BlockSpec((SB, 1, LANES), lambda i, j: (i, 0, 0))
    return pl.pallas_call(
        functools.partial(_mlstm_kernel, SB=SB, L=L),
        grid=(N // SB, T // L),
        in_specs=[tok(QK_B), tr(QK_B), tok(V_B), tok(V_B), tr(8), tok(LANES), tok(LANES),
                  _const_spec((1, V_B)), st_c, st_n, st_m],
        out_specs=[tok(V_B), st_c, st_n, st_m],
        out_shape=[jax.ShapeDtypeStruct((N, T, V_B), BF16),
                   jax.ShapeDtypeStruct((N, N_HEADS_B, DV_B, DQK_B), F32),
                   jax.ShapeDtypeStruct((N, N_HEADS_B, DQK_B, 1), F32),
                   jax.ShapeDtypeStruct((N, 1, LANES), F32)],
        scratch_shapes=[pltpu.VMEM((SB, N_HEADS_B, DQK_B, DV_B + LANES), F32)],
        compiler_params=pltpu.CompilerParams(dimension_semantics=("parallel", "arbitrary")),
        name="mlstm",
    )(qm, kt, vm, og, a_row, b_col, a_col, norm_w, c0, n0, m0)


def _merge_kernel(x_ref, mod_ref, ya_ref, yb_ref, wg_ref,
                  wpa_ref, wpb_ref, wo_ref, l1w_ref, l1b_ref, x1_ref, *, S, R):
    SR = S * R
    g1 = mod_ref[:, 2:3, :]
    h1 = (x_ref[...] * (1.0 + mod_ref[:, 1:2, :]) + mod_ref[:, 0:1, :]).reshape(SR, D_MODEL).astype(BF16)
    gab = jnp.dot(h1, wg_ref[...], preferred_element_type=F32)
    pa = jnp.dot(ya_ref[...].reshape(SR, Q_A), wpa_ref[...], preferred_element_type=F32)
    pb = jnp.dot(yb_ref[...].reshape(SR, V_B), wpb_ref[...], preferred_element_type=F32)
    merged = jax.nn.sigmoid(gab[:, :D_MODEL]) * pa + jax.nn.sigmoid(gab[:, D_MODEL:]) * pb
    merged = merged.astype(BF16).reshape(S, R, D_MODEL)
    for lo in range(0, R, R // EPI_SPLIT):
        rows = slice(lo, lo + R // EPI_SPLIT)
        mo = jnp.dot(merged[:, rows].reshape(S * (R // EPI_SPLIT), D_MODEL), wo_ref[...],
                     preferred_element_type=F32).reshape(S, R // EPI_SPLIT, D_MODEL)
        x1_ref[:, rows, :] = _layer_norm(ALPHA * x_ref[:, rows, :] + g1 * mo, l1w_ref[...], l1b_ref[...])


def _merge_call(x, mod, ya, yb, wg, wpa, wpb, wo, l1w, l1b, *, S, R):
    N, T, _ = x.shape

    def tok(c):
        return pl.BlockSpec((S, R, c), lambda i, j: (i, j, 0))

    return pl.pallas_call(
        functools.partial(_merge_kernel, S=S, R=R),
        grid=(N // S, T // R),
        in_specs=[tok(D_MODEL), pl.BlockSpec((S, 6, D_MODEL), lambda i, j: (i, 0, 0)),
                  tok(Q_A), tok(V_B), _const_spec((D_MODEL, 2 * D_MODEL)),
                  _const_spec((Q_A, D_MODEL)), _const_spec((V_B, D_MODEL)), _const_spec((D_MODEL, D_MODEL)),
                  _const_spec((1, D_MODEL)), _const_spec((1, D_MODEL))],
        out_specs=tok(D_MODEL),
        out_shape=jax.ShapeDtypeStruct((N, T, D_MODEL), F32),
        compiler_params=pltpu.CompilerParams(dimension_semantics=("parallel", "parallel")),
        name="merge",
    )(x, mod, ya, yb, wg, wpa, wpb, wo, l1w, l1b)


def _ffn_kernel(x1_ref, mod_ref, cin_ref, wup_ref, bup_ref, cw_ref, cb_ref, wdn_ref, l2w_ref, l2b_ref,
                y_ref, cout_ref, carry_ref, act_ref, *perm_refs, S, R, permute):
    SR = S * R
    FC = D_FF // FFN_CHUNKS
    NV = R // SUBLANES
    NT = D_MODEL // LANES

    @pl.when(pl.program_id(1) == 0)
    def _():
        carry_ref[...] = cin_ref[...] - bup_ref[...]

    sh2 = mod_ref[:, 3:4, :]
    sc2 = mod_ref[:, 4:5, :]
    g2 = mod_ref[:, 5:6, :]

    def regroup(src_ref, start_of, stride):
        return jnp.concatenate(
            [jnp.concatenate([src_ref[s, c, pl.ds(start_of(v), SUBLANES, stride=stride), :] for c in range(NT)],
                             axis=1) for s in range(S) for v in range(NV)], axis=0).reshape(S, R, D_MODEL)

    if permute:
        (tiles_ref,) = perm_refs
        for c in range(NT):
            tiles_ref[:, c] = x1_ref[:, :, c * LANES:(c + 1) * LANES]
        x1 = regroup(tiles_ref, lambda v: v, NV)
    else:
        x1 = x1_ref[...]
    h2 = (x1 * (1.0 + sc2) + sh2).reshape(SR, D_MODEL).astype(BF16)
    row8 = lax.broadcasted_iota(jnp.int32, (SUBLANES, FC), 0)

    def wrap(group, first):
        return jnp.where(row8 == 0, first, pltpu.roll(group, 1, axis=0))

    def back_permuted(blk, c0, c1):
        g_last = wrap(blk[(NV - 1) * SUBLANES:], c1)
        g_prev = wrap(blk[(NV - 2) * SUBLANES:(NV - 1) * SUBLANES], c0)
        new = jnp.concatenate([blk[(NV - 1) * SUBLANES - 1:(NV - 1) * SUBLANES], blk[R - 1:R]], axis=0)
        return ([g_last, blk[:(NV - 1) * SUBLANES]], [g_prev, g_last, blk[:(NV - 2) * SUBLANES]], new)

    def back_natural(blk, c0, c1):
        b1 = pltpu.roll(blk, 1, axis=0)
        b1 = [jnp.where(row8 == 0, c1, b1[:SUBLANES]), b1[SUBLANES:]]
        b2 = pltpu.roll(blk, 2, axis=0)
        b2 = [jnp.where(row8 == 0, c0, jnp.where(row8 == 1, c1, b2[:SUBLANES])), b2[SUBLANES:]]
        return b1, b2, blk[R - 2:]

    def conv(lo):
        cs = slice(lo, lo + FC)
        u = jnp.dot(h2, wup_ref[:, cs], preferred_element_type=F32)
        w0, w1, w2 = cw_ref[0:1, cs], cw_ref[1:2, cs], cw_ref[2:3, cs]
        const = cb_ref[:, cs] + bup_ref[:, cs] * (w0 + w1 + w2)
        back1, back2 = [], []
        for s in range(S):
            b1, b2, new = (back_permuted if permute else back_natural)(
                u[s * R:(s + 1) * R], carry_ref[s, 0:1, cs], carry_ref[s, 1:2, cs])
            back1 += b1
            back2 += b2
            carry_ref[s, :, cs] = new
            cout_ref[s, :, cs] = new + bup_ref[:, cs]
        return w0 * jnp.concatenate(back2, axis=0) + w1 * jnp.concatenate(back1, axis=0) + w2 * u + const

    for ci in range(FFN_CHUNKS):
        lo = ci * FC
        a = conv(lo)
        g = conv(D_FF + lo)
        act_ref[:, lo:lo + FC] = (a * jax.nn.gelu(g, approximate=True)).astype(BF16)

    parts = []
    for k in range(EPI_SPLIT):
        if S == 1:
            sl_s, sl_r, n = slice(0, 1), slice(k * R // EPI_SPLIT, (k + 1) * R // EPI_SPLIT), R // EPI_SPLIT
            rows = sl_r
        else:
            sl_s, sl_r, n = slice(k * S // EPI_SPLIT, (k + 1) * S // EPI_SPLIT), slice(0, R), SR // EPI_SPLIT
            rows = slice(k * SR // EPI_SPLIT, (k + 1) * SR // EPI_SPLIT)
        f = jnp.dot(act_ref[rows, :], wdn_ref[...], preferred_element_type=F32)
        f = f.reshape((1, n, D_MODEL) if S == 1 else (S // EPI_SPLIT, R, D_MODEL))
        parts.append(_layer_norm(ALPHA * x1[sl_s, sl_r] + g2[sl_s] * f, l2w_ref[...], l2b_ref[...]))
    y = jnp.concatenate(parts, axis=1 if S == 1 else 0)
    if permute:
        for c in range(NT):
            tiles_ref[:, c] = y[:, :, c * LANES:(c + 1) * LANES]
        y = regroup(tiles_ref, lambda v: ((SUBLANES * v) % NV) * SUBLANES + (SUBLANES * v) // NV, SUBLANES)
    y_ref[...] = y


def _ffn_call(x1, mod, conv_in, wup, bup, cw, cb, wdn, l2w, l2b, *, S, R):
    N, T, _ = x1.shape
    permute = (R // SUBLANES) % SUBLANES == 0

    def tok(c):
        return pl.BlockSpec((S, R, c), lambda i, j: (i, j, 0))

    conv_spec = pl.BlockSpec((S, CONV_W - 1, 2 * D_FF), lambda i, j: (i, 0, 0))
    return pl.pallas_call(
        functools.partial(_ffn_kernel, S=S, R=R, permute=permute),
        grid=(N // S, T // R),
        in_specs=[tok(D_MODEL), pl.BlockSpec((S, 6, D_MODEL), lambda i, j: (i, 0, 0)), conv_spec,
                  _const_spec((D_MODEL, 2 * D_FF)), _const_spec((1, 2 * D_FF)),
                  _const_spec((CONV_W, 2 * D_FF)), _const_spec((1, 2 * D_FF)),
                  _const_spec((D_FF, D_MODEL)), _const_spec((1, D_MODEL)), _const_spec((1, D_MODEL))],
        out_specs=[tok(D_MODEL), conv_spec],
        out_shape=[jax.ShapeDtypeStruct((N, T, D_MODEL), F32),
                   jax.ShapeDtypeStruct((N, CONV_W - 1, 2 * D_FF), F32)],
        scratch_shapes=[pltpu.VMEM((S, CONV_W - 1, 2 * D_FF), F32), pltpu.VMEM((S * R, D_FF), BF16)]
        + ([pltpu.VMEM((S, D_MODEL // LANES, R, LANES), F32)] if permute else []),
        compiler_params=pltpu.CompilerParams(dimension_semantics=("parallel", "arbitrary")),
        name="ffn",
    )(x1, mod, conv_in, wup, bup, cw, cb, wdn, l2w, l2b)


def kernel(x_prompt, x_sample, cache_attn_k, cache_attn_v, state_mlstm_C, state_mlstm_n, state_mlstm_m, state_ffn_conv, c_prompt, c_sample, w_ada, b_ada, w_in, b_igate, b_fgate, attn_sinks, mlstm_norm_w, w_proj_a, w_proj_b, w_out, ln1_w, ln1_b, w_up, b_up, conv_w, conv_b, w_down, ln2_w, ln2_b):
    Bp, Tp, D = x_prompt.shape
    Bs, Ts, _ = x_sample.shape
    l = 0

    mod = _mod_call(jnp.concatenate([c_prompt, c_sample], axis=0), w_ada[l], b_ada[l][None])
    mod_p = mod[:Bp].reshape(Bp, 6, D)
    mod_s = mod[Bp:].reshape(Bs, 6, D)

    w = w_in[l]
    o_i, o_f, o_o = Z_OFF[6], Z_OFF[7], Z_OFF[8]
    lane_pad = ((0, 0), (0, LANES - N_HEADS_B))
    w_main = jnp.concatenate([w[:, :o_i], w[:, o_o:Z_OFF[9]],
                              jnp.pad(w[:, o_i:o_f], lane_pad), jnp.pad(w[:, o_f:o_o], lane_pad)],
                             axis=1).astype(BF16)
    b_ic = jnp.pad(b_igate[l][None], lane_pad)
    b_fc = jnp.pad(b_fgate[l][None], lane_pad)
    inproj_w = (w_main, b_ic, b_fc)
    w_gates = w[:, Z_OFF[9]:].astype(BF16)
    sinks = attn_sinks[l]
    norm_w = mlstm_norm_w[l][None]
    merge_w = (w_proj_a[l].astype(BF16), w_proj_b[l].astype(BF16), w_out[l].astype(BF16),
               ln1_w[l][None], ln1_b[l][None])
    ffn_w = (w_up[l].astype(BF16), b_up[l][None], conv_w[l], conv_b[l][None],
             w_down[l].astype(BF16), ln2_w[l][None], ln2_b[l][None])

    qa, ka, va, qm, kt, vm, og, a_row, b_col, a_col = _inproj_call(
        x_prompt, mod_p, *inproj_w, S=1, R=INPROJ_ROWS, L=MLSTM_CHUNK, q_transposed=True)
    ya = _attn_prompt_call(qa, ka, va, sinks)
    yb, p_c, p_n, p_m = _mlstm_call(
        qm, kt, vm, og, a_row, b_col, a_col, norm_w,
        jnp.zeros((Bp, N_HEADS_B, DV_B, DQK_B), F32), jnp.zeros((Bp, N_HEADS_B, DQK_B, 1), F32),
        jnp.zeros((Bp, 1, LANES), F32), SB=4, L=MLSTM_CHUNK)
    p_m = p_m[:, 0, :N_HEADS_B]
    x1 = _merge_call(x_prompt, mod_p, ya, yb, w_gates, *merge_w, S=1, R=POST_ROWS)
    y_prompt, p_conv = _ffn_call(x1, mod_p, jnp.zeros((Bp, CONV_W - 1, 2 * D_FF), F32), *ffn_w, S=1, R=POST_ROWS)
    p_k = ka[:, Tp - KV_WIN:].reshape(Bp, KV_WIN, N_KV_A, HEAD_DIM_A)
    p_v = va[:, Tp - KV_WIN:].reshape(Bp, KV_WIN, N_KV_A, HEAD_DIM_A)

    qa, ka, va, qm, kt, vm, og, a_row, b_col, a_col = _inproj_call(
        x_sample, mod_s, *inproj_w, S=16, R=Ts, L=Ts, q_transposed=False)
    ya, s_k, s_v = _attn_sample_call(qa, ka, va, cache_attn_k[l].reshape(Bs, KV_WIN, KV_A_W),
                                     cache_attn_v[l].reshape(Bs, KV_WIN, KV_A_W), sinks, NS=8)
    yb, s_c, s_n, s_m = _mlstm_call(
        qm, kt, vm, og, a_row, b_col, a_col, norm_w,
        state_mlstm_C[l], state_mlstm_n[l][:, :, :, None], jnp.pad(state_mlstm_m[l][:, None, :], ((0, 0),) + lane_pad),
        SB=4, L=Ts)
    s_m = s_m[:, 0, :N_HEADS_B]
    x1 = _merge_call(x_sample, mod_s, ya, yb, w_gates, *merge_w, S=16, R=Ts)
    y_sample, s_conv = _ffn_call(x1, mod_s, state_ffn_conv[l], *ffn_w, S=8, R=Ts)

    def st(a, shape):
        return a.reshape((1,) + shape)

    return (y_prompt, y_sample,
            st(p_k, (Bp, KV_WIN, N_KV_A, HEAD_DIM_A)), st(p_v, (Bp, KV_WIN, N_KV_A, HEAD_DIM_A)),
            st(p_c, (Bp, N_HEADS_B, DV_B, DQK_B)), st(p_n, (Bp, N_HEADS_B, DQK_B)), st(p_m, (Bp, N_HEADS_B)),
            st(p_conv, (Bp, CONV_W - 1, 2 * D_FF)),
            st(s_k, (Bs, KV_WIN, N_KV_A, HEAD_DIM_A)), st(s_v, (Bs, KV_WIN, N_KV_A, HEAD_DIM_A)),
            st(s_c, (Bs, N_HEADS_B, DV_B, DQK_B)), st(s_n, (Bs, N_HEADS_B, DQK_B)), st(s_m, (Bs, N_HEADS_B)),
            st(s_conv, (Bs, CONV_W - 1, 2 * D_FF)))
```
